```python
import math, functools
import jax, jax.numpy as jnp
from jax import lax
import numpy as np

D_MODEL = 4096
BATCH = 4
SEQ = 2048
DEPTH = 2
DEC_BATCH = 8
DEC_SEQ = 4
PAST_LEN = 16384
PAGE_SIZE = 128

D_MIX = D_MODEL
W_A = D_MIX // 4
C_A = 16
G_A = W_A // C_A
N_A = 64
W_B = D_MIX // 4
DK_B = 128
DV_B = 128
H_B = W_B // DK_B
W_C = D_MIX // 4
DV_C = 128
DH_C = DV_C // 2
H_C = W_C // DV_C
W_M = D_MIX - W_A - W_B - W_C
H_M = 4
DH_M = W_M // H_M
N_MEM = 256
HGRN_CHUNK = 64
Q_BLOCK = 128
EPS = 1e-6
DT_MIN = 1e-3
DT_MAX = 1e-1
GROUP_SIZES = (W_A, W_A, W_B, W_B, W_B, W_B, W_C, W_C, W_C, W_C, W_M, W_M)
SPLITS = tuple(sum(GROUP_SIZES[:i + 1]) for i in range(len(GROUP_SIZES) - 1))
D_IN = sum(GROUP_SIZES)

kernel_name = 'hybrid_s5_hgrn2_diffattn_decode_step'


def rmsnorm(x, g):
    xf = x.astype(jnp.float32)
    y = xf * lax.rsqrt(jnp.mean(xf * xf, axis=-1, keepdims=True) + EPS)
    return (y * g.astype(jnp.float32)).astype(x.dtype)


def _s5_combine(e1, e2):
    a1, b1 = e1
    a2, b2 = e2
    return a1 * a2, a2 * b1 + b2


def s5_branch(u, h0_re, h0_im, lam_re, lam_im, log_dt, b_re, b_im, c_re, c_im, d, w_glu, b_glu):
    bsz, t = u.shape[:2]
    f32 = jnp.float32
    uf = u.astype(f32).reshape(bsz, t, G_A, C_A)
    lam = lax.complex(lam_re.astype(f32), lam_im.astype(f32))
    dt = jnp.exp(log_dt.astype(f32))[:, None]
    a_bar = jnp.exp(lam * dt)
    b_c = lax.complex(b_re.astype(f32), b_im.astype(f32))
    b_bar = ((a_bar - 1.0) / lam)[..., None] * b_c
    c_c = lax.complex(c_re.astype(f32), c_im.astype(f32))
    bu = jnp.einsum('btgc,gnc->btgn', uf.astype(jnp.complex64), b_bar)
    h0 = lax.complex(h0_re.astype(f32), h0_im.astype(f32))
    bu = bu.at[:, 0].add(a_bar * h0)
    a_seq = jnp.broadcast_to(a_bar, bu.shape)
    _, h = lax.associative_scan(_s5_combine, (a_seq, bu), axis=1)
    y = jnp.einsum('btgn,gcn->btgc', h, c_c).real + d.astype(f32).reshape(G_A, C_A) * uf
    y = jax.nn.gelu(y.reshape(bsz, t, W_A))
    y = y * jax.nn.sigmoid(y @ w_glu.astype(f32) + b_glu.astype(f32))
    h_last = h[:, -1]
    return y, h_last.real.astype(h0_re.dtype), h_last.imag.astype(h0_im.dtype)


def hgrn_branch(q, f_pre, inp, s0, lb, norm_g):
    bsz, t = q.shape[:2]
    f32 = jnp.float32
    chunk = math.gcd(t, HGRN_CHUNK)
    n_chunks = t // chunk
    f = lb + (1.0 - lb) * jax.nn.sigmoid(f_pre.astype(f32))
    log_f = jnp.log(f)
    k = 1.0 - f

    def heads(a, dh):
        return a.astype(f32).reshape(bsz, n_chunks, chunk, H_B, dh).transpose(1, 0, 3, 2, 4)

    qh, kh, lfh, ih = heads(q, DK_B), heads(k, DK_B), heads(log_f, DK_B), heads(inp, DV_B)
    causal = jnp.tril(jnp.ones((chunk, chunk), dtype=bool))

    def step(s, xs):
        qc, kc, lfc, ic = xs
        cum = jnp.cumsum(lfc, axis=2)
        o_inter = jnp.einsum('bhtk,bhkv->bhtv', qc * jnp.exp(cum), s)
        rel = cum[:, :, :, None, :] - cum[:, :, None, :, :]
        decay = jnp.exp(jnp.where(causal[:, :, None], rel, -jnp.inf))
        scores = jnp.einsum('bhtk,bhsk,bhtsk->bhts', qc, kc, decay)
        o_intra = jnp.einsum('bhts,bhsv->bhtv', scores, ic)
        last = cum[:, :, -1:, :]
        s_new = (jnp.exp(last[:, :, 0, :])[..., None] * s
                 + jnp.einsum('bhsk,bhsv->bhkv', kc * jnp.exp(last - cum), ic))
        return s_new, o_inter + o_intra

    s_last, o = lax.scan(step, s0.astype(f32), (qh, kh, lfh, ih))
    o = o.transpose(1, 0, 3, 2, 4).reshape(bsz, t, H_B, DV_B)
    o = rmsnorm(o, norm_g).reshape(bsz, t, W_B)
    return o, s_last.astype(s0.dtype)


def diff_attend(q, k, v, mask, lam):
    s = jnp.einsum('bqhjd,bkhjd->bhjqk', q.astype(jnp.float32), k.astype(jnp.float32)) * DH_C ** -0.5
    s = jnp.where(mask, s, -jnp.inf)
    p = jax.nn.softmax(s, axis=-1)
    attn = p[:, :, 0] - lam * p[:, :, 1]
    return jnp.einsum('bhqk,bkhv->bqhv', attn, v.astype(jnp.float32))


def diff_attn_prompt(q, k, v, lam):
    bsz, t = q.shape[:2]
    blk = math.gcd(t, Q_BLOCK)
    n_blk = t // blk
    qb = q.reshape(bsz, n_blk, blk, H_C, 2, DH_C).transpose(1, 0, 2, 3, 4, 5)
    key_pos = jnp.arange(t)

    def one_block(args):
        b_idx, q_blk = args
        q_pos = b_idx * blk + jnp.arange(blk)
        mask = key_pos[None, :] <= q_pos[:, None]
        return diff_attend(q_blk, k, v, mask, lam)

    o = lax.map(one_block, (jnp.arange(n_blk), qb))
    return o.transpose(1, 0, 2, 3, 4).reshape(bsz, t, H_C, DV_C)


def diff_attn_sample(q, k, v, lam, k_past, v_past):
    p_len = k_past.shape[1]
    t = q.shape[1]
    k_all = jnp.concatenate([k_past.astype(k.dtype), k], axis=1)
    v_all = jnp.concatenate([v_past.astype(v.dtype), v], axis=1)
    q_pos = p_len + jnp.arange(t)
    key_pos = jnp.arange(p_len + t)
    mask = key_pos[None, :] <= q_pos[:, None]
    return diff_attend(q, k_all, v_all, mask, lam)


def mem_kv(mem, g, w):
    bsz, n_mem = mem.shape[:2]
    kv = rmsnorm(mem, g) @ w
    k, v = jnp.split(kv, 2, axis=-1)
    return k.reshape(bsz, n_mem, H_M, DH_M), v.reshape(bsz, n_mem, H_M, DH_M)


def mem_attend(q, mem_k, mem_v):
    bsz, t = q.shape[:2]
    qh = q.astype(jnp.float32).reshape(bsz, t, H_M, DH_M)
    s = jnp.einsum('bqhd,bkhd->bhqk', qh, mem_k.astype(jnp.float32)) * DH_M ** -0.5
    p = jax.nn.softmax(s, axis=-1)
    o = jnp.einsum('bhqk,bkhd->bqhd', p, mem_v.astype(jnp.float32))
    return o.reshape(bsz, t, W_M)


def trunk_layer(x, p, attend, mem_k, mem_v, s5_re0, s5_im0, hgrn_s0):
    bsz, t = x.shape[:2]
    h = rmsnorm(x, p['norm_g'])
    z = h @ p['w_in']
    u_a, g_a, q_b, f_b, i_b, g_b, q_c, k_c, v_c, g_c, q_m, g_m = jnp.split(z, SPLITS, axis=-1)
    y_a, s5_re, s5_im = s5_branch(u_a, s5_re0, s5_im0, p['s5_lambda_re'], p['s5_lambda_im'],
                                  p['s5_log_dt'], p['s5_b_re'], p['s5_b_im'], p['s5_c_re'],
                                  p['s5_c_im'], p['s5_d'], p['s5_w_glu'], p['s5_b_glu'])
    y_b, hgrn_s = hgrn_branch(q_b, f_b, i_b, hgrn_s0, p['lb'], p['hgrn_norm_g'])
    q_heads = q_c.reshape(bsz, t, H_C, 2, DH_C)
    k_heads = k_c.reshape(bsz, t, H_C, 2, DH_C)
    v_heads = v_c.reshape(bsz, t, H_C, DV_C)
    o_c = attend(q_heads, k_heads, v_heads, p['lam'])
    y_c = (rmsnorm(o_c, p['diff_norm_g']) * (1.0 - p['lam_init'])).reshape(bsz, t, W_C)
    y_m = mem_attend(q_m, mem_k, mem_v)
    mixed = jnp.concatenate([y_a * jax.nn.silu(g_a), y_b * jax.nn.silu(g_b),
                             y_c * jax.nn.silu(g_c), y_m * jax.nn.silu(g_m)], axis=-1).astype(x.dtype)
    x_out = x + mixed @ p['w_out']
    new_k = k_c.reshape(bsz, t, H_C, 2 * DH_C)
    return x_out, new_k, v_heads, s5_re, s5_im, hgrn_s


def setup_inputs(seed: int = 0) -> dict:
    key = jax.random.key(seed)
    kit = iter(jax.random.split(key, 40))
    f32 = jnp.float32

    def nrm(shape, scale):
        return scale * jax.random.normal(next(kit), shape, f32)

    n_pages = PAST_LEN // PAGE_SIZE
    n_used = DEC_BATCH * n_pages
    n_pool = n_used + (n_used + 3) // 4
    perm = jax.random.permutation(next(kit), n_pool)
    page_table = perm[:n_used].reshape(DEC_BATCH, n_pages).astype(jnp.int32)

    x_prompt = nrm((BATCH, SEQ, D_MODEL), 1.0)
    x_sample = nrm((DEC_BATCH, DEC_SEQ, D_MODEL), 1.0)
    cache_k = nrm((DEPTH, n_pool, PAGE_SIZE, H_C, 2 * DH_C), 1.0)
    cache_v = nrm((DEPTH, n_pool, PAGE_SIZE, H_C, DV_C), 1.0)
    cache_mem_k = nrm((DEPTH, DEC_BATCH, N_MEM, H_M, DH_M), 1.0)
    cache_mem_v = nrm((DEPTH, DEC_BATCH, N_MEM, H_M, DH_M), 1.0)
    state_s5_re = nrm((DEPTH, DEC_BATCH, G_A, N_A), 0.5)
    state_s5_im = nrm((DEPTH, DEC_BATCH, G_A, N_A), 0.5)
    state_hgrn = nrm((DEPTH, DEC_BATCH, H_B, DK_B, DV_B), 0.5)
    mem_prompt = nrm((BATCH, N_MEM, D_MODEL), 1.0)

    norm_g = 1.0 + nrm((DEPTH, D_MODEL), 0.02)
    w_in = nrm((DEPTH, D_MODEL, D_IN), D_MODEL ** -0.5)
    s5_lambda_re = -0.5 + nrm((DEPTH, G_A, N_A), 0.01)
    s5_lambda_im = math.pi * jnp.arange(N_A, dtype=f32) + nrm((DEPTH, G_A, N_A), 0.01)
    s5_log_dt = jax.random.uniform(next(kit), (DEPTH, G_A), f32,
                                   minval=math.log(DT_MIN), maxval=math.log(DT_MAX))
    s5_b_re = nrm((DEPTH, G_A, N_A, C_A), (2 * C_A) ** -0.5)
    s5_b_im = nrm((DEPTH, G_A, N_A, C_A), (2 * C_A) ** -0.5)
    s5_c_re = nrm((DEPTH, G_A, C_A, N_A), (2 * N_A) ** -0.5)
    s5_c_im = nrm((DEPTH, G_A, C_A, N_A), (2 * N_A) ** -0.5)
    s5_d = nrm((DEPTH, W_A), 1.0)
    s5_w_glu = nrm((DEPTH, W_A, W_A), W_A ** -0.5)
    s5_b_glu = nrm((DEPTH, W_A), 0.01)
    hgrn_lower_bounds = nrm((DEPTH, W_B), 0.1)
    hgrn_norm_g = 1.0 + nrm((DEPTH, DV_B), 0.02)
    diff_lq1 = nrm((DEPTH, DH_C), 0.1)
    diff_lk1 = nrm((DEPTH, DH_C), 0.1)
    diff_lq2 = nrm((DEPTH, DH_C), 0.1)
    diff_lk2 = nrm((DEPTH, DH_C), 0.1)
    diff_norm_g = 1.0 + nrm((DEPTH, DV_C), 0.02)
    mem_norm_g = 1.0 + nrm((DEPTH, D_MODEL), 0.02)
    w_mem_kv = nrm((DEPTH, D_MODEL, 2 * W_M), D_MODEL ** -0.5)
    w_out = nrm((DEPTH, D_MIX, D_MODEL), D_MIX ** -0.5)
    final_norm_g = 1.0 + nrm((D_MODEL,), 0.02)
    return {'x_prompt': x_prompt, 'x_sample': x_sample, 'cache_k': cache_k, 'cache_v': cache_v,
            'cache_mem_k': cache_mem_k, 'cache_mem_v': cache_mem_v,
            'state_s5_re': state_s5_re, 'state_s5_im': state_s5_im, 'state_hgrn': state_hgrn,
            'page_table': page_table, 'mem_prompt': mem_prompt,
            'norm_g': norm_g, 'w_in': w_in, 's5_lambda_re': s5_lambda_re, 's5_lambda_im': s5_lambda_im,
            's5_log_dt': s5_log_dt, 's5_b_re': s5_b_re, 's5_b_im': s5_b_im, 's5_c_re': s5_c_re,
            's5_c_im': s5_c_im, 's5_d': s5_d, 's5_w_glu': s5_w_glu, 's5_b_glu': s5_b_glu,
            'hgrn_lower_bounds': hgrn_lower_bounds, 'hgrn_norm_g': hgrn_norm_g,
            'diff_lq1': diff_lq1, 'diff_lk1': diff_lk1, 'diff_lq2': diff_lq2, 'diff_lk2': diff_lk2,
            'diff_norm_g': diff_norm_g, 'mem_norm_g': mem_norm_g, 'w_mem_kv': w_mem_kv,
            'w_out': w_out, 'final_norm_g': final_norm_g}


def reference(x_prompt, x_sample, cache_k, cache_v, cache_mem_k, cache_mem_v,
              state_s5_re, state_s5_im, state_hgrn, page_table, mem_prompt,
              norm_g, w_in, s5_lambda_re, s5_lambda_im, s5_log_dt, s5_b_re, s5_b_im,
              s5_c_re, s5_c_im, s5_d, s5_w_glu, s5_b_glu, hgrn_lower_bounds, hgrn_norm_g,
              diff_lq1, diff_lk1, diff_lq2, diff_lk2, diff_norm_g, mem_norm_g, w_mem_kv,
              w_out, final_norm_g):
    f32 = jnp.float32
    bsz_p = x_prompt.shape[0]
    bsz_s = x_sample.shape[0]
    past_len = page_table.shape[1] * cache_k.shape[2]
    lb_all = jnp.cumsum(jax.nn.softmax(hgrn_lower_bounds.astype(f32), axis=0), axis=0)
    lb_all = lb_all - lb_all[0]
    zeros_s5 = jnp.zeros((bsz_p, G_A, N_A), f32)
    zeros_hgrn = jnp.zeros((bsz_p, H_B, DK_B, DV_B), f32)

    xp, xs = x_prompt, x_sample
    kp_l, vp_l, ks_l, vs_l, mkp_l, mvp_l = [], [], [], [], [], []
    s5rp_l, s5ip_l, s5rs_l, s5is_l, hgp_l, hgs_l = [], [], [], [], [], []
    for l in range(DEPTH):
        lam_init = 0.8 - 0.6 * math.exp(-0.3 * l)
        lam = (jnp.exp(jnp.sum(diff_lq1[l].astype(f32) * diff_lk1[l].astype(f32)))
               - jnp.exp(jnp.sum(diff_lq2[l].astype(f32) * diff_lk2[l].astype(f32))) + lam_init)
        p = {'norm_g': norm_g[l], 'w_in': w_in[l],
             's5_lambda_re': s5_lambda_re[l], 's5_lambda_im': s5_lambda_im[l], 's5_log_dt': s5_log_dt[l],
             's5_b_re': s5_b_re[l], 's5_b_im': s5_b_im[l], 's5_c_re': s5_c_re[l], 's5_c_im': s5_c_im[l],
             's5_d': s5_d[l], 's5_w_glu': s5_w_glu[l], 's5_b_glu': s5_b_glu[l],
             'lb': lb_all[l], 'hgrn_norm_g': hgrn_norm_g[l],
             'lam': lam, 'lam_init': lam_init, 'diff_norm_g': diff_norm_g[l], 'w_out': w_out[l]}
        mk_p, mv_p = mem_kv(mem_prompt, mem_norm_g[l], w_mem_kv[l])
        xp, k_new, v_new, s5r, s5i, hg = trunk_layer(xp, p, diff_attn_prompt, mk_p, mv_p,
                                                     zeros_s5, zeros_s5, zeros_hgrn)
        kp_l.append(k_new); vp_l.append(v_new); mkp_l.append(mk_p); mvp_l.append(mv_p)
        s5rp_l.append(s5r); s5ip_l.append(s5i); hgp_l.append(hg)
        k_past = cache_k[l, page_table].reshape(bsz_s, past_len, H_C, 2, DH_C)
        v_past = cache_v[l, page_table].reshape(bsz_s, past_len, H_C, DV_C)
        attend_s = functools.partial(diff_attn_sample, k_past=k_past, v_past=v_past)
        xs, k_new, v_new, s5r, s5i, hg = trunk_layer(xs, p, attend_s, cache_mem_k[l], cache_mem_v[l],
                                                     state_s5_re[l], state_s5_im[l], state_hgrn[l])
        ks_l.append(k_new); vs_l.append(v_new)
        s5rs_l.append(s5r); s5is_l.append(s5i); hgs_l.append(hg)

    y_prompt = rmsnorm(xp, final_norm_g)
    y_sample = rmsnorm(xs, final_norm_g)
    new_k_prompt = jnp.stack(kp_l)
    new_v_prompt = jnp.stack(vp_l)
    new_k_sample = jnp.stack(ks_l)
    new_v_sample = jnp.stack(vs_l)
    new_mem_k_prompt = jnp.stack(mkp_l)
    new_mem_v_prompt = jnp.stack(mvp_l)
    new_s5_re_prompt = jnp.stack(s5rp_l)
    new_s5_im_prompt = jnp.stack(s5ip_l)
    new_s5_re_sample = jnp.stack(s5rs_l)
    new_s5_im_sample = jnp.stack(s5is_l)
    new_hgrn_prompt = jnp.stack(hgp_l)
    new_hgrn_sample = jnp.stack(hgs_l)
    return (y_prompt, y_sample, new_k_prompt, new_v_prompt, new_k_sample, new_v_sample,
            new_mem_k_prompt, new_mem_v_prompt, new_s5_re_prompt, new_s5_im_prompt,
            new_s5_re_sample, new_s5_im_sample, new_hgrn_prompt, new_hgrn_sample)
```

```python
import functools
import math

import jax
import jax.numpy as jnp
from jax import lax
from jax.experimental import pallas as pl
from jax.experimental.pallas import tpu as pltpu

F32 = jnp.float32
BF16 = jnp.bfloat16

D_MODEL = 4096
GROUP_W = 1024
N_GROUPS_IN = 12
C_A = 16
G_A = GROUP_W // C_A
N_A = 64
S5_GROUPS_PER_DOT = 16
S5_CHUNKS = G_A // S5_GROUPS_PER_DOT
S5_STATE = G_A * N_A
DK_B = 128
DV_B = 128
H_B = GROUP_W // DK_B
DV_C = 128
DH_C = DV_C // 2
H_C = GROUP_W // DV_C
H_M = 4
DH_M = GROUP_W // H_M
EPS = 1e-6
HGRN_SUB = 16
SUBLANES = 8
LANES = 128
VMEM_LIMIT = 56 * 1024 * 1024

(G_UA, G_GA, G_QB, G_FB, G_IB, G_GB, G_QC, G_KC, G_VC, G_GC, G_QM, G_GM) = range(N_GROUPS_IN)


def _params(*sem):
    return pltpu.CompilerParams(dimension_semantics=sem, vmem_limit_bytes=VMEM_LIMIT)


def _silu(x):
    return x * jax.nn.sigmoid(x)


def _row_tile(m, cap):
    return cap if m % cap == 0 else m


def _rmsnorm_kernel(x_ref, g_ref, o_ref):
    x = x_ref[...]
    y = x * lax.rsqrt(jnp.mean(x * x, axis=-1, keepdims=True) + EPS)
    o_ref[...] = (y * g_ref[...]).astype(o_ref.dtype)


def rmsnorm(x, g, out_dtype):
    m, d = x.shape
    tm = _row_tile(m, 256)
    return pl.pallas_call(
        _rmsnorm_kernel,
        grid=(m // tm,),
        in_specs=[pl.BlockSpec((tm, d), lambda i: (i, 0)),
                  pl.BlockSpec((1, d), lambda i: (0, 0))],
        out_specs=pl.BlockSpec((tm, d), lambda i: (i, 0)),
        out_shape=jax.ShapeDtypeStruct((m, d), out_dtype),
        compiler_params=_params("parallel"),
        name="rmsnorm",
    )(x, g.reshape(1, d))


def _matmul_kernel(a_ref, b_ref, o_ref, acc_ref):
    k = pl.program_id(2)

    @pl.when(k == 0)
    def _():
        acc_ref[...] = jnp.zeros_like(acc_ref)

    acc_ref[...] += jnp.dot(a_ref[...], b_ref[...], preferred_element_type=F32)

    @pl.when(k == pl.num_programs(2) - 1)
    def _():
        o_ref[...] = acc_ref[...]


def matmul_groups(a, b):
    m, kd = a.shape
    n = b.shape[1]
    tm = _row_tile(m, 1024)
    tn = GROUP_W
    tk = 2048
    return pl.pallas_call(
        _matmul_kernel,
        grid=(m // tm, n // tn, kd // tk),
        in_specs=[pl.BlockSpec((tm, tk), lambda i, j, k: (i, k)),
                  pl.BlockSpec((tk, tn), lambda i, j, k: (k, j))],
        out_specs=pl.BlockSpec((None, tm, tn), lambda i, j, k: (j, i, 0)),
        out_shape=jax.ShapeDtypeStruct((n // tn, m, tn), F32),
        scratch_shapes=[pltpu.VMEM((tm, tn), F32)],
        compiler_params=_params("parallel", "parallel", "arbitrary"),
        name="matmul_groups",
    )(a, b)


def _outproj_kernel(a0_ref, a1_ref, a2_ref, a3_ref, w_ref, x_ref, o_ref):
    acc = x_ref[...]
    for g, a_ref in enumerate((a0_ref, a1_ref, a2_ref, a3_ref)):
        acc += jnp.dot(a_ref[...].astype(BF16), w_ref[g * GROUP_W:(g + 1) * GROUP_W, :],
                       preferred_element_type=F32)
    o_ref[...] = acc


def outproj(parts, w, x):
    m, d = x.shape
    tm = _row_tile(m, 1024)
    tn = 512
    part_spec = pl.BlockSpec((tm, GROUP_W), lambda i, j: (i, 0))
    return pl.pallas_call(
        _outproj_kernel,
        grid=(m // tm, d // tn),
        in_specs=[part_spec, part_spec, part_spec, part_spec,
                  pl.BlockSpec((4 * GROUP_W, tn), lambda i, j: (0, j)),
                  pl.BlockSpec((tm, tn), lambda i, j: (i, j))],
        out_specs=pl.BlockSpec((tm, tn), lambda i, j: (i, j)),
        out_shape=jax.ShapeDtypeStruct((m, d), F32),
        compiler_params=_params("parallel", "parallel"),
        name="outproj",
    )(*parts, w, x)


def _s5_bu_kernel(u_ref, w_ref, o_ref):
    r = jnp.dot(u_ref[...].astype(BF16), w_ref[...], preferred_element_type=F32)
    half = r.shape[1] // 2
    o_ref[0] = r[:, :half]
    o_ref[1] = r[:, half:]


def s5_bu(z, w_bu):
    m = z.shape[1]
    tm = _row_tile(m, 512)
    kw = S5_GROUPS_PER_DOT * C_A
    nw = S5_GROUPS_PER_DOT * N_A
    return pl.pallas_call(
        _s5_bu_kernel,
        grid=(m // tm, S5_CHUNKS),
        in_specs=[pl.BlockSpec((None, tm, kw), lambda i, k: (G_UA, i, k)),
                  pl.BlockSpec((None, kw, 2 * nw), lambda i, k: (k, 0, 0))],
        out_specs=pl.BlockSpec((2, tm, nw), lambda i, k: (0, i, k)),
        out_shape=jax.ShapeDtypeStruct((2, m, S5_STATE), F32),
        compiler_params=_params("parallel", "parallel"),
        name="s5_bu",
    )(z, w_bu)


def _s5_scan_kernel(bu_ref, a_ref, h0_ref, h_ref, hl_ref, st_ref, *, tt):
    j = pl.program_id(1)

    @pl.when(j == 0)
    def _():
        st_ref[...] = h0_ref[...]

    a_re = a_ref[0]
    a_im = a_ref[1]

    def body(t, carry):
        h_re, h_im = carry
        n_re = a_re * h_re - a_im * h_im + bu_ref[0, t]
        n_im = a_re * h_im + a_im * h_re + bu_ref[1, t]
        h_ref[0, t] = n_re
        h_ref[1, t] = n_im
        return n_re, n_im

    h_re, h_im = lax.fori_loop(0, tt, body, (st_ref[0], st_ref[1]), unroll=min(tt, 8))
    st_ref[0] = h_re
    st_ref[1] = h_im

    @pl.when(j == pl.num_programs(1) - 1)
    def _():
        hl_ref[0] = h_re
        hl_ref[1] = h_im


def s5_scan(bu, a_bar, h0, bsz, t):
    rows = S5_STATE // LANES
    bu5 = bu.reshape(2, bsz, t, rows, LANES)
    tt = _row_tile(t, 128)
    h, h_last = pl.pallas_call(
        functools.partial(_s5_scan_kernel, tt=tt),
        grid=(bsz, t // tt),
        in_specs=[pl.BlockSpec((2, None, tt, rows, LANES), lambda b, j: (0, b, j, 0, 0)),
                  pl.BlockSpec((2, rows, LANES), lambda b, j: (0, 0, 0)),
                  pl.BlockSpec((2, None, rows, LANES), lambda b, j: (0, b, 0, 0))],
        out_specs=[pl.BlockSpec((2, None, tt, rows, LANES), lambda b, j: (0, b, j, 0, 0)),
                   pl.BlockSpec((2, None, rows, LANES), lambda b, j: (0, b, 0, 0))],
        out_shape=[jax.ShapeDtypeStruct((2, bsz, t, rows, LANES), F32),
                   jax.ShapeDtypeStruct((2, bsz, rows, LANES), F32)],
        scratch_shapes=[pltpu.VMEM((2, rows, LANES), F32)],
        compiler_params=_params("parallel", "arbitrary"),
        name="s5_scan",
    )(bu5, a_bar.reshape(2, rows, LANES), h0.reshape(2, bsz, rows, LANES))
    return h.reshape(2, bsz * t, S5_STATE), h_last.reshape(2, bsz, G_A, N_A)


def _s5_y_kernel(h_ref, u_ref, g_ref, wc_ref, d_ref, wg_ref, bg_ref, o_ref):
    nw = S5_GROUPS_PER_DOT * N_A
    parts = []
    for k in range(S5_CHUNKS):
        h_re = h_ref[0, :, k * nw:(k + 1) * nw].astype(BF16)
        h_im = h_ref[1, :, k * nw:(k + 1) * nw].astype(BF16)
        parts.append(jnp.dot(h_re, wc_ref[0, k], preferred_element_type=F32)
                     + jnp.dot(h_im, wc_ref[1, k], preferred_element_type=F32))
    y = jnp.concatenate(parts, axis=-1) + d_ref[...] * u_ref[...]
    y = jax.nn.gelu(y)
    glu = jnp.dot(y.astype(BF16), wg_ref[...], preferred_element_type=F32) + bg_ref[...]
    y = y * jax.nn.sigmoid(glu)
    o_ref[...] = (y * _silu(g_ref[...])).astype(o_ref.dtype)


def s5_y(h, z, wc, d, w_glu, b_glu, out_dtype):
    m = z.shape[1]
    tm = _row_tile(m, 256)
    kw = S5_GROUPS_PER_DOT * C_A
    nw = S5_GROUPS_PER_DOT * N_A
    return pl.pallas_call(
        _s5_y_kernel,
        grid=(m // tm,),
        in_specs=[pl.BlockSpec((2, tm, S5_STATE), lambda i: (0, i, 0)),
                  pl.BlockSpec((None, tm, GROUP_W), lambda i: (G_UA, i, 0)),
                  pl.BlockSpec((None, tm, GROUP_W), lambda i: (G_GA, i, 0)),
                  pl.BlockSpec((2, S5_CHUNKS, nw, kw), lambda i: (0, 0, 0, 0)),
                  pl.BlockSpec((1, GROUP_W), lambda i: (0, 0)),
                  pl.BlockSpec((GROUP_W, GROUP_W), lambda i: (0, 0)),
                  pl.BlockSpec((1, GROUP_W), lambda i: (0, 0))],
        out_specs=pl.BlockSpec((tm, GROUP_W), lambda i: (i, 0)),
        out_shape=jax.ShapeDtypeStruct((m, GROUP_W), out_dtype),
        compiler_params=_params("parallel"),
        name="s5_y",
    )(h, z, z, wc, d.reshape(1, GROUP_W), w_glu, b_glu.reshape(1, GROUP_W))


def s5_weights(lam_re, lam_im, log_dt, b_re, b_im, c_re, c_im):
    lam = lax.complex(lam_re.astype(F32), lam_im.astype(F32))
    dt = jnp.exp(log_dt.astype(F32))[:, None]
    a_bar = jnp.exp(lam * dt)
    b_bar = ((a_bar - 1.0) / lam)[..., None] * lax.complex(b_re.astype(F32), b_im.astype(F32))
    eye = jnp.eye(S5_GROUPS_PER_DOT, dtype=F32)
    bb = jnp.stack([b_bar.real, b_bar.imag]).reshape(2, S5_CHUNKS, S5_GROUPS_PER_DOT, N_A, C_A)
    w_bu = jnp.einsum('pkgnc,gh->kgcphn', bb, eye).reshape(
        S5_CHUNKS, S5_GROUPS_PER_DOT * C_A, 2 * S5_GROUPS_PER_DOT * N_A).astype(BF16)
    cc = jnp.stack([c_re.astype(F32), -c_im.astype(F32)]).reshape(
        2, S5_CHUNKS, S5_GROUPS_PER_DOT, C_A, N_A)
    wc = jnp.einsum('pkgcn,gh->pkgnhc', cc, eye).reshape(
        2, S5_CHUNKS, S5_GROUPS_PER_DOT * N_A, S5_GROUPS_PER_DOT * C_A).astype(BF16)
    a_pair = jnp.stack([a_bar.real, a_bar.imag])
    return a_pair, w_bu, wc


def _hgrn_kernel(q_ref, f_ref, i_ref, g_ref, lb_ref, ng_ref, s0_ref, y_ref, sl_ref,
                 st_ref, o_scr, kp_scr, ip_scr, pad_scr, *, tc, n_sub, t_valid):
    c = HGRN_SUB
    j = pl.program_id(2)

    @pl.when(j == 0)
    def _():
        st_ref[...] = s0_ref[...].T

    kp_scr[...] = jnp.zeros_like(kp_scr)
    ip_scr[...] = jnp.zeros_like(ip_scr)
    lb = lb_ref[...]
    row = lax.broadcasted_iota(jnp.int32, (c, LANES), 0)
    padded = tc < c
    if padded:
        pad_scr[...] = jnp.zeros_like(pad_scr)
        pad_scr[0, 0:tc, :] = q_ref[...]
        pad_scr[1, 0:tc, :] = f_ref[...]
        pad_scr[2, 0:tc, :] = i_ref[...]

    def sub_chunk(si, carry):
        r0 = pl.multiple_of(si * c, c)
        if padded:
            q, f_pre, inp = pad_scr[0], pad_scr[1], pad_scr[2]
        else:
            q = q_ref[pl.ds(r0, c), :]
            f_pre = f_ref[pl.ds(r0, c), :]
            inp = i_ref[pl.ds(r0, c), :]
        f = lb + (1.0 - lb) * jax.nn.sigmoid(f_pre)
        log_f = jnp.log(f)
        kk = 1.0 - f
        if padded:
            log_f = jnp.where(row < t_valid, log_f, 0.0)
            kk = jnp.where(row < t_valid, kk, 0.0)
        cum = jnp.zeros((c, LANES), F32)
        for s in range(t_valid):
            cum = cum + jnp.where(row >= s, log_f[s:s + 1, :], 0.0)
        last = cum[c - 1:c, :]
        st = st_ref[...]
        o_inter = lax.dot_general((q * jnp.exp(cum)).astype(BF16), st.astype(BF16),
                                  (((1,), (1,)), ((), ())), preferred_element_type=F32)
        for t in range(t_valid):
            rel = cum[t:t + 1, :] - cum
            decay = jnp.exp(jnp.where(row <= t, rel, -jnp.inf))
            w = (q[t:t + 1, :] * kk) * decay
            r = jnp.sum(w, axis=-1, keepdims=True)
            o_t = jnp.sum(r * inp, axis=0, keepdims=True)
            o_scr[pl.ds(r0 + t, 1), :] = o_t + o_inter[t:t + 1, :]
        kp_scr[0:c, :] = kk * jnp.exp(last - cum)
        ip_scr[0:c, :] = inp
        upd = jnp.dot(ip_scr[...].T.astype(BF16), kp_scr[...].astype(BF16),
                      preferred_element_type=F32)
        st_ref[...] = st * jnp.exp(last) + upd
        return carry

    lax.fori_loop(0, n_sub, sub_chunk, 0)

    o = o_scr[0:tc, :]
    y = o * lax.rsqrt(jnp.mean(o * o, axis=-1, keepdims=True) + EPS) * ng_ref[...]
    y_ref[...] = (y * _silu(g_ref[...])).astype(y_ref.dtype)

    @pl.when(j == pl.num_programs(2) - 1)
    def _():
        sl_ref[...] = st_ref[...].T


def hgrn(z4, lb, norm_g, s0, out_dtype):
    _, bsz, t, _ = z4.shape
    c = HGRN_SUB
    tc = _row_tile(t, 256) if t >= c else t
    n_sub = max(tc // c, 1)
    t_valid = min(c, tc)
    o_rows = max(tc, c)

    def col(g):
        return pl.BlockSpec((None, None, tc, DK_B), lambda b, h, j: (g, b, j, h))

    state_spec = pl.BlockSpec((None, None, DK_B, DV_B), lambda b, h, j: (b, h, 0, 0))
    return pl.pallas_call(
        functools.partial(_hgrn_kernel, tc=tc, n_sub=n_sub, t_valid=t_valid),
        grid=(bsz, H_B, t // tc),
        in_specs=[col(G_QB), col(G_FB), col(G_IB), col(G_GB),
                  pl.BlockSpec((1, DK_B), lambda b, h, j: (0, h)),
                  pl.BlockSpec((1, DV_B), lambda b, h, j: (0, 0)),
                  state_spec],
        out_specs=[pl.BlockSpec((None, tc, DV_B), lambda b, h, j: (b, j, h)), state_spec],
        out_shape=[jax.ShapeDtypeStruct((bsz, t, GROUP_W), out_dtype),
                   jax.ShapeDtypeStruct((bsz, H_B, DK_B, DV_B), F32)],
        scratch_shapes=[pltpu.VMEM((DV_B, DK_B), F32),
                        pltpu.VMEM((o_rows, DV_B), F32),
                        pltpu.VMEM((LANES, DK_B), F32),
                        pltpu.VMEM((LANES, DV_B), F32),
                        pltpu.VMEM((3, c, LANES), F32)],
        compiler_params=_params("parallel", "parallel", "arbitrary"),
        name="hgrn",
    )(z4, z4, z4, z4, lb.reshape(1, GROUP_W), norm_g.reshape(1, DV_B), s0)


def _diff_norm_gate(o, ng, gate, out_scale):
    y = o * lax.rsqrt(jnp.mean(o * o, axis=-1, keepdims=True) + EPS) * ng
    return (y * out_scale) * _silu(gate)


def _dattn_prompt_kernel(lam_ref, q_ref, k_ref, v_ref, g_ref, ng_ref, o_ref, *, tq, t, out_scale):
    i = pl.program_id(2)
    q = q_ref[...] * (DH_C ** -0.5)
    lane = lax.broadcasted_iota(jnp.int32, (tq, DV_C), 1)
    q1 = jnp.where(lane < DH_C, q, 0.0).astype(BF16)
    q2 = jnp.where(lane >= DH_C, q, 0.0).astype(BF16)
    k = k_ref[...].astype(BF16)
    v = v_ref[...].astype(BF16)
    nt = (((1,), (1,)), ((), ()))
    q_pos = i * tq + lax.broadcasted_iota(jnp.int32, (tq, t), 0)
    k_pos = lax.broadcasted_iota(jnp.int32, (tq, t), 1)
    mask = k_pos <= q_pos

    def softmax_pv(qj):
        s = lax.dot_general(qj, k, nt, preferred_element_type=F32)
        s = jnp.where(mask, s, -jnp.inf)
        e = jnp.exp(s - jnp.max(s, axis=-1, keepdims=True))
        l = jnp.sum(e, axis=-1, keepdims=True)
        return jnp.dot(e.astype(BF16), v, preferred_element_type=F32) / l

    o = softmax_pv(q1) - lam_ref[...] * softmax_pv(q2)
    o_ref[...] = _diff_norm_gate(o, ng_ref[...], g_ref[...], out_scale).astype(o_ref.dtype)


def dattn_prompt(z4, lam, norm_g, lam_init, out_dtype):
    _, bsz, t, _ = z4.shape
    tq = _row_tile(t, 256)

    def q_col(g):
        return pl.BlockSpec((None, None, tq, DV_C), lambda b, h, i: (g, b, i, h))

    def kv_col(g):
        return pl.BlockSpec((None, None, t, DV_C), lambda b, h, i: (g, b, 0, h))

    vec = pl.BlockSpec((1, DV_C), lambda b, h, i: (0, 0))
    return pl.pallas_call(
        functools.partial(_dattn_prompt_kernel, tq=tq, t=t, out_scale=1.0 - lam_init),
        grid=(bsz, H_C, t // tq),
        in_specs=[vec, q_col(G_QC), kv_col(G_KC), kv_col(G_VC), q_col(G_GC), vec],
        out_specs=pl.BlockSpec((None, tq, DV_C), lambda b, h, i: (b, i, h)),
        out_shape=jax.ShapeDtypeStruct((bsz, t, GROUP_W), out_dtype),
        compiler_params=_params("parallel", "parallel", "arbitrary"),
        name="dattn_prompt",
    )(jnp.full((1, DV_C), lam, F32), z4, z4, z4, z4, norm_g.reshape(1, DV_C))


def _dattn_sample_kernel(pt_ref, lam_ref, q_ref, kn_ref, vn_ref, g_ref, ng_ref, *rest,
                         pps, t, out_scale):
    k_refs = rest[:pps]
    v_refs = rest[pps:2 * pps]
    o_ref = rest[2 * pps]
    qe_scr, m_scr, l_scr, acc_scr, kn_scr, vn_scr = rest[2 * pps + 1:]
    del pt_ref
    j = pl.program_id(1)
    n_rows = 2 * t * H_C
    width = H_C * DV_C
    nt = (((1,), (1,)), ((), ()))
    row = lax.broadcasted_iota(jnp.int32, (n_rows, width), 0)
    lane = lax.broadcasted_iota(jnp.int32, (n_rows, width), 1)

    @pl.when(j == 0)
    def _():
        for r in range(2 * t):
            qi = r % t
            acc_scr[r * H_C:(r + 1) * H_C, :] = jnp.broadcast_to(q_ref[qi:qi + 1, :], (H_C, width))
        keep = (lane // DV_C == row % H_C) & ((lane % DV_C) // DH_C == row // (t * H_C))
        qe_scr[...] = jnp.where(keep, acc_scr[...] * (DH_C ** -0.5), 0.0).astype(BF16)
        m_scr[...] = jnp.full_like(m_scr, -jnp.inf)
        l_scr[...] = jnp.zeros_like(l_scr)
        acc_scr[...] = jnp.zeros_like(acc_scr)

    def online_update(s, v_blocks):
        m_old = m_scr[...]
        m_new = jnp.maximum(m_old, jnp.max(s, axis=-1, keepdims=True))
        alpha = jnp.exp(m_old - m_new)
        e = jnp.exp(s - m_new)
        l_scr[...] = alpha * l_scr[...] + jnp.sum(e, axis=-1, keepdims=True)
        pv = None
        for r, vb in enumerate(v_blocks):
            d = jnp.dot(e[:, r * LANES:(r + 1) * LANES].astype(BF16), vb, preferred_element_type=F32)
            pv = d if pv is None else pv + d
        acc_scr[...] = alpha * acc_scr[...] + pv
        m_scr[...] = m_new

    qe = qe_scr[...]
    s_past = jnp.concatenate(
        [lax.dot_general(qe, k_refs[r][...].astype(BF16), nt, preferred_element_type=F32)
         for r in range(pps)], axis=-1)
    online_update(s_past, [v_refs[r][...].astype(BF16) for r in range(pps)])

    @pl.when(j == pl.num_programs(1) - 1)
    def _():
        kn_scr[...] = jnp.zeros_like(kn_scr)
        vn_scr[...] = jnp.zeros_like(vn_scr)
        kn_scr[0:t, :] = kn_ref[...]
        vn_scr[0:t, :] = vn_ref[...]
        s_new = lax.dot_general(qe, kn_scr[...].astype(BF16), nt, preferred_element_type=F32)
        key = lax.broadcasted_iota(jnp.int32, (n_rows, LANES), 1)
        qry = (lax.broadcasted_iota(jnp.int32, (n_rows, LANES), 0) // H_C) % t
        s_new = jnp.where(key <= qry, s_new, -jnp.inf)
        online_update(s_new, [vn_scr[...].astype(BF16)])

        half = t * H_C
        o = (acc_scr[0:half, :] / l_scr[0:half, :]
             - lam_ref[...] * (acc_scr[half:n_rows, :] / l_scr[half:n_rows, :]))
        diag = (lax.broadcasted_iota(jnp.int32, (H_C, width), 1) // DV_C
                == lax.broadcasted_iota(jnp.int32, (H_C, width), 0))
        for qi in range(t):
            d = jnp.where(diag, o[qi * H_C:(qi + 1) * H_C, :], 0.0)
            d = d * lax.rsqrt(jnp.sum(d * d, axis=-1, keepdims=True) / DV_C + EPS)
            y = jnp.sum(d, axis=0, keepdims=True) * ng_ref[...]
            o_ref[qi:qi + 1, :] = ((y * out_scale) * _silu(g_ref[qi:qi + 1, :])).astype(o_ref.dtype)


def dattn_sample(z4, cache_k, cache_v, layer, page_table, lam, norm_g, lam_init, out_dtype):
    _, bsz, t, _ = z4.shape
    depth, n_pool, page, _, _ = cache_k.shape
    n_pages = page_table.shape[1]
    pps = 4 if n_pages % 4 == 0 else 1
    width = H_C * DV_C
    ck = cache_k.reshape(depth, n_pool, page, width)
    cv = cache_v.reshape(depth, n_pool, page, width)

    def tok(g):
        return pl.BlockSpec((None, None, t, width), lambda b, j, pt: (g, b, 0, 0))

    def page_spec(r):
        return pl.BlockSpec((None, None, page, width),
                            lambda b, j, pt: (layer, pt[b * n_pages + j * pps + r], 0, 0))

    vec = pl.BlockSpec((1, width), lambda b, j, pt: (0, 0))
    n_rows = 2 * t * H_C
    grid_spec = pltpu.PrefetchScalarGridSpec(
        num_scalar_prefetch=1,
        grid=(bsz, n_pages // pps),
        in_specs=[vec, tok(G_QC), tok(G_KC), tok(G_VC), tok(G_GC), vec]
        + [page_spec(r) for r in range(pps)] + [page_spec(r) for r in range(pps)],
        out_specs=pl.BlockSpec((None, t, width), lambda b, j, pt: (b, 0, 0)),
        scratch_shapes=[pltpu.VMEM((n_rows, width), BF16),
                        pltpu.VMEM((n_rows, 1), F32),
                        pltpu.VMEM((n_rows, 1), F32),
                        pltpu.VMEM((n_rows, width), F32),
                        pltpu.VMEM((LANES, width), F32),
                        pltpu.VMEM((LANES, width), F32)],
    )
    return pl.pallas_call(
        functools.partial(_dattn_sample_kernel, pps=pps, t=t, out_scale=1.0 - lam_init),
        grid_spec=grid_spec,
        out_shape=jax.ShapeDtypeStruct((bsz, t, width), out_dtype),
        compiler_params=_params("parallel", "arbitrary"),
        name="dattn_sample",
    )(page_table.reshape(-1), jnp.full((1, width), lam, F32), z4, z4, z4, z4,
      jnp.tile(norm_g.reshape(1, DV_C), (1, H_C)), *([ck] * pps), *([cv] * pps))


def _mem_attn_kernel(q_ref, g_ref, mk_ref, mv_ref, o_ref, pad_scr, *, tq):
    rows = pad_scr.shape[0]
    if tq < rows:
        pad_scr[...] = jnp.zeros_like(pad_scr)
        pad_scr[0:tq, :] = q_ref[...]
        q_all = pad_scr[...]
    else:
        q_all = q_ref[...]
    nt = (((1,), (1,)), ((), ()))
    for h in range(H_M):
        cols = slice(h * DH_M, (h + 1) * DH_M)
        q = (q_all[:, cols] * (DH_M ** -0.5)).astype(BF16)
        s = lax.dot_general(q, mk_ref[:, cols].astype(BF16), nt, preferred_element_type=F32)
        e = jnp.exp(s - jnp.max(s, axis=-1, keepdims=True))
        l = jnp.sum(e, axis=-1, keepdims=True)
        o = jnp.dot(e.astype(BF16), mv_ref[:, cols].astype(BF16), preferred_element_type=F32) / l
        o_ref[:, cols] = (o[0:tq, :] * _silu(g_ref[:, cols])).astype(o_ref.dtype)


def mem_attn(z4, mem_k, mem_v, out_dtype):
    _, bsz, t, _ = z4.shape
    n_mem = mem_k.shape[1]
    tq = _row_tile(t, 256)
    rows = max(tq, 2 * SUBLANES)

    def tok(g):
        return pl.BlockSpec((None, None, tq, GROUP_W), lambda b, i: (g, b, i, 0))

    mem_spec = pl.BlockSpec((None, n_mem, GROUP_W), lambda b, i: (b, 0, 0))
    return pl.pallas_call(
        functools.partial(_mem_attn_kernel, tq=tq),
        grid=(bsz, t // tq),
        in_specs=[tok(G_QM), tok(G_GM), mem_spec, mem_spec],
        out_specs=pl.BlockSpec((None, tq, GROUP_W), lambda b, i: (b, i, 0)),
        out_shape=jax.ShapeDtypeStruct((bsz, t, GROUP_W), out_dtype),
        scratch_shapes=[pltpu.VMEM((rows, GROUP_W), F32)],
        compiler_params=_params("parallel", "parallel"),
        name="mem_attn",
    )(z4, z4, mem_k.reshape(bsz, n_mem, GROUP_W), mem_v.reshape(bsz, n_mem, GROUP_W))


def trunk_layer(x, lw, attend, mem_k, mem_v, s5_h0, hgrn_s0, part_dtype):
    bsz, t, d = x.shape
    m = bsz * t
    x2 = x.reshape(m, d)
    h = rmsnorm(x2, lw['norm_g'], BF16)
    z = matmul_groups(h, lw['w_in'])
    z4 = z.reshape(N_GROUPS_IN, bsz, t, GROUP_W)

    bu = s5_bu(z, lw['w_bu'])
    h_all, s5_last = s5_scan(bu, lw['a_bar'], s5_h0, bsz, t)
    y_a = s5_y(h_all, z, lw['wc'], lw['s5_d'], lw['w_glu'], lw['b_glu'], part_dtype)

    y_b, hgrn_s = hgrn(z4, lw['lb'], lw['hgrn_norm_g'], hgrn_s0, part_dtype)
    y_c = attend(z4)
    y_m = mem_attn(z4, mem_k, mem_v, part_dtype)

    parts = (y_a, y_b.reshape(m, GROUP_W), y_c.reshape(m, GROUP_W), y_m.reshape(m, GROUP_W))
    x_out = outproj(parts, lw['w_out'], x2).reshape(bsz, t, d)
    return x_out, z4, s5_last, hgrn_s


def kernel(x_prompt, x_sample, cache_k, cache_v, cache_mem_k, cache_mem_v, state_s5_re, state_s5_im, state_hgrn, page_table, mem_prompt, norm_g, w_in, s5_lambda_re, s5_lambda_im, s5_log_dt, s5_b_re, s5_b_im, s5_c_re, s5_c_im, s5_d, s5_w_glu, s5_b_glu, hgrn_lower_bounds, hgrn_norm_g, diff_lq1, diff_lk1, diff_lq2, diff_lk2, diff_norm_g, mem_norm_g, w_mem_kv, w_out, final_norm_g):
    depth = w_in.shape[0]
    bsz_p, t_p, d = x_prompt.shape
    bsz_s, t_s, _ = x_sample.shape
    n_mem = mem_prompt.shape[1]
    lb_all = jnp.cumsum(jax.nn.softmax(hgrn_lower_bounds.astype(F32), axis=0), axis=0)
    lb_all = lb_all - lb_all[0]
    zeros_s5 = jnp.zeros((2, bsz_p, G_A, N_A), F32)
    zeros_hgrn = jnp.zeros((bsz_p, H_B, DK_B, DV_B), F32)

    xp, xs = x_prompt, x_sample
    outs = {k: [] for k in ('kp', 'vp', 'ks', 'vs', 'mkp', 'mvp', 's5p', 's5s', 'hgp', 'hgs')}
    for l in range(depth):
        lam_init = 0.8 - 0.6 * math.exp(-0.3 * l)
        lam = (jnp.exp(jnp.sum(diff_lq1[l].astype(F32) * diff_lk1[l].astype(F32)))
               - jnp.exp(jnp.sum(diff_lq2[l].astype(F32) * diff_lk2[l].astype(F32))) + lam_init)
        a_bar, w_bu, wc = s5_weights(s5_lambda_re[l], s5_lambda_im[l], s5_log_dt[l],
                                     s5_b_re[l], s5_b_im[l], s5_c_re[l], s5_c_im[l])
        lw = {'norm_g': norm_g[l], 'w_in': w_in[l].astype(BF16), 'w_out': w_out[l].astype(BF16),
              'a_bar': a_bar, 'w_bu': w_bu, 'wc': wc, 's5_d': s5_d[l],
              'w_glu': s5_w_glu[l].astype(BF16), 'b_glu': s5_b_glu[l],
              'lb': lb_all[l], 'hgrn_norm_g': hgrn_norm_g[l]}

        hm = rmsnorm(mem_prompt.reshape(bsz_p * n_mem, d), mem_norm_g[l], BF16)
        mkv = matmul_groups(hm, w_mem_kv[l].astype(BF16))
        mk_p = mkv[0].reshape(bsz_p, n_mem, H_M, DH_M)
        mv_p = mkv[1].reshape(bsz_p, n_mem, H_M, DH_M)
        attend_p = functools.partial(dattn_prompt, lam=lam, norm_g=diff_norm_g[l],
                                     lam_init=lam_init, out_dtype=BF16)
        xp, z4, s5_last, hg = trunk_layer(xp, lw, attend_p, mk_p, mv_p, zeros_s5, zeros_hgrn, BF16)
        outs['kp'].append(z4[G_KC].reshape(bsz_p, t_p, H_C, 2 * DH_C))
        outs['vp'].append(z4[G_VC].reshape(bsz_p, t_p, H_C, DV_C))
        outs['mkp'].append(mk_p)
        outs['mvp'].append(mv_p)
        outs['s5p'].append(s5_last)
        outs['hgp'].append(hg)

        attend_s = functools.partial(dattn_sample, cache_k=cache_k, cache_v=cache_v, layer=l,
                                     page_table=page_table, lam=lam, norm_g=diff_norm_g[l],
                                     lam_init=lam_init, out_dtype=F32)
        s5_h0 = jnp.stack([state_s5_re[l].astype(F32), state_s5_im[l].astype(F32)])
        xs, z4, s5_last, hg = trunk_layer(xs, lw, attend_s, cache_mem_k[l], cache_mem_v[l],
                                          s5_h0, state_hgrn[l], F32)
        outs['ks'].append(z4[G_KC].reshape(bsz_s, t_s, H_C, 2 * DH_C))
        outs['vs'].append(z4[G_VC].reshape(bsz_s, t_s, H_C, DV_C))
        outs['s5s'].append(s5_last)
        outs['hgs'].append(hg)

    y_prompt = rmsnorm(xp.reshape(bsz_p * t_p, d), final_norm_g, F32).reshape(bsz_p, t_p, d)
    y_sample = rmsnorm(xs.reshape(bsz_s * t_s, d), final_norm_g, F32).reshape(bsz_s, t_s, d)
    s5p = jnp.stack(outs['s5p'])
    s5s = jnp.stack(outs['s5s'])
    return (y_prompt, y_sample,
            jnp.stack(outs['kp']), jnp.stack(outs['vp']), jnp.stack(outs['ks']), jnp.stack(outs['vs']),
            jnp.stack(outs['mkp']), jnp.stack(outs['mvp']),
            s5p[:, 0], s5p[:, 1], s5s[:, 0], s5s[:, 1],
            jnp.stack(outs['hgp']), jnp.stack(outs['hgs']))
```

```python
import functools
import math

import jax
import jax.numpy as jnp
from jax import lax
from jax.experimental import pallas as pl
from jax.experimental.pallas import tpu as pltpu

F32 = jnp.float32
BF16 = jnp.bfloat16

D_MODEL = 4096
GROUP_W = 1024
N_GROUPS_IN = 12
C_A = 16
G_A = GROUP_W // C_A
N_A = 64
S5_GROUPS_PER_DOT = 16
S5_CHUNKS = G_A // S5_GROUPS_PER_DOT
S5_STATE = G_A * N_A
DK_B = 128
DV_B = 128
H_B = GROUP_W // DK_B
DV_C = 128
DH_C = DV_C // 2
H_C = GROUP_W // DV_C
H_M = 4
DH_M = GROUP_W // H_M
EPS = 1e-6
HGRN_CHUNK = 64
HGRN_SUB = 16
SUBLANES = 8
LANES = 128
MXU_N = 256
VMEM_LIMIT = 56 * 1024 * 1024

(G_UA, G_GA, G_QB, G_FB, G_IB, G_GB, G_QC, G_KC, G_VC, G_GC, G_QM, G_GM) = range(N_GROUPS_IN)


def _params(*sem):
    return pltpu.CompilerParams(dimension_semantics=sem, vmem_limit_bytes=VMEM_LIMIT)


def _silu(x):
    return x * jax.nn.sigmoid(x)


def _row_tile(m, cap):
    return cap if m % cap == 0 else m


def _rmsnorm_kernel(x_ref, g_ref, o_ref):
    x = x_ref[...]
    y = x * lax.rsqrt(jnp.mean(x * x, axis=-1, keepdims=True) + EPS)
    o_ref[...] = (y * g_ref[...]).astype(o_ref.dtype)


def rmsnorm(x, g, out_dtype):
    m, d = x.shape
    tm = _row_tile(m, 256)
    return pl.pallas_call(
        _rmsnorm_kernel,
        grid=(m // tm,),
        in_specs=[pl.BlockSpec((tm, d), lambda i: (i, 0)),
                  pl.BlockSpec((1, d), lambda i: (0, 0))],
        out_specs=pl.BlockSpec((tm, d), lambda i: (i, 0)),
        out_shape=jax.ShapeDtypeStruct((m, d), out_dtype),
        compiler_params=_params("parallel"),
        name="rmsnorm",
    )(x, g.reshape(1, d))


def _matmul_kernel(a_ref, b_ref, o_ref):
    o_ref[...] = jnp.dot(a_ref[...], b_ref[...].astype(BF16), preferred_element_type=F32)


def matmul_groups(a, b, layer):
    m, kd = a.shape
    n = b.shape[2]
    tm = _row_tile(m, 2048)
    tn = MXU_N
    per_group = GROUP_W // tn
    return pl.pallas_call(
        _matmul_kernel,
        grid=(m // tm, n // tn),
        in_specs=[pl.BlockSpec((tm, kd), lambda i, j: (i, 0)),
                  pl.BlockSpec((None, kd, tn), lambda i, j: (layer, 0, j))],
        out_specs=pl.BlockSpec((None, tm, tn), lambda i, j: (j // per_group, i, j % per_group)),
        out_shape=jax.ShapeDtypeStruct((n // GROUP_W, m, GROUP_W), F32),
        compiler_params=_params("parallel", "arbitrary"),
        name="matmul_groups",
    )(a, b)


def _outproj_kernel(a0_ref, a1_ref, a2_ref, a3_ref, w_ref, x_ref, o_ref):
    acc = x_ref[...]
    for g, a_ref in enumerate((a0_ref, a1_ref, a2_ref, a3_ref)):
        acc += jnp.dot(a_ref[...].astype(BF16), w_ref[g * GROUP_W:(g + 1) * GROUP_W, :].astype(BF16),
                       preferred_element_type=F32)
    o_ref[...] = acc


def outproj(parts, w, layer, x):
    m, d = x.shape
    tm = _row_tile(m, 1024)
    tn = MXU_N
    part_spec = pl.BlockSpec((tm, GROUP_W), lambda i, j: (i, 0))
    return pl.pallas_call(
        _outproj_kernel,
        grid=(m // tm, d // tn),
        in_specs=[part_spec, part_spec, part_spec, part_spec,
                  pl.BlockSpec((None, 4 * GROUP_W, tn), lambda i, j: (layer, 0, j)),
                  pl.BlockSpec((tm, tn), lambda i, j: (i, j))],
        out_specs=pl.BlockSpec((tm, tn), lambda i, j: (i, j)),
        out_shape=jax.ShapeDtypeStruct((m, d), F32),
        compiler_params=_params("parallel", "arbitrary"),
        name="outproj",
    )(*parts, w, x)


def _s5_bu_kernel(u_ref, w_ref, o_ref):
    r = jnp.dot(u_ref[...].astype(BF16), w_ref[...], preferred_element_type=F32)
    half = r.shape[1] // 2
    o_ref[0] = r[:, :half]
    o_ref[1] = r[:, half:]


def s5_bu(z, w_bu):
    m = z.shape[1]
    tm = _row_tile(m, 512)
    kw = S5_GROUPS_PER_DOT * C_A
    nw = S5_GROUPS_PER_DOT * N_A
    return pl.pallas_call(
        _s5_bu_kernel,
        grid=(m // tm, S5_CHUNKS),
        in_specs=[pl.BlockSpec((None, tm, kw), lambda i, k: (G_UA, i, k)),
                  pl.BlockSpec((None, kw, 2 * nw), lambda i, k: (k, 0, 0))],
        out_specs=pl.BlockSpec((2, tm, nw), lambda i, k: (0, i, k)),
        out_shape=jax.ShapeDtypeStruct((2, m, S5_STATE), F32),
        compiler_params=_params("parallel", "parallel"),
        name="s5_bu",
    )(z, w_bu)


def _s5_scan_kernel(bu_ref, a_ref, h0_ref, h_ref, hl_ref, st_ref, *, tt):
    j = pl.program_id(1)

    @pl.when(j == 0)
    def _():
        st_ref[...] = h0_ref[...]

    a_re = a_ref[0]
    a_im = a_ref[1]

    def body(t, carry):
        h_re, h_im = carry
        n_re = a_re * h_re - a_im * h_im + bu_ref[0, t]
        n_im = a_re * h_im + a_im * h_re + bu_ref[1, t]
        h_ref[0, t] = n_re
        h_ref[1, t] = n_im
        return n_re, n_im

    h_re, h_im = lax.fori_loop(0, tt, body, (st_ref[0], st_ref[1]), unroll=min(tt, 8))
    st_ref[0] = h_re
    st_ref[1] = h_im

    @pl.when(j == pl.num_programs(1) - 1)
    def _():
        hl_ref[0] = h_re
        hl_ref[1] = h_im


def s5_scan(bu, a_bar, h0, bsz, t):
    rows = S5_STATE // LANES
    bu5 = bu.reshape(2, bsz, t, rows, LANES)
    tt = _row_tile(t, 128)
    h, h_last = pl.pallas_call(
        functools.partial(_s5_scan_kernel, tt=tt),
        grid=(bsz, t // tt),
        in_specs=[pl.BlockSpec((2, None, tt, rows, LANES), lambda b, j: (0, b, j, 0, 0)),
                  pl.BlockSpec((2, rows, LANES), lambda b, j: (0, 0, 0)),
                  pl.BlockSpec((2, None, rows, LANES), lambda b, j: (0, b, 0, 0))],
        out_specs=[pl.BlockSpec((2, None, tt, rows, LANES), lambda b, j: (0, b, j, 0, 0)),
                   pl.BlockSpec((2, None, rows, LANES), lambda b, j: (0, b, 0, 0))],
        out_shape=[jax.ShapeDtypeStruct((2, bsz, t, rows, LANES), F32),
                   jax.ShapeDtypeStruct((2, bsz, rows, LANES), F32)],
        scratch_shapes=[pltpu.VMEM((2, rows, LANES), F32)],
        compiler_params=_params("parallel", "arbitrary"),
        name="s5_scan",
    )(bu5, a_bar.reshape(2, rows, LANES), h0.reshape(2, bsz, rows, LANES))
    return h.reshape(2, bsz * t, S5_STATE), h_last.reshape(2, bsz, G_A, N_A)


def _s5_y_kernel(h_ref, u_ref, g_ref, wc_ref, d_ref, wg_ref, bg_ref, o_ref):
    nw = S5_GROUPS_PER_DOT * N_A
    parts = []
    for k in range(S5_CHUNKS):
        h_re = h_ref[0, :, k * nw:(k + 1) * nw].astype(BF16)
        h_im = h_ref[1, :, k * nw:(k + 1) * nw].astype(BF16)
        parts.append(jnp.dot(h_re, wc_ref[0, k], preferred_element_type=F32)
                     + jnp.dot(h_im, wc_ref[1, k], preferred_element_type=F32))
    y = jnp.concatenate(parts, axis=-1) + d_ref[...] * u_ref[...]
    y = jax.nn.gelu(y)
    glu = jnp.dot(y.astype(BF16), wg_ref[...], preferred_element_type=F32) + bg_ref[...]
    y = y * jax.nn.sigmoid(glu)
    o_ref[...] = (y * _silu(g_ref[...])).astype(o_ref.dtype)


def s5_y(h, z, wc, d, w_glu, b_glu, out_dtype):
    m = z.shape[1]
    tm = _row_tile(m, 256)
    kw = S5_GROUPS_PER_DOT * C_A
    nw = S5_GROUPS_PER_DOT * N_A
    return pl.pallas_call(
        _s5_y_kernel,
        grid=(m // tm,),
        in_specs=[pl.BlockSpec((2, tm, S5_STATE), lambda i: (0, i, 0)),
                  pl.BlockSpec((None, tm, GROUP_W), lambda i: (G_UA, i, 0)),
                  pl.BlockSpec((None, tm, GROUP_W), lambda i: (G_GA, i, 0)),
                  pl.BlockSpec((2, S5_CHUNKS, nw, kw), lambda i: (0, 0, 0, 0)),
                  pl.BlockSpec((1, GROUP_W), lambda i: (0, 0)),
                  pl.BlockSpec((GROUP_W, GROUP_W), lambda i: (0, 0)),
                  pl.BlockSpec((1, GROUP_W), lambda i: (0, 0))],
        out_specs=pl.BlockSpec((tm, GROUP_W), lambda i: (i, 0)),
        out_shape=jax.ShapeDtypeStruct((m, GROUP_W), out_dtype),
        compiler_params=_params("parallel"),
        name="s5_y",
    )(h, z, z, wc, d.reshape(1, GROUP_W), w_glu, b_glu.reshape(1, GROUP_W))


def s5_weights(lam_re, lam_im, log_dt, b_re, b_im, c_re, c_im):
    lam = lax.complex(lam_re.astype(F32), lam_im.astype(F32))
    dt = jnp.exp(log_dt.astype(F32))[:, None]
    a_bar = jnp.exp(lam * dt)
    b_bar = ((a_bar - 1.0) / lam)[..., None] * lax.complex(b_re.astype(F32), b_im.astype(F32))
    eye = jnp.eye(S5_GROUPS_PER_DOT, dtype=F32)
    bb = jnp.stack([b_bar.real, b_bar.imag]).reshape(2, S5_CHUNKS, S5_GROUPS_PER_DOT, N_A, C_A)
    w_bu = jnp.einsum('pkgnc,gh->kgcphn', bb, eye).reshape(
        S5_CHUNKS, S5_GROUPS_PER_DOT * C_A, 2 * S5_GROUPS_PER_DOT * N_A).astype(BF16)
    cc = jnp.stack([c_re.astype(F32), -c_im.astype(F32)]).reshape(
        2, S5_CHUNKS, S5_GROUPS_PER_DOT, C_A, N_A)
    wc = jnp.einsum('pkgcn,gh->pkgnhc', cc, eye).reshape(
        2, S5_CHUNKS, S5_GROUPS_PER_DOT * N_A, S5_GROUPS_PER_DOT * C_A).astype(BF16)
    a_pair = jnp.stack([a_bar.real, a_bar.imag])
    return a_pair, w_bu, wc


def _hgrn_kernel(q_ref, f_ref, i_ref, g_ref, lb_ref, ng_ref, s0_ref, y_ref, sl_ref,
                 st_ref, pad_scr, *, tc, chunk, t_valid):
    sub = HGRN_SUB
    n_sb = chunk // sub
    j = pl.program_id(2)
    nt = (((1,), (1,)), ((), ()))

    @pl.when(j == 0)
    def _():
        st_ref[...] = s0_ref[...].T

    lb = lb_ref[...]
    padded = tc < chunk
    if padded:
        pad_scr[...] = jnp.zeros_like(pad_scr)
        pad_scr[0, 0:tc, :] = q_ref[...]
        pad_scr[1, 0:tc, :] = f_ref[...]
        pad_scr[2, 0:tc, :] = i_ref[...]
    row_c = lax.broadcasted_iota(jnp.int32, (chunk, LANES), 0)
    row_s = lax.broadcasted_iota(jnp.int32, (sub, LANES), 0)
    lane_s = lax.broadcasted_iota(jnp.int32, (sub, chunk), 1)
    tri = (lax.broadcasted_iota(jnp.int32, (chunk, chunk), 1)
           <= lax.broadcasted_iota(jnp.int32, (chunk, chunk), 0)).astype(F32)

    def chunk_step(ci, carry):
        r0 = pl.multiple_of(ci * chunk, chunk)
        if padded:
            q, f_pre, inp = pad_scr[0], pad_scr[1], pad_scr[2]
        else:
            q = q_ref[pl.ds(r0, chunk), :]
            f_pre = f_ref[pl.ds(r0, chunk), :]
            inp = i_ref[pl.ds(r0, chunk), :]
        f = lb + (1.0 - lb) * jax.nn.sigmoid(f_pre)
        log_f = jnp.log(f)
        kk = 1.0 - f
        if padded:
            log_f = jnp.where(row_c < t_valid, log_f, 0.0)
            kk = jnp.where(row_c < t_valid, kk, 0.0)
        cum = jnp.dot(tri, log_f, precision=lax.Precision.HIGHEST, preferred_element_type=F32)
        last = cum[chunk - 1:chunk, :]
        st = st_ref[...]
        o = lax.dot_general((q * jnp.exp(cum)).astype(BF16), st.astype(BF16), nt,
                            preferred_element_type=F32)
        inp_b = inp.astype(BF16)
        blocks = []
        for i in range(n_sb):
            rows = slice(i * sub, (i + 1) * sub)
            q_i, kk_i, cum_i = q[rows], kk[rows], cum[rows]
            if i > 0:
                edge = cum[i * sub - 1:i * sub, :]
                a_i = q_i * jnp.exp(cum_i - edge)
                kt_i = kk * jnp.exp(jnp.where(row_c < i * sub, edge - cum, -jnp.inf))
                sc = lax.dot_general(a_i.astype(BF16), kt_i.astype(BF16), nt, preferred_element_type=F32)
            else:
                sc = jnp.zeros((sub, chunk), F32)
            for s in range(min(sub, t_valid)):
                decay = jnp.exp(jnp.where(row_s >= s, cum_i - cum_i[s:s + 1, :], -jnp.inf))
                w = (q_i * kk_i[s:s + 1, :]) * decay
                sc = jnp.where(lane_s == i * sub + s, jnp.sum(w, axis=-1, keepdims=True), sc)
            blocks.append(sc)
        scores = blocks[0] if n_sb == 1 else jnp.concatenate(blocks, axis=0)
        o = o + jnp.dot(scores.astype(BF16), inp_b, preferred_element_type=F32)
        kt = (kk * jnp.exp(last - cum)).astype(BF16)
        upd = lax.dot_general(inp_b, kt, (((0,), (0,)), ((), ())), preferred_element_type=F32)
        st_ref[...] = st * jnp.exp(last) + upd
        y = o * lax.rsqrt(jnp.mean(o * o, axis=-1, keepdims=True) + EPS) * ng_ref[...]
        if padded:
            y_ref[...] = (y[0:tc] * _silu(g_ref[...])).astype(y_ref.dtype)
        else:
            y_ref[pl.ds(r0, chunk), :] = (y * _silu(g_ref[pl.ds(r0, chunk), :])).astype(y_ref.dtype)
        return carry

    lax.fori_loop(0, max(tc // chunk, 1), chunk_step, 0, unroll=True)

    @pl.when(j == pl.num_programs(2) - 1)
    def _():
        sl_ref[...] = st_ref[...].T


def hgrn(z4, lb, norm_g, s0, out_dtype):
    _, bsz, t, _ = z4.shape
    chunk = HGRN_CHUNK if t % HGRN_CHUNK == 0 else HGRN_SUB
    tc = _row_tile(t, 256) if t >= chunk else t
    t_valid = min(chunk, tc)

    def col(g):
        return pl.BlockSpec((None, None, tc, DK_B), lambda b, h, j: (g, b, j, h))

    state_spec = pl.BlockSpec((None, None, DK_B, DV_B), lambda b, h, j: (b, h, 0, 0))
    return pl.pallas_call(
        functools.partial(_hgrn_kernel, tc=tc, chunk=chunk, t_valid=t_valid),
        grid=(bsz, H_B, t // tc),
        in_specs=[col(G_QB), col(G_FB), col(G_IB), col(G_GB),
                  pl.BlockSpec((1, DK_B), lambda b, h, j: (0, h)),
                  pl.BlockSpec((1, DV_B), lambda b, h, j: (0, 0)),
                  state_spec],
        out_specs=[pl.BlockSpec((None, tc, DV_B), lambda b, h, j: (b, j, h)), state_spec],
        out_shape=[jax.ShapeDtypeStruct((bsz, t, GROUP_W), out_dtype),
                   jax.ShapeDtypeStruct((bsz, H_B, DK_B, DV_B), F32)],
        scratch_shapes=[pltpu.VMEM((DV_B, DK_B), F32),
                        pltpu.VMEM((3, chunk, LANES), F32)],
        compiler_params=_params("parallel", "parallel", "arbitrary"),
        name="hgrn",
    )(z4, z4, z4, z4, lb.reshape(1, GROUP_W), norm_g.reshape(1, DV_B), s0)


def _diff_norm_gate(o, ng, gate, out_scale):
    y = o * lax.rsqrt(jnp.mean(o * o, axis=-1, keepdims=True) + EPS) * ng
    return (y * out_scale) * _silu(gate)


def _dattn_prompt_kernel(lam_ref, q_ref, k_ref, v_ref, g_ref, ng_ref, o_ref,
                         kb_scr, vb_scr, m_scr, l_scr, acc_scr, *, tq, out_scale):
    i = pl.program_id(2)

    @pl.when(i == 0)
    def _():
        kb_scr[...] = k_ref[...].astype(BF16)
        vb_scr[...] = v_ref[...].astype(BF16)

    q = q_ref[...] * (DH_C ** -0.5)
    lane = lax.broadcasted_iota(jnp.int32, (tq, DV_C), 1)
    q_maps = (jnp.where(lane < DH_C, q, 0.0).astype(BF16), jnp.where(lane >= DH_C, q, 0.0).astype(BF16))
    nt = (((1,), (1,)), ((), ()))
    m_scr[...] = jnp.full_like(m_scr, -jnp.inf)
    l_scr[...] = jnp.zeros_like(l_scr)
    acc_scr[...] = jnp.zeros_like(acc_scr)
    causal = (lax.broadcasted_iota(jnp.int32, (tq, tq), 1) <= lax.broadcasted_iota(jnp.int32, (tq, tq), 0))

    def kv_block(j, masked):
        r0 = pl.multiple_of(j * tq, tq)
        kb = kb_scr[pl.ds(r0, tq), :]
        vb = vb_scr[pl.ds(r0, tq), :]
        for mi in range(2):
            s = lax.dot_general(q_maps[mi], kb, nt, preferred_element_type=F32)
            if masked:
                s = jnp.where(causal, s, -jnp.inf)
            m_old = m_scr[mi]
            m_new = jnp.maximum(m_old, jnp.max(s, axis=-1, keepdims=True))
            alpha = jnp.exp(m_old - m_new)
            e = jnp.exp(s - m_new)
            l_scr[mi] = alpha * l_scr[mi] + jnp.sum(e, axis=-1, keepdims=True)
            acc_scr[mi] = alpha * acc_scr[mi] + jnp.dot(e.astype(BF16), vb, preferred_element_type=F32)
            m_scr[mi] = m_new

    def past_block(j, carry):
        kv_block(j, False)
        return carry

    lax.fori_loop(0, i, past_block, 0)
    kv_block(i, True)
    o = acc_scr[0] / l_scr[0] - lam_ref[...] * (acc_scr[1] / l_scr[1])
    o_ref[...] = _diff_norm_gate(o, ng_ref[...], g_ref[...], out_scale).astype(o_ref.dtype)


def dattn_prompt(z4, lam, norm_g, lam_init, out_dtype):
    _, bsz, t, _ = z4.shape
    tq = _row_tile(t, 256)

    def q_col(g):
        return pl.BlockSpec((None, None, tq, DV_C), lambda b, h, i: (g, b, i, h))

    def kv_col(g):
        return pl.BlockSpec((None, None, t, DV_C), lambda b, h, i: (g, b, 0, h))

    vec = pl.BlockSpec((1, DV_C), lambda b, h, i: (0, 0))
    return pl.pallas_call(
        functools.partial(_dattn_prompt_kernel, tq=tq, out_scale=1.0 - lam_init),
        grid=(bsz, H_C, t // tq),
        in_specs=[vec, q_col(G_QC), kv_col(G_KC), kv_col(G_VC), q_col(G_GC), vec],
        out_specs=pl.BlockSpec((None, tq, DV_C), lambda b, h, i: (b, i, h)),
        out_shape=jax.ShapeDtypeStruct((bsz, t, GROUP_W), out_dtype),
        scratch_shapes=[pltpu.VMEM((t, DV_C), BF16),
                        pltpu.VMEM((t, DV_C), BF16),
                        pltpu.VMEM((2, tq, 1), F32),
                        pltpu.VMEM((2, tq, 1), F32),
                        pltpu.VMEM((2, tq, DV_C), F32)],
        compiler_params=_params("parallel", "parallel", "arbitrary"),
        name="dattn_prompt",
    )(jnp.full((1, DV_C), lam, F32), z4, z4, z4, z4, norm_g.reshape(1, DV_C))


def _dattn_sample_kernel(pt_ref, lam_ref, q_ref, kn_ref, vn_ref, g_ref, ng_ref, *rest,
                         pps, t, out_scale):
    k_refs = rest[:pps]
    v_refs = rest[pps:2 * pps]
    o_ref = rest[2 * pps]
    qm_scr, m_scr, l_scr, acc_scr, kn_scr, vn_scr = rest[2 * pps + 1:]
    del pt_ref
    j = pl.program_id(1)
    half = t * H_C
    n_rows = 2 * half
    page_rows = k_refs[0].shape[0] * H_C
    nt = (((1,), (1,)), ((), ()))

    @pl.when(j == 0)
    def _():
        q = q_ref[...].reshape(half, DV_C) * (DH_C ** -0.5)
        lane = lax.broadcasted_iota(jnp.int32, (half, DV_C), 1)
        qm_scr[0:half, :] = jnp.where(lane < DH_C, q, 0.0).astype(BF16)
        qm_scr[half:n_rows, :] = jnp.where(lane >= DH_C, q, 0.0).astype(BF16)
        m_scr[...] = jnp.full_like(m_scr, -jnp.inf)
        l_scr[...] = jnp.zeros_like(l_scr)
        acc_scr[...] = jnp.zeros_like(acc_scr)

    def online_update(s, v_blocks, width):
        m_old = m_scr[...]
        m_new = jnp.maximum(m_old, jnp.max(s, axis=-1, keepdims=True))
        alpha = jnp.exp(m_old - m_new)
        e = jnp.exp(s - m_new)
        l_scr[...] = alpha * l_scr[...] + jnp.sum(e, axis=-1, keepdims=True)
        pv = None
        for r, vb in enumerate(v_blocks):
            d = jnp.dot(e[:, r * width:(r + 1) * width].astype(BF16), vb, preferred_element_type=F32)
            pv = d if pv is None else pv + d
        acc_scr[...] = alpha * acc_scr[...] + pv
        m_scr[...] = m_new

    qm = qm_scr[...]
    same_head = (lax.broadcasted_iota(jnp.int32, (n_rows, page_rows), 0) % H_C
                 == lax.broadcasted_iota(jnp.int32, (n_rows, page_rows), 1) % H_C)
    s_parts = []
    for r in range(pps):
        kp = k_refs[r][...].reshape(page_rows, DV_C).astype(BF16)
        s_parts.append(jnp.where(same_head, lax.dot_general(qm, kp, nt, preferred_element_type=F32),
                                 -jnp.inf))
    online_update(jnp.concatenate(s_parts, axis=-1),
                  [v_refs[r][...].reshape(page_rows, DV_C).astype(BF16) for r in range(pps)], page_rows)

    @pl.when(j == pl.num_programs(1) - 1)
    def _():
        kn_scr[...] = jnp.zeros_like(kn_scr)
        vn_scr[...] = jnp.zeros_like(vn_scr)
        kn_scr[0:half, :] = kn_ref[...].reshape(half, DV_C)
        vn_scr[0:half, :] = vn_ref[...].reshape(half, DV_C)
        s_new = lax.dot_general(qm, kn_scr[...].astype(BF16), nt, preferred_element_type=F32)
        row = lax.broadcasted_iota(jnp.int32, (n_rows, LANES), 0)
        col = lax.broadcasted_iota(jnp.int32, (n_rows, LANES), 1)
        visible = (col % H_C == row % H_C) & (col // H_C <= (row // H_C) % t)
        online_update(jnp.where(visible, s_new, -jnp.inf), [vn_scr[...].astype(BF16)], LANES)

        o = (acc_scr[0:half, :] / l_scr[0:half, :]
             - lam_ref[...] * (acc_scr[half:n_rows, :] / l_scr[half:n_rows, :]))
        y = _diff_norm_gate(o, ng_ref[...], g_ref[...].reshape(half, DV_C), out_scale)
        o_ref[...] = y.reshape(t, H_C, DV_C).astype(o_ref.dtype)


def dattn_sample(z4, cache_k, cache_v, layer, page_table, lam, norm_g, lam_init, out_dtype):
    _, bsz, t, _ = z4.shape
    page = cache_k.shape[2]
    n_pages = page_table.shape[1]
    pps = 8 if n_pages % 8 == 0 else 1
    heads = z4[G_QC:G_GC + 1].reshape(4, bsz, t, H_C, DV_C)

    def tok(g):
        return pl.BlockSpec((None, None, t, H_C, DV_C), lambda b, j, pt: (g, b, 0, 0, 0))

    def page_spec(r):
        return pl.BlockSpec((None, None, page, H_C, DV_C),
                            lambda b, j, pt: (layer, pt[b * n_pages + j * pps + r], 0, 0, 0))

    vec = pl.BlockSpec((1, DV_C), lambda b, j, pt: (0, 0))
    n_rows = 2 * t * H_C
    grid_spec = pltpu.PrefetchScalarGridSpec(
        num_scalar_prefetch=1,
        grid=(bsz, n_pages // pps),
        in_specs=[vec, tok(0), tok(1), tok(2), tok(3), vec]
        + [page_spec(r) for r in range(pps)] + [page_spec(r) for r in range(pps)],
        out_specs=pl.BlockSpec((None, t, H_C, DV_C), lambda b, j, pt: (b, 0, 0, 0)),
        scratch_shapes=[pltpu.VMEM((n_rows, DV_C), BF16),
                        pltpu.VMEM((n_rows, 1), F32),
                        pltpu.VMEM((n_rows, 1), F32),
                        pltpu.VMEM((n_rows, DV_C), F32),
                        pltpu.VMEM((LANES, DV_C), F32),
                        pltpu.VMEM((LANES, DV_C), F32)],
    )
    y = pl.pallas_call(
        functools.partial(_dattn_sample_kernel, pps=pps, t=t, out_scale=1.0 - lam_init),
        grid_spec=grid_spec,
        out_shape=jax.ShapeDtypeStruct((bsz, t, H_C, DV_C), out_dtype),
        compiler_params=_params("parallel", "arbitrary"),
        name="dattn_sample",
    )(page_table.reshape(-1), jnp.full((1, DV_C), lam, F32), heads, heads, heads, heads,
      norm_g.reshape(1, DV_C), *([cache_k] * pps), *([cache_v] * pps))
    return y.reshape(bsz, t, H_C * DV_C)


def _mem_attn_kernel(q_ref, g_ref, mk_ref, mv_ref, o_ref, pad_scr, *, tq):
    rows = pad_scr.shape[0]
    if tq < rows:
        pad_scr[...] = jnp.zeros_like(pad_scr)
        pad_scr[0:tq, :] = q_ref[...]
        q_all = pad_scr[...]
    else:
        q_all = q_ref[...]
    nt = (((1,), (1,)), ((), ()))
    for h in range(H_M):
        cols = slice(h * DH_M, (h + 1) * DH_M)
        q = (q_all[:, cols] * (DH_M ** -0.5)).astype(BF16)
        s = lax.dot_general(q, mk_ref[:, cols].astype(BF16), nt, preferred_element_type=F32)
        e = jnp.exp(s - jnp.max(s, axis=-1, keepdims=True))
        l = jnp.sum(e, axis=-1, keepdims=True)
        o = jnp.dot(e.astype(BF16), mv_ref[:, cols].astype(BF16), preferred_element_type=F32) / l
        o_ref[:, cols] = (o[0:tq, :] * _silu(g_ref[:, cols])).astype(o_ref.dtype)


def mem_attn(z4, mem_k, mem_v, out_dtype):
    _, bsz, t, _ = z4.shape
    n_mem = mem_k.shape[1]
    tq = _row_tile(t, 256)
    rows = max(tq, 2 * SUBLANES)

    def tok(g):
        return pl.BlockSpec((None, None, tq, GROUP_W), lambda b, i: (g, b, i, 0))

    mem_spec = pl.BlockSpec((None, n_mem, GROUP_W), lambda b, i: (b, 0, 0))
    return pl.pallas_call(
        functools.partial(_mem_attn_kernel, tq=tq),
        grid=(bsz, t // tq),
        in_specs=[tok(G_QM), tok(G_GM), mem_spec, mem_spec],
        out_specs=pl.BlockSpec((None, tq, GROUP_W), lambda b, i: (b, i, 0)),
        out_shape=jax.ShapeDtypeStruct((bsz, t, GROUP_W), out_dtype),
        scratch_shapes=[pltpu.VMEM((rows, GROUP_W), F32)],
        compiler_params=_params("parallel", "parallel"),
        name="mem_attn",
    )(z4, z4, mem_k.reshape(bsz, n_mem, GROUP_W), mem_v.reshape(bsz, n_mem, GROUP_W))


def trunk_layer(x, lw, attend, mem_k, mem_v, s5_h0, hgrn_s0, part_dtype):
    bsz, t, d = x.shape
    m = bsz * t
    x2 = x.reshape(m, d)
    h = rmsnorm(x2, lw['norm_g'], BF16)
    z = matmul_groups(h, lw['w_in'], lw['layer'])
    z4 = z.reshape(N_GROUPS_IN, bsz, t, GROUP_W)

    bu = s5_bu(z, lw['w_bu'])
    h_all, s5_last = s5_scan(bu, lw['a_bar'], s5_h0, bsz, t)
    y_a = s5_y(h_all, z, lw['wc'], lw['s5_d'], lw['w_glu'], lw['b_glu'], part_dtype)

    y_b, hgrn_s = hgrn(z4, lw['lb'], lw['hgrn_norm_g'], hgrn_s0, part_dtype)
    y_c = attend(z4)
    y_m = mem_attn(z4, mem_k, mem_v, part_dtype)

    parts = (y_a, y_b.reshape(m, GROUP_W), y_c.reshape(m, GROUP_W), y_m.reshape(m, GROUP_W))
    x_out = outproj(parts, lw['w_out'], lw['layer'], x2).reshape(bsz, t, d)
    return x_out, z4, s5_last, hgrn_s


def kernel(x_prompt, x_sample, cache_k, cache_v, cache_mem_k, cache_mem_v, state_s5_re, state_s5_im, state_hgrn, page_table, mem_prompt, norm_g, w_in, s5_lambda_re, s5_lambda_im, s5_log_dt, s5_b_re, s5_b_im, s5_c_re, s5_c_im, s5_d, s5_w_glu, s5_b_glu, hgrn_lower_bounds, hgrn_norm_g, diff_lq1, diff_lk1, diff_lq2, diff_lk2, diff_norm_g, mem_norm_g, w_mem_kv, w_out, final_norm_g):
    depth = w_in.shape[0]
    bsz_p, t_p, d = x_prompt.shape
    bsz_s, t_s, _ = x_sample.shape
    n_mem = mem_prompt.shape[1]
    lb_all = jnp.cumsum(jax.nn.softmax(hgrn_lower_bounds.astype(F32), axis=0), axis=0)
    lb_all = lb_all - lb_all[0]
    zeros_s5 = jnp.zeros((2, bsz_p, G_A, N_A), F32)
    zeros_hgrn = jnp.zeros((bsz_p, H_B, DK_B, DV_B), F32)

    xp, xs = x_prompt, x_sample
    outs = {k: [] for k in ('kp', 'vp', 'ks', 'vs', 'mkp', 'mvp', 's5p', 's5s', 'hgp', 'hgs')}
    for l in range(depth):
        lam_init = 0.8 - 0.6 * math.exp(-0.3 * l)
        lam = (jnp.exp(jnp.sum(diff_lq1[l].astype(F32) * diff_lk1[l].astype(F32)))
               - jnp.exp(jnp.sum(diff_lq2[l].astype(F32) * diff_lk2[l].astype(F32))) + lam_init)
        a_bar, w_bu, wc = s5_weights(s5_lambda_re[l], s5_lambda_im[l], s5_log_dt[l],
                                     s5_b_re[l], s5_b_im[l], s5_c_re[l], s5_c_im[l])
        lw = {'layer': l, 'norm_g': norm_g[l], 'w_in': w_in, 'w_out': w_out,
              'a_bar': a_bar, 'w_bu': w_bu, 'wc': wc, 's5_d': s5_d[l],
              'w_glu': s5_w_glu[l].astype(BF16), 'b_glu': s5_b_glu[l],
              'lb': lb_all[l], 'hgrn_norm_g': hgrn_norm_g[l]}

        hm = rmsnorm(mem_prompt.reshape(bsz_p * n_mem, d), mem_norm_g[l], BF16)
        mkv = matmul_groups(hm, w_mem_kv, l)
        mk_p = mkv[0].reshape(bsz_p, n_mem, H_M, DH_M)
        mv_p = mkv[1].reshape(bsz_p, n_mem, H_M, DH_M)
        attend_p = functools.partial(dattn_prompt, lam=lam, norm_g=diff_norm_g[l],
                                     lam_init=lam_init, out_dtype=BF16)
        xp, z4, s5_last, hg = trunk_layer(xp, lw, attend_p, mk_p, mv_p, zeros_s5, zeros_hgrn, BF16)
        outs['kp'].append(z4[G_KC].reshape(bsz_p, t_p, H_C, 2 * DH_C))
        outs['vp'].append(z4[G_VC].reshape(bsz_p, t_p, H_C, DV_C))
        outs['mkp'].append(mk_p)
        outs['mvp'].append(mv_p)
        outs['s5p'].append(s5_last)
        outs['hgp'].append(hg)

        attend_s = functools.partial(dattn_sample, cache_k=cache_k, cache_v=cache_v, layer=l,
                                     page_table=page_table, lam=lam, norm_g=diff_norm_g[l],
                                     lam_init=lam_init, out_dtype=F32)
        s5_h0 = jnp.stack([state_s5_re[l].astype(F32), state_s5_im[l].astype(F32)])
        xs, z4, s5_last, hg = trunk_layer(xs, lw, attend_s, cache_mem_k[l], cache_mem_v[l],
                                          s5_h0, state_hgrn[l], F32)
        outs['ks'].append(z4[G_KC].reshape(bsz_s, t_s, H_C, 2 * DH_C))
        outs['vs'].append(z4[G_VC].reshape(bsz_s, t_s, H_C, DV_C))
        outs['s5s'].append(s5_last)
        outs['hgs'].append(hg)

    y_prompt = rmsnorm(xp.reshape(bsz_p * t_p, d), final_norm_g, F32).reshape(bsz_p, t_p, d)
    y_sample = rmsnorm(xs.reshape(bsz_s * t_s, d), final_norm_g, F32).reshape(bsz_s, t_s, d)
    s5p = jnp.stack(outs['s5p'])
    s5s = jnp.stack(outs['s5s'])
    return (y_prompt, y_sample,
            jnp.stack(outs['kp']), jnp.stack(outs['vp']), jnp.stack(outs['ks']), jnp.stack(outs['vs']),
            jnp.stack(outs['mkp']), jnp.stack(outs['mvp']),
            s5p[:, 0], s5p[:, 1], s5s[:, 0], s5s[:, 1],
            jnp.stack(outs['hgp']), jnp.stack(outs['hgs']))
```

```python
import functools
import math

import jax
import jax.numpy as jnp
from jax import lax
from jax.experimental import pallas as pl
from jax.experimental.pallas import tpu as pltpu

F32 = jnp.float32
BF16 = jnp.bfloat16

D_MODEL = 4096
GROUP_W = 1024
N_GROUPS_IN = 12
C_A = 16
G_A = GROUP_W // C_A
N_A = 64
S5_GROUPS_PER_DOT = 16
S5_CHUNKS = G_A // S5_GROUPS_PER_DOT
S5_STATE = G_A * N_A
DK_B = 128
DV_B = 128
H_B = GROUP_W // DK_B
DV_C = 128
DH_C = DV_C // 2
H_C = GROUP_W // DV_C
H_M = 4
DH_M = GROUP_W // H_M
EPS = 1e-6
HGRN_CHUNK = 64
HGRN_SUB = 16
SUBLANES = 8
LANES = 128
MXU_N = 256
VMEM_LIMIT = 56 * 1024 * 1024

(G_UA, G_GA, G_QB, G_FB, G_IB, G_GB, G_QC, G_KC, G_VC, G_GC, G_QM, G_GM) = range(N_GROUPS_IN)


def _params(*sem):
    return pltpu.CompilerParams(dimension_semantics=sem, vmem_limit_bytes=VMEM_LIMIT)


def _silu(x):
    return x * jax.nn.sigmoid(x)


def _row_tile(m, cap):
    return cap if m % cap == 0 else m


def _rmsnorm_kernel(x_ref, g_ref, o_ref):
    x = x_ref[...]
    y = x * lax.rsqrt(jnp.mean(x * x, axis=-1, keepdims=True) + EPS)
    o_ref[...] = (y * g_ref[...]).astype(o_ref.dtype)


def rmsnorm(x, g, out_dtype):
    m, d = x.shape
    tm = _row_tile(m, 256)
    return pl.pallas_call(
        _rmsnorm_kernel,
        grid=(m // tm,),
        in_specs=[pl.BlockSpec((tm, d), lambda i: (i, 0)),
                  pl.BlockSpec((1, d), lambda i: (0, 0))],
        out_specs=pl.BlockSpec((tm, d), lambda i: (i, 0)),
        out_shape=jax.ShapeDtypeStruct((m, d), out_dtype),
        compiler_params=_params("parallel"),
        name="rmsnorm",
    )(x, g.reshape(1, d))


def _matmul_kernel(a_ref, b_ref, o_ref):
    o_ref[...] = jnp.dot(a_ref[...], b_ref[...].astype(BF16), preferred_element_type=F32)


def matmul_groups(a, b, layer):
    m, kd = a.shape
    n = b.shape[2]
    tm = _row_tile(m, 2048)
    tn = MXU_N
    per_group = GROUP_W // tn
    return pl.pallas_call(
        _matmul_kernel,
        grid=(m // tm, n // tn),
        in_specs=[pl.BlockSpec((tm, kd), lambda i, j: (i, 0)),
                  pl.BlockSpec((None, kd, tn), lambda i, j: (layer, 0, j))],
        out_specs=pl.BlockSpec((None, tm, tn), lambda i, j: (j // per_group, i, j % per_group)),
        out_shape=jax.ShapeDtypeStruct((n // GROUP_W, m, GROUP_W), F32),
        compiler_params=_params("parallel", "arbitrary"),
        name="matmul_groups",
    )(a, b)


def _outproj_kernel(a0_ref, a1_ref, a2_ref, a3_ref, w_ref, x_ref, o_ref):
    acc = x_ref[...]
    for g, a_ref in enumerate((a0_ref, a1_ref, a2_ref, a3_ref)):
        acc += jnp.dot(a_ref[...].astype(BF16), w_ref[g * GROUP_W:(g + 1) * GROUP_W, :].astype(BF16),
                       preferred_element_type=F32)
    o_ref[...] = acc


def outproj(parts, w, layer, x):
    m, d = x.shape
    tm = _row_tile(m, 1024)
    tn = 2 * MXU_N
    part_spec = pl.BlockSpec((tm, GROUP_W), lambda i, j: (i, 0))
    return pl.pallas_call(
        _outproj_kernel,
        grid=(m // tm, d // tn),
        in_specs=[part_spec, part_spec, part_spec, part_spec,
                  pl.BlockSpec((None, 4 * GROUP_W, tn), lambda i, j: (layer, 0, j)),
                  pl.BlockSpec((tm, tn), lambda i, j: (i, j))],
        out_specs=pl.BlockSpec((tm, tn), lambda i, j: (i, j)),
        out_shape=jax.ShapeDtypeStruct((m, d), F32),
        compiler_params=_params("parallel", "arbitrary"),
        name="outproj",
    )(*parts, w, x)


def _s5_bu_kernel(u_ref, w_ref, o_ref):
    r = jnp.dot(u_ref[...].astype(BF16), w_ref[...], preferred_element_type=F32)
    half = r.shape[1] // 2
    o_ref[0] = r[:, :half]
    o_ref[1] = r[:, half:]


def s5_bu(z, w_bu):
    m = z.shape[1]
    tm = _row_tile(m, 512)
    kw = S5_GROUPS_PER_DOT * C_A
    nw = S5_GROUPS_PER_DOT * N_A
    return pl.pallas_call(
        _s5_bu_kernel,
        grid=(m // tm, S5_CHUNKS),
        in_specs=[pl.BlockSpec((None, tm, kw), lambda i, k: (G_UA, i, k)),
                  pl.BlockSpec((None, kw, 2 * nw), lambda i, k: (k, 0, 0))],
        out_specs=pl.BlockSpec((2, tm, nw), lambda i, k: (0, i, k)),
        out_shape=jax.ShapeDtypeStruct((2, m, S5_STATE), F32),
        compiler_params=_params("parallel", "parallel"),
        name="s5_bu",
    )(z, w_bu)


def _s5_scan_kernel(bu_ref, a_ref, h0_ref, h_ref, hl_ref, st_ref, *, tt):
    j = pl.program_id(1)

    @pl.when(j == 0)
    def _():
        st_ref[...] = h0_ref[...]

    a_re = a_ref[0]
    a_im = a_ref[1]

    def body(t, carry):
        h_re, h_im = carry
        n_re = a_re * h_re - a_im * h_im + bu_ref[0, t]
        n_im = a_re * h_im + a_im * h_re + bu_ref[1, t]
        h_ref[0, t] = n_re
        h_ref[1, t] = n_im
        return n_re, n_im

    h_re, h_im = lax.fori_loop(0, tt, body, (st_ref[0], st_ref[1]), unroll=min(tt, 8))
    st_ref[0] = h_re
    st_ref[1] = h_im

    @pl.when(j == pl.num_programs(1) - 1)
    def _():
        hl_ref[0] = h_re
        hl_ref[1] = h_im


def s5_scan(bu, a_bar, h0, bsz, t):
    rows = S5_STATE // LANES
    bu5 = bu.reshape(2, bsz, t, rows, LANES)
    tt = _row_tile(t, 128)
    h, h_last = pl.pallas_call(
        functools.partial(_s5_scan_kernel, tt=tt),
        grid=(bsz, t // tt),
        in_specs=[pl.BlockSpec((2, None, tt, rows, LANES), lambda b, j: (0, b, j, 0, 0)),
                  pl.BlockSpec((2, rows, LANES), lambda b, j: (0, 0, 0)),
                  pl.BlockSpec((2, None, rows, LANES), lambda b, j: (0, b, 0, 0))],
        out_specs=[pl.BlockSpec((2, None, tt, rows, LANES), lambda b, j: (0, b, j, 0, 0)),
                   pl.BlockSpec((2, None, rows, LANES), lambda b, j: (0, b, 0, 0))],
        out_shape=[jax.ShapeDtypeStruct((2, bsz, t, rows, LANES), F32),
                   jax.ShapeDtypeStruct((2, bsz, rows, LANES), F32)],
        scratch_shapes=[pltpu.VMEM((2, rows, LANES), F32)],
        compiler_params=_params("parallel", "arbitrary"),
        name="s5_scan",
    )(bu5, a_bar.reshape(2, rows, LANES), h0.reshape(2, bsz, rows, LANES))
    return h.reshape(2, bsz * t, S5_STATE), h_last.reshape(2, bsz, G_A, N_A)


def _s5_y_kernel(h_ref, u_ref, g_ref, wc_ref, d_ref, wg_ref, bg_ref, o_ref):
    nw = S5_GROUPS_PER_DOT * N_A
    parts = []
    for k in range(S5_CHUNKS):
        h_re = h_ref[0, :, k * nw:(k + 1) * nw].astype(BF16)
        h_im = h_ref[1, :, k * nw:(k + 1) * nw].astype(BF16)
        parts.append(jnp.dot(h_re, wc_ref[0, k], preferred_element_type=F32)
                     + jnp.dot(h_im, wc_ref[1, k], preferred_element_type=F32))
    y = jnp.concatenate(parts, axis=-1) + d_ref[...] * u_ref[...]
    y = jax.nn.gelu(y)
    glu = jnp.dot(y.astype(BF16), wg_ref[...], preferred_element_type=F32) + bg_ref[...]
    y = y * jax.nn.sigmoid(glu)
    o_ref[...] = (y * _silu(g_ref[...])).astype(o_ref.dtype)


def s5_y(h, z, wc, d, w_glu, b_glu, out_dtype):
    m = z.shape[1]
    tm = _row_tile(m, 256)
    kw = S5_GROUPS_PER_DOT * C_A
    nw = S5_GROUPS_PER_DOT * N_A
    return pl.pallas_call(
        _s5_y_kernel,
        grid=(m // tm,),
        in_specs=[pl.BlockSpec((2, tm, S5_STATE), lambda i: (0, i, 0)),
                  pl.BlockSpec((None, tm, GROUP_W), lambda i: (G_UA, i, 0)),
                  pl.BlockSpec((None, tm, GROUP_W), lambda i: (G_GA, i, 0)),
                  pl.BlockSpec((2, S5_CHUNKS, nw, kw), lambda i: (0, 0, 0, 0)),
                  pl.BlockSpec((1, GROUP_W), lambda i: (0, 0)),
                  pl.BlockSpec((GROUP_W, GROUP_W), lambda i: (0, 0)),
                  pl.BlockSpec((1, GROUP_W), lambda i: (0, 0))],
        out_specs=pl.BlockSpec((tm, GROUP_W), lambda i: (i, 0)),
        out_shape=jax.ShapeDtypeStruct((m, GROUP_W), out_dtype),
        compiler_params=_params("parallel"),
        name="s5_y",
    )(h, z, z, wc, d.reshape(1, GROUP_W), w_glu, b_glu.reshape(1, GROUP_W))


def _s5_fused_kernel(u_ref, g_ref, wbu_ref, pw_ref, h0_ref, wc_ref, d_ref, wg_ref, bg_ref,
                     o_ref, hl_ref, st_ref, h_scr, *, tt):
    j = pl.program_id(1)
    nw = S5_GROUPS_PER_DOT * N_A
    kw = S5_GROUPS_PER_DOT * C_A
    n_tiles = tt // SUBLANES

    @pl.when(j == 0)
    def _():
        st_ref[...] = h0_ref[...]

    def axpy(x_re, x_im, a_re, a_im, s_re, s_im):
        return x_re + (a_re * s_re - a_im * s_im), x_im + (a_re * s_im + a_im * s_re)

    u = u_ref[...]
    y_parts = []
    for k in range(S5_CHUNKS):
        cols = slice(k * nw, (k + 1) * nw)
        bu = jnp.dot(u[:, k * kw:(k + 1) * kw].astype(BF16), wbu_ref[k], preferred_element_type=F32)
        h_scr[0] = bu[:, :nw].reshape(n_tiles, SUBLANES, nw)
        h_scr[1] = bu[:, nw:].reshape(n_tiles, SUBLANES, nw)

        def tile_step(g, carry, cols=cols):
            c_re, c_im = carry
            x_re, x_im = h_scr[0, g], h_scr[1, g]
            for p, shift in enumerate((1, 2, 4)):
                x_re, x_im = axpy(x_re, x_im, pw_ref[0, p, :, cols], pw_ref[1, p, :, cols],
                                  pltpu.roll(x_re, shift, axis=0), pltpu.roll(x_im, shift, axis=0))
            x_re, x_im = axpy(x_re, x_im, pw_ref[0, 3, :, cols], pw_ref[1, 3, :, cols], c_re, c_im)
            h_scr[0, g] = x_re
            h_scr[1, g] = x_im
            return x_re[SUBLANES - 1:SUBLANES, :], x_im[SUBLANES - 1:SUBLANES, :]

        c_re, c_im = lax.fori_loop(0, n_tiles, tile_step, (st_ref[0, :, cols], st_ref[1, :, cols]),
                                   unroll=min(n_tiles, 4))
        st_ref[0, :, cols] = c_re
        st_ref[1, :, cols] = c_im
        h_re = h_scr[0].reshape(tt, nw).astype(BF16)
        h_im = h_scr[1].reshape(tt, nw).astype(BF16)
        y_parts.append(jnp.dot(h_re, wc_ref[0, k], preferred_element_type=F32)
                       + jnp.dot(h_im, wc_ref[1, k], preferred_element_type=F32))
    y = jnp.concatenate(y_parts, axis=-1) + d_ref[...] * u
    y = jax.nn.gelu(y)
    glu = jnp.dot(y.astype(BF16), wg_ref[...], preferred_element_type=F32) + bg_ref[...]
    y = y * jax.nn.sigmoid(glu)
    o_ref[...] = (y * _silu(g_ref[...])).astype(o_ref.dtype)

    @pl.when(j == pl.num_programs(1) - 1)
    def _():
        hl_ref[...] = st_ref[...]


def s5_fused(z4, lw, h0, out_dtype):
    _, bsz, t, _ = z4.shape
    tt = _row_tile(t, 256)
    kw = S5_GROUPS_PER_DOT * C_A
    nw = S5_GROUPS_PER_DOT * N_A

    def tok(g):
        return pl.BlockSpec((None, None, tt, GROUP_W), lambda b, j: (g, b, j, 0))

    def whole(shape):
        return pl.BlockSpec(shape, lambda b, j: (0,) * len(shape))

    state_spec = pl.BlockSpec((2, None, 1, S5_STATE), lambda b, j: (0, b, 0, 0))
    y, h_last = pl.pallas_call(
        functools.partial(_s5_fused_kernel, tt=tt),
        grid=(bsz, t // tt),
        in_specs=[tok(G_UA), tok(G_GA),
                  whole((S5_CHUNKS, kw, 2 * nw)),
                  whole((2, 4, SUBLANES, S5_STATE)),
                  state_spec,
                  whole((2, S5_CHUNKS, nw, kw)),
                  whole((1, GROUP_W)), whole((GROUP_W, GROUP_W)), whole((1, GROUP_W))],
        out_specs=[pl.BlockSpec((None, tt, GROUP_W), lambda b, j: (b, j, 0)), state_spec],
        out_shape=[jax.ShapeDtypeStruct((bsz, t, GROUP_W), out_dtype),
                   jax.ShapeDtypeStruct((2, bsz, 1, S5_STATE), F32)],
        scratch_shapes=[pltpu.VMEM((2, 1, S5_STATE), F32),
                        pltpu.VMEM((2, tt // SUBLANES, SUBLANES, nw), F32)],
        compiler_params=_params("parallel", "arbitrary"),
        name="s5_fused",
    )(z4, z4, lw['w_bu'], lw['a_pow'], h0.reshape(2, bsz, 1, S5_STATE), lw['wc'],
      lw['s5_d'].reshape(1, GROUP_W), lw['w_glu'], lw['b_glu'].reshape(1, GROUP_W))
    return y, h_last.reshape(2, bsz, G_A, N_A)


def s5_scan_multipliers(a_pair):
    a = lax.complex(a_pair[0], a_pair[1]).reshape(S5_STATE)
    powers = [a]
    for _ in range(SUBLANES - 1):
        powers.append(powers[-1] * a)
    row = jnp.arange(SUBLANES)[:, None]
    planes = [jnp.where(row >= s, powers[s - 1][None, :], 0.0) for s in (1, 2, 4)]
    planes.append(jnp.stack(powers))
    pw = jnp.stack(planes)
    return jnp.stack([pw.real, pw.imag])


def s5_weights(lam_re, lam_im, log_dt, b_re, b_im, c_re, c_im):
    lam = lax.complex(lam_re.astype(F32), lam_im.astype(F32))
    dt = jnp.exp(log_dt.astype(F32))[:, None]
    a_bar = jnp.exp(lam * dt)
    b_bar = ((a_bar - 1.0) / lam)[..., None] * lax.complex(b_re.astype(F32), b_im.astype(F32))
    eye = jnp.eye(S5_GROUPS_PER_DOT, dtype=F32)
    bb = jnp.stack([b_bar.real, b_bar.imag]).reshape(2, S5_CHUNKS, S5_GROUPS_PER_DOT, N_A, C_A)
    w_bu = jnp.einsum('pkgnc,gh->kgcphn', bb, eye).reshape(
        S5_CHUNKS, S5_GROUPS_PER_DOT * C_A, 2 * S5_GROUPS_PER_DOT * N_A).astype(BF16)
    cc = jnp.stack([c_re.astype(F32), -c_im.astype(F32)]).reshape(
        2, S5_CHUNKS, S5_GROUPS_PER_DOT, C_A, N_A)
    wc = jnp.einsum('pkgcn,gh->pkgnhc', cc, eye).reshape(
        2, S5_CHUNKS, S5_GROUPS_PER_DOT * N_A, S5_GROUPS_PER_DOT * C_A).astype(BF16)
    a_pair = jnp.stack([a_bar.real, a_bar.imag])
    return a_pair, w_bu, wc


def _hgrn_kernel(q_ref, f_ref, i_ref, g_ref, lb_ref, ng_ref, s0_ref, y_ref, sl_ref,
                 st_ref, pad_scr, *, tc, chunk, t_valid):
    sub = HGRN_SUB
    n_sb = chunk // sub
    j = pl.program_id(2)
    nt = (((1,), (1,)), ((), ()))

    @pl.when(j == 0)
    def _():
        st_ref[...] = s0_ref[...].T

    lb = lb_ref[...]
    padded = tc < chunk
    if padded:
        pad_scr[...] = jnp.zeros_like(pad_scr)
        pad_scr[0, 0:tc, :] = q_ref[...]
        pad_scr[1, 0:tc, :] = f_ref[...]
        pad_scr[2, 0:tc, :] = i_ref[...]
    row_c = lax.broadcasted_iota(jnp.int32, (chunk, LANES), 0)
    row_s = lax.broadcasted_iota(jnp.int32, (sub, LANES), 0)
    lane_s = lax.broadcasted_iota(jnp.int32, (sub, chunk), 1)
    tri = (lax.broadcasted_iota(jnp.int32, (chunk, chunk), 1)
           <= lax.broadcasted_iota(jnp.int32, (chunk, chunk), 0)).astype(F32)

    def chunk_step(ci, carry):
        r0 = pl.multiple_of(ci * chunk, chunk)
        if padded:
            q, f_pre, inp = pad_scr[0], pad_scr[1], pad_scr[2]
        else:
            q = q_ref[pl.ds(r0, chunk), :]
            f_pre = f_ref[pl.ds(r0, chunk), :]
            inp = i_ref[pl.ds(r0, chunk), :]
        f = lb + (1.0 - lb) * jax.nn.sigmoid(f_pre)
        log_f = jnp.log(f)
        kk = 1.0 - f
        if padded:
            log_f = jnp.where(row_c < t_valid, log_f, 0.0)
            kk = jnp.where(row_c < t_valid, kk, 0.0)
        cum = jnp.dot(tri, log_f, precision=lax.Precision.HIGHEST, preferred_element_type=F32)
        last = cum[chunk - 1:chunk, :]
        st = st_ref[...]
        o = lax.dot_general((q * jnp.exp(cum)).astype(BF16), st.astype(BF16), nt,
                            preferred_element_type=F32)
        inp_b = inp.astype(BF16)
        blocks = []
        for i in range(n_sb):
            rows = slice(i * sub, (i + 1) * sub)
            q_i, kk_i, cum_i = q[rows], kk[rows], cum[rows]
            if i > 0:
                edge = cum[i * sub - 1:i * sub, :]
                a_i = q_i * jnp.exp(cum_i - edge)
                kt_i = kk * jnp.exp(jnp.where(row_c < i * sub, edge - cum, -jnp.inf))
                sc = lax.dot_general(a_i.astype(BF16), kt_i.astype(BF16), nt, preferred_element_type=F32)
            else:
                sc = jnp.zeros((sub, chunk), F32)
            for s in range(min(sub, t_valid)):
                decay = jnp.exp(jnp.where(row_s >= s, cum_i - cum_i[s:s + 1, :], -jnp.inf))
                w = (q_i * kk_i[s:s + 1, :]) * decay
                sc = jnp.where(lane_s == i * sub + s, jnp.sum(w, axis=-1, keepdims=True), sc)
            blocks.append(sc)
        scores = blocks[0] if n_sb == 1 else jnp.concatenate(blocks, axis=0)
        o = o + jnp.dot(scores.astype(BF16), inp_b, preferred_element_type=F32)
        kt = (kk * jnp.exp(last - cum)).astype(BF16)
        upd = lax.dot_general(inp_b, kt, (((0,), (0,)), ((), ())), preferred_element_type=F32)
        st_ref[...] = st * jnp.exp(last) + upd
        y = o * lax.rsqrt(jnp.mean(o * o, axis=-1, keepdims=True) + EPS) * ng_ref[...]
        if padded:
            y_ref[...] = (y[0:tc] * _silu(g_ref[...])).astype(y_ref.dtype)
        else:
            y_ref[pl.ds(r0, chunk), :] = (y * _silu(g_ref[pl.ds(r0, chunk), :])).astype(y_ref.dtype)
        return carry

    lax.fori_loop(0, max(tc // chunk, 1), chunk_step, 0, unroll=True)

    @pl.when(j == pl.num_programs(2) - 1)
    def _():
        sl_ref[...] = st_ref[...].T


def hgrn(z4, lb, norm_g, s0, out_dtype):
    _, bsz, t, _ = z4.shape
    chunk = HGRN_CHUNK if t % HGRN_CHUNK == 0 else HGRN_SUB
    tc = _row_tile(t, 256) if t >= chunk else t
    t_valid = min(chunk, tc)

    def col(g):
        return pl.BlockSpec((None, None, tc, DK_B), lambda b, h, j: (g, b, j, h))

    state_spec = pl.BlockSpec((None, None, DK_B, DV_B), lambda b, h, j: (b, h, 0, 0))
    return pl.pallas_call(
        functools.partial(_hgrn_kernel, tc=tc, chunk=chunk, t_valid=t_valid),
        grid=(bsz, H_B, t // tc),
        in_specs=[col(G_QB), col(G_FB), col(G_IB), col(G_GB),
                  pl.BlockSpec((1, DK_B), lambda b, h, j: (0, h)),
                  pl.BlockSpec((1, DV_B), lambda b, h, j: (0, 0)),
                  state_spec],
        out_specs=[pl.BlockSpec((None, tc, DV_B), lambda b, h, j: (b, j, h)), state_spec],
        out_shape=[jax.ShapeDtypeStruct((bsz, t, GROUP_W), out_dtype),
                   jax.ShapeDtypeStruct((bsz, H_B, DK_B, DV_B), F32)],
        scratch_shapes=[pltpu.VMEM((DV_B, DK_B), F32),
                        pltpu.VMEM((3, chunk, LANES), F32)],
        compiler_params=_params("parallel", "parallel", "arbitrary"),
        name="hgrn",
    )(z4, z4, z4, z4, lb.reshape(1, GROUP_W), norm_g.reshape(1, DV_B), s0)


def _diff_norm_gate(o, ng, gate, out_scale):
    y = o * lax.rsqrt(jnp.mean(o * o, axis=-1, keepdims=True) + EPS) * ng
    return (y * out_scale) * _silu(gate)


def _dattn_prompt_kernel(lam_ref, q_ref, k_ref, v_ref, g_ref, ng_ref, o_ref,
                         kb_scr, vb_scr, *, tq, n_blk, out_scale):
    i = pl.program_id(2)

    @pl.when(i == 0)
    def _():
        kb_scr[...] = k_ref[...].astype(BF16)
        vb_scr[...] = v_ref[...].astype(BF16)

    nt = (((1,), (1,)), ((), ()))

    def attend(n_past):
        past = n_past * tq
        q = q_ref[...] * (DH_C ** -0.5)
        lane = lax.broadcasted_iota(jnp.int32, (tq, DV_C), 1)
        causal = (lax.broadcasted_iota(jnp.int32, (tq, tq), 1)
                  <= lax.broadcasted_iota(jnp.int32, (tq, tq), 0))
        k_diag = kb_scr[past:past + tq, :]
        v_diag = vb_scr[past:past + tq, :]

        def softmax_pv(qj):
            s_d = jnp.where(causal, lax.dot_general(qj, k_diag, nt, preferred_element_type=F32), -jnp.inf)
            m = jnp.max(s_d, axis=-1, keepdims=True)
            if n_past:
                s_p = lax.dot_general(qj, kb_scr[0:past, :], nt, preferred_element_type=F32)
                m = jnp.maximum(m, jnp.max(s_p, axis=-1, keepdims=True))
            e_d = jnp.exp(s_d - m)
            l = jnp.sum(e_d, axis=-1, keepdims=True)
            pv = jnp.dot(e_d.astype(BF16), v_diag, preferred_element_type=F32)
            if n_past:
                e_p = jnp.exp(s_p - m)
                l = l + jnp.sum(e_p, axis=-1, keepdims=True)
                pv = pv + jnp.dot(e_p.astype(BF16), vb_scr[0:past, :], preferred_element_type=F32)
            return pv / l

        o = (softmax_pv(jnp.where(lane < DH_C, q, 0.0).astype(BF16))
             - lam_ref[...] * softmax_pv(jnp.where(lane >= DH_C, q, 0.0).astype(BF16)))
        o_ref[...] = _diff_norm_gate(o, ng_ref[...], g_ref[...], out_scale).astype(o_ref.dtype)

    for n_past in range(n_blk):
        pl.when(i == n_past)(functools.partial(attend, n_past))


def dattn_prompt(z4, lam, norm_g, lam_init, out_dtype):
    _, bsz, t, _ = z4.shape
    tq = _row_tile(t, 256)

    def q_col(g):
        return pl.BlockSpec((None, None, tq, DV_C), lambda b, h, i: (g, b, i, h))

    def kv_col(g):
        return pl.BlockSpec((None, None, t, DV_C), lambda b, h, i: (g, b, 0, h))

    vec = pl.BlockSpec((1, DV_C), lambda b, h, i: (0, 0))
    return pl.pallas_call(
        functools.partial(_dattn_prompt_kernel, tq=tq, n_blk=t // tq, out_scale=1.0 - lam_init),
        grid=(bsz, H_C, t // tq),
        in_specs=[vec, q_col(G_QC), kv_col(G_KC), kv_col(G_VC), q_col(G_GC), vec],
        out_specs=pl.BlockSpec((None, tq, DV_C), lambda b, h, i: (b, i, h)),
        out_shape=jax.ShapeDtypeStruct((bsz, t, GROUP_W), out_dtype),
        scratch_shapes=[pltpu.VMEM((t, DV_C), BF16),
                        pltpu.VMEM((t, DV_C), BF16)],
        compiler_params=_params("parallel", "parallel", "arbitrary"),
        name="dattn_prompt",
    )(jnp.full((1, DV_C), lam, F32), z4, z4, z4, z4, norm_g.reshape(1, DV_C))


def _dattn_sample_kernel(pt_ref, lam_ref, q_ref, kn_ref, vn_ref, g_ref, ng_ref, *rest,
                         pps, t, out_scale):
    k_refs = rest[:pps]
    v_refs = rest[pps:2 * pps]
    o_ref = rest[2 * pps]
    qm_scr, m_scr, l_scr, acc_scr, kn_scr, vn_scr = rest[2 * pps + 1:]
    del pt_ref
    j = pl.program_id(1)
    half = t * H_C
    n_rows = 2 * half
    page_rows = k_refs[0].shape[0] * H_C
    nt = (((1,), (1,)), ((), ()))

    @pl.when(j == 0)
    def _():
        q = q_ref[...].reshape(half, DV_C) * (DH_C ** -0.5)
        lane = lax.broadcasted_iota(jnp.int32, (half, DV_C), 1)
        qm_scr[0:half, :] = jnp.where(lane < DH_C, q, 0.0).astype(BF16)
        qm_scr[half:n_rows, :] = jnp.where(lane >= DH_C, q, 0.0).astype(BF16)
        m_scr[...] = jnp.full_like(m_scr, -jnp.inf)
        l_scr[...] = jnp.zeros_like(l_scr)
        acc_scr[...] = jnp.zeros_like(acc_scr)

    def online_update(s, v_blocks, width):
        m_old = m_scr[...]
        m_new = jnp.maximum(m_old, jnp.max(s, axis=-1, keepdims=True))
        alpha = jnp.exp(m_old - m_new)
        e = jnp.exp(s - m_new)
        l_scr[...] = alpha * l_scr[...] + jnp.sum(e, axis=-1, keepdims=True)
        pv = None
        for r, vb in enumerate(v_blocks):
            d = jnp.dot(e[:, r * width:(r + 1) * width].astype(BF16), vb, preferred_element_type=F32)
            pv = d if pv is None else pv + d
        acc_scr[...] = alpha * acc_scr[...] + pv
        m_scr[...] = m_new

    qm = qm_scr[...]
    same_head = (lax.broadcasted_iota(jnp.int32, (n_rows, page_rows), 0) % H_C
                 == lax.broadcasted_iota(jnp.int32, (n_rows, page_rows), 1) % H_C)
    s_parts = []
    for r in range(pps):
        kp = k_refs[r][...].reshape(page_rows, DV_C).astype(BF16)
        s_parts.append(jnp.where(same_head, lax.dot_general(qm, kp, nt, preferred_element_type=F32),
                                 -jnp.inf))
    online_update(jnp.concatenate(s_parts, axis=-1),
                  [v_refs[r][...].reshape(page_rows, DV_C).astype(BF16) for r in range(pps)], page_rows)

    @pl.when(j == pl.num_programs(1) - 1)
    def _():
        kn_scr[...] = jnp.zeros_like(kn_scr)
        vn_scr[...] = jnp.zeros_like(vn_scr)
        kn_scr[0:half, :] = kn_ref[...].reshape(half, DV_C)
        vn_scr[0:half, :] = vn_ref[...].reshape(half, DV_C)
        s_new = lax.dot_general(qm, kn_scr[...].astype(BF16), nt, preferred_element_type=F32)
        row = lax.broadcasted_iota(jnp.int32, (n_rows, LANES), 0)
        col = lax.broadcasted_iota(jnp.int32, (n_rows, LANES), 1)
        visible = (col % H_C == row % H_C) & (col // H_C <= (row // H_C) % t)
        online_update(jnp.where(visible, s_new, -jnp.inf), [vn_scr[...].astype(BF16)], LANES)

        o = (acc_scr[0:half, :] / l_scr[0:half, :]
             - lam_ref[...] * (acc_scr[half:n_rows, :] / l_scr[half:n_rows, :]))
        y = _diff_norm_gate(o, ng_ref[...], g_ref[...].reshape(half, DV_C), out_scale)
        o_ref[...] = y.reshape(t, H_C, DV_C).astype(o_ref.dtype)


def dattn_sample(z4, cache_k, cache_v, layer, page_table, lam, norm_g, lam_init, out_dtype):
    _, bsz, t, _ = z4.shape
    page = cache_k.shape[2]
    n_pages = page_table.shape[1]
    pps = 8 if n_pages % 8 == 0 else 1
    heads = z4[G_QC:G_GC + 1].reshape(4, bsz, t, H_C, DV_C)

    def tok(g):
        return pl.BlockSpec((None, None, t, H_C, DV_C), lambda b, j, pt: (g, b, 0, 0, 0))

    def page_spec(r):
        return pl.BlockSpec((None, None, page, H_C, DV_C),
                            lambda b, j, pt: (layer, pt[b * n_pages + j * pps + r], 0, 0, 0))

    vec = pl.BlockSpec((1, DV_C), lambda b, j, pt: (0, 0))
    n_rows = 2 * t * H_C
    grid_spec = pltpu.PrefetchScalarGridSpec(
        num_scalar_prefetch=1,
        grid=(bsz, n_pages // pps),
        in_specs=[vec, tok(0), tok(1), tok(2), tok(3), vec]
        + [page_spec(r) for r in range(pps)] + [page_spec(r) for r in range(pps)],
        out_specs=pl.BlockSpec((None, t, H_C, DV_C), lambda b, j, pt: (b, 0, 0, 0)),
        scratch_shapes=[pltpu.VMEM((n_rows, DV_C), BF16),
                        pltpu.VMEM((n_rows, 1), F32),
                        pltpu.VMEM((n_rows, 1), F32),
                        pltpu.VMEM((n_rows, DV_C), F32),
                        pltpu.VMEM((LANES, DV_C), F32),
                        pltpu.VMEM((LANES, DV_C), F32)],
    )
    y = pl.pallas_call(
        functools.partial(_dattn_sample_kernel, pps=pps, t=t, out_scale=1.0 - lam_init),
        grid_spec=grid_spec,
        out_shape=jax.ShapeDtypeStruct((bsz, t, H_C, DV_C), out_dtype),
        compiler_params=_params("parallel", "arbitrary"),
        name="dattn_sample",
    )(page_table.reshape(-1), jnp.full((1, DV_C), lam, F32), heads, heads, heads, heads,
      norm_g.reshape(1, DV_C), *([cache_k] * pps), *([cache_v] * pps))
    return y.reshape(bsz, t, H_C * DV_C)


def _mem_attn_kernel(q_ref, g_ref, mk_ref, mv_ref, o_ref, pad_scr, *, tq):
    rows = pad_scr.shape[0]
    if tq < rows:
        pad_scr[...] = jnp.zeros_like(pad_scr)
        pad_scr[0:tq, :] = q_ref[...]
        q_all = pad_scr[...]
    else:
        q_all = q_ref[...]
    nt = (((1,), (1,)), ((), ()))
    for h in range(H_M):
        cols = slice(h * DH_M, (h + 1) * DH_M)
        q = (q_all[:, cols] * (DH_M ** -0.5)).astype(BF16)
        s = lax.dot_general(q, mk_ref[:, cols].astype(BF16), nt, preferred_element_type=F32)
        e = jnp.exp(s - jnp.max(s, axis=-1, keepdims=True))
        l = jnp.sum(e, axis=-1, keepdims=True)
        o = jnp.dot(e.astype(BF16), mv_ref[:, cols].astype(BF16), preferred_element_type=F32) / l
        o_ref[:, cols] = (o[0:tq, :] * _silu(g_ref[:, cols])).astype(o_ref.dtype)


def mem_attn(z4, mem_k, mem_v, out_dtype):
    _, bsz, t, _ = z4.shape
    n_mem = mem_k.shape[1]
    tq = _row_tile(t, 256)
    rows = max(tq, 2 * SUBLANES)

    def tok(g):
        return pl.BlockSpec((None, None, tq, GROUP_W), lambda b, i: (g, b, i, 0))

    mem_spec = pl.BlockSpec((None, n_mem, GROUP_W), lambda b, i: (b, 0, 0))
    return pl.pallas_call(
        functools.partial(_mem_attn_kernel, tq=tq),
        grid=(bsz, t // tq),
        in_specs=[tok(G_QM), tok(G_GM), mem_spec, mem_spec],
        out_specs=pl.BlockSpec((None, tq, GROUP_W), lambda b, i: (b, i, 0)),
        out_shape=jax.ShapeDtypeStruct((bsz, t, GROUP_W), out_dtype),
        scratch_shapes=[pltpu.VMEM((rows, GROUP_W), F32)],
        compiler_params=_params("parallel", "parallel"),
        name="mem_attn",
    )(z4, z4, mem_k.reshape(bsz, n_mem, GROUP_W), mem_v.reshape(bsz, n_mem, GROUP_W))


def trunk_layer(x, lw, attend, mem_k, mem_v, s5_h0, hgrn_s0, part_dtype):
    bsz, t, d = x.shape
    m = bsz * t
    x2 = x.reshape(m, d)
    h = rmsnorm(x2, lw['norm_g'], BF16)
    z = matmul_groups(h, lw['w_in'], lw['layer'])
    z4 = z.reshape(N_GROUPS_IN, bsz, t, GROUP_W)

    if t % SUBLANES == 0:
        y_a, s5_last = s5_fused(z4, lw, s5_h0, part_dtype)
        y_a = y_a.reshape(m, GROUP_W)
    else:
        bu = s5_bu(z, lw['w_bu'])
        h_all, s5_last = s5_scan(bu, lw['a_bar'], s5_h0, bsz, t)
        y_a = s5_y(h_all, z, lw['wc'], lw['s5_d'], lw['w_glu'], lw['b_glu'], part_dtype)

    y_b, hgrn_s = hgrn(z4, lw['lb'], lw['hgrn_norm_g'], hgrn_s0, part_dtype)
    y_c = attend(z4)
    y_m = mem_attn(z4, mem_k, mem_v, part_dtype)

    parts = (y_a, y_b.reshape(m, GROUP_W), y_c.reshape(m, GROUP_W), y_m.reshape(m, GROUP_W))
    x_out = outproj(parts, lw['w_out'], lw['layer'], x2).reshape(bsz, t, d)
    return x_out, z4, s5_last, hgrn_s


def kernel(x_prompt, x_sample, cache_k, cache_v, cache_mem_k, cache_mem_v, state_s5_re, state_s5_im, state_hgrn, page_table, mem_prompt, norm_g, w_in, s5_lambda_re, s5_lambda_im, s5_log_dt, s5_b_re, s5_b_im, s5_c_re, s5_c_im, s5_d, s5_w_glu, s5_b_glu, hgrn_lower_bounds, hgrn_norm_g, diff_lq1, diff_lk1, diff_lq2, diff_lk2, diff_norm_g, mem_norm_g, w_mem_kv, w_out, final_norm_g):
    depth = w_in.shape[0]
    bsz_p, t_p, d = x_prompt.shape
    bsz_s, t_s, _ = x_sample.shape
    n_mem = mem_prompt.shape[1]
    lb_all = jnp.cumsum(jax.nn.softmax(hgrn_lower_bounds.astype(F32), axis=0), axis=0)
    lb_all = lb_all - lb_all[0]
    zeros_s5 = jnp.zeros((2, bsz_p, G_A, N_A), F32)
    zeros_hgrn = jnp.zeros((bsz_p, H_B, DK_B, DV_B), F32)

    xp, xs = x_prompt, x_sample
    outs = {k: [] for k in ('kp', 'vp', 'ks', 'vs', 'mkp', 'mvp', 's5p', 's5s', 'hgp', 'hgs')}
    for l in range(depth):
        lam_init = 0.8 - 0.6 * math.exp(-0.3 * l)
        lam = (jnp.exp(jnp.sum(diff_lq1[l].astype(F32) * diff_lk1[l].astype(F32)))
               - jnp.exp(jnp.sum(diff_lq2[l].astype(F32) * diff_lk2[l].astype(F32))) + lam_init)
        a_bar, w_bu, wc = s5_weights(s5_lambda_re[l], s5_lambda_im[l], s5_log_dt[l],
                                     s5_b_re[l], s5_b_im[l], s5_c_re[l], s5_c_im[l])
        lw = {'layer': l, 'norm_g': norm_g[l], 'w_in': w_in, 'w_out': w_out,
              'a_bar': a_bar, 'a_pow': s5_scan_multipliers(a_bar), 'w_bu': w_bu, 'wc': wc, 's5_d': s5_d[l],
              'w_glu': s5_w_glu[l].astype(BF16), 'b_glu': s5_b_glu[l],
              'lb': lb_all[l], 'hgrn_norm_g': hgrn_norm_g[l]}

        hm = rmsnorm(mem_prompt.reshape(bsz_p * n_mem, d), mem_norm_g[l], BF16)
        mkv = matmul_groups(hm, w_mem_kv, l)
        mk_p = mkv[0].reshape(bsz_p, n_mem, H_M, DH_M)
        mv_p = mkv[1].reshape(bsz_p, n_mem, H_M, DH_M)
        attend_p = functools.partial(dattn_prompt, lam=lam, norm_g=diff_norm_g[l],
                                     lam_init=lam_init, out_dtype=BF16)
        xp, z4, s5_last, hg = trunk_layer(xp, lw, attend_p, mk_p, mv_p, zeros_s5, zeros_hgrn, BF16)
        outs['kp'].append(z4[G_KC].reshape(bsz_p, t_p, H_C, 2 * DH_C))
        outs['vp'].append(z4[G_VC].reshape(bsz_p, t_p, H_C, DV_C))
        outs['mkp'].append(mk_p)
        outs['mvp'].append(mv_p)
        outs['s5p'].append(s5_last)
        outs['hgp'].append(hg)

        attend_s = functools.partial(dattn_sample, cache_k=cache_k, cache_v=cache_v, layer=l,
                                     page_table=page_table, lam=lam, norm_g=diff_norm_g[l],
                                     lam_init=lam_init, out_dtype=F32)
        s5_h0 = jnp.stack([state_s5_re[l].astype(F32), state_s5_im[l].astype(F32)])
        xs, z4, s5_last, hg = trunk_layer(xs, lw, attend_s, cache_mem_k[l], cache_mem_v[l],
                                          s5_h0, state_hgrn[l], F32)
        outs['ks'].append(z4[G_KC].reshape(bsz_s, t_s, H_C, 2 * DH_C))
        outs['vs'].append(z4[G_VC].reshape(bsz_s, t_s, H_C, DV_C))
        outs['s5s'].append(s5_last)
        outs['hgs'].append(hg)

    y_prompt = rmsnorm(xp.reshape(bsz_p * t_p, d), final_norm_g, F32).reshape(bsz_p, t_p, d)
    y_sample = rmsnorm(xs.reshape(bsz_s * t_s, d), final_norm_g, F32).reshape(bsz_s, t_s, d)
    s5p = jnp.stack(outs['s5p'])
    s5s = jnp.stack(outs['s5s'])
    return (y_prompt, y_sample,
            jnp.stack(outs['kp']), jnp.stack(outs['vp']), jnp.stack(outs['ks']), jnp.stack(outs['vs']),
            jnp.stack(outs['mkp']), jnp.stack(outs['mvp']),
            s5p[:, 0], s5p[:, 1], s5s[:, 0], s5s[:, 1],
            jnp.stack(outs['hgp']), jnp.stack(outs['hgs']))
```

```python
import functools
import math

import jax
import jax.numpy as jnp
from jax import lax
from jax.experimental import pallas as pl
from jax.experimental.pallas import tpu as pltpu

F32 = jnp.float32
BF16 = jnp.bfloat16

D_MODEL = 4096
GROUP_W = 1024
N_GROUPS_IN = 12
C_A = 16
G_A = GROUP_W // C_A
N_A = 64
S5_GROUPS_PER_DOT = 16
S5_CHUNKS = G_A // S5_GROUPS_PER_DOT
S5_STATE = G_A * N_A
DK_B = 128
DV_B = 128
H_B = GROUP_W // DK_B
DV_C = 128
DH_C = DV_C // 2
H_C = GROUP_W // DV_C
H_M = 4
DH_M = GROUP_W // H_M
EPS = 1e-6
HGRN_CHUNK = 64
HGRN_SUB = 16
SUBLANES = 8
LANES = 128
MXU_N = 256
VMEM_LIMIT = 56 * 1024 * 1024

(G_UA, G_GA, G_QB, G_FB, G_IB, G_GB, G_QC, G_KC, G_VC, G_GC, G_QM, G_GM) = range(N_GROUPS_IN)


def _params(*sem):
    return pltpu.CompilerParams(dimension_semantics=sem, vmem_limit_bytes=VMEM_LIMIT)


def _silu(x):
    return x * jax.nn.sigmoid(x)


def _row_tile(m, cap):
    return cap if m % cap == 0 else m


def _rmsnorm_kernel(x_ref, g_ref, o_ref):
    x = x_ref[...]
    y = x * lax.rsqrt(jnp.mean(x * x, axis=-1, keepdims=True) + EPS)
    o_ref[...] = (y * g_ref[...]).astype(o_ref.dtype)


def rmsnorm(x, g, out_dtype):
    m, d = x.shape
    tm = _row_tile(m, 256)
    return pl.pallas_call(
        _rmsnorm_kernel,
        grid=(m // tm,),
        in_specs=[pl.BlockSpec((tm, d), lambda i: (i, 0)),
                  pl.BlockSpec((1, d), lambda i: (0, 0))],
        out_specs=pl.BlockSpec((tm, d), lambda i: (i, 0)),
        out_shape=jax.ShapeDtypeStruct((m, d), out_dtype),
        compiler_params=_params("parallel"),
        name="rmsnorm",
    )(x, g.reshape(1, d))


def _matmul_kernel(*refs):
    n_lhs = len(refs) // 2
    w = refs[n_lhs][...].astype(BF16)
    for a_ref, o_ref in zip(refs[:n_lhs], refs[n_lhs + 1:]):
        o_ref[...] = jnp.dot(a_ref[...], w, preferred_element_type=F32)


def matmul_groups(a, b, layer, a_small=None):
    m, kd = a.shape
    n = b.shape[2]
    tm = _row_tile(m, 2048)
    tn = MXU_N
    per_group = GROUP_W // tn
    lhs = [a] if a_small is None else [a, a_small]
    in_specs = [pl.BlockSpec((tm, kd), lambda i, j: (i, 0))]
    out_specs = [pl.BlockSpec((None, tm, tn), lambda i, j: (j // per_group, i, j % per_group))]
    out_shape = [jax.ShapeDtypeStruct((n // GROUP_W, m, GROUP_W), F32)]
    if a_small is not None:
        m2 = a_small.shape[0]
        in_specs.append(pl.BlockSpec((m2, kd), lambda i, j: (0, 0)))
        out_specs.append(pl.BlockSpec((None, m2, tn), lambda i, j: (j // per_group, 0, j % per_group)))
        out_shape.append(jax.ShapeDtypeStruct((n // GROUP_W, m2, GROUP_W), F32))
    in_specs.append(pl.BlockSpec((None, kd, tn), lambda i, j: (layer, 0, j)))
    outs = pl.pallas_call(
        _matmul_kernel,
        grid=(m // tm, n // tn),
        in_specs=in_specs,
        out_specs=out_specs,
        out_shape=out_shape,
        compiler_params=_params("arbitrary", "arbitrary"),
        name="matmul_groups",
    )(*lhs, b)
    return outs[0] if a_small is None else outs


def _outproj_kernel(a0_ref, a1_ref, a2_ref, a3_ref, w_ref, x_ref, o_ref):
    acc = x_ref[...]
    for g, a_ref in enumerate((a0_ref, a1_ref, a2_ref, a3_ref)):
        acc += jnp.dot(a_ref[...].astype(BF16), w_ref[g * GROUP_W:(g + 1) * GROUP_W, :].astype(BF16),
                       preferred_element_type=F32)
    o_ref[...] = acc


def outproj(parts, w, layer, x):
    m, d = x.shape
    tm = _row_tile(m, 1024)
    tn = 2 * MXU_N
    part_spec = pl.BlockSpec((tm, GROUP_W), lambda i, j: (i, 0))
    return pl.pallas_call(
        _outproj_kernel,
        grid=(m // tm, d // tn),
        in_specs=[part_spec, part_spec, part_spec, part_spec,
                  pl.BlockSpec((None, 4 * GROUP_W, tn), lambda i, j: (layer, 0, j)),
                  pl.BlockSpec((tm, tn), lambda i, j: (i, j))],
        out_specs=pl.BlockSpec((tm, tn), lambda i, j: (i, j)),
        out_shape=jax.ShapeDtypeStruct((m, d), F32),
        compiler_params=_params("parallel", "arbitrary"),
        name="outproj",
    )(*parts, w, x)


def _s5_bu_kernel(u_ref, w_ref, o_ref):
    r = jnp.dot(u_ref[...].astype(BF16), w_ref[...], preferred_element_type=F32)
    half = r.shape[1] // 2
    o_ref[0] = r[:, :half]
    o_ref[1] = r[:, half:]


def s5_bu(z, w_bu):
    m = z.shape[1]
    tm = _row_tile(m, 512)
    kw = S5_GROUPS_PER_DOT * C_A
    nw = S5_GROUPS_PER_DOT * N_A
    return pl.pallas_call(
        _s5_bu_kernel,
        grid=(m // tm, S5_CHUNKS),
        in_specs=[pl.BlockSpec((None, tm, kw), lambda i, k: (G_UA, i, k)),
                  pl.BlockSpec((None, kw, 2 * nw), lambda i, k: (k, 0, 0))],
        out_specs=pl.BlockSpec((2, tm, nw), lambda i, k: (0, i, k)),
        out_shape=jax.ShapeDtypeStruct((2, m, S5_STATE), F32),
        compiler_params=_params("parallel", "parallel"),
        name="s5_bu",
    )(z, w_bu)


def _s5_scan_kernel(bu_ref, a_ref, h0_ref, h_ref, hl_ref, st_ref, *, tt):
    j = pl.program_id(1)

    @pl.when(j == 0)
    def _():
        st_ref[...] = h0_ref[...]

    a_re = a_ref[0]
    a_im = a_ref[1]

    def body(t, carry):
        h_re, h_im = carry
        n_re = a_re * h_re - a_im * h_im + bu_ref[0, t]
        n_im = a_re * h_im + a_im * h_re + bu_ref[1, t]
        h_ref[0, t] = n_re
        h_ref[1, t] = n_im
        return n_re, n_im

    h_re, h_im = lax.fori_loop(0, tt, body, (st_ref[0], st_ref[1]), unroll=min(tt, 8))
    st_ref[0] = h_re
    st_ref[1] = h_im

    @pl.when(j == pl.num_programs(1) - 1)
    def _():
        hl_ref[0] = h_re
        hl_ref[1] = h_im


def s5_scan(bu, a_bar, h0, bsz, t):
    rows = S5_STATE // LANES
    bu5 = bu.reshape(2, bsz, t, rows, LANES)
    tt = _row_tile(t, 128)
    h, h_last = pl.pallas_call(
        functools.partial(_s5_scan_kernel, tt=tt),
        grid=(bsz, t // tt),
        in_specs=[pl.BlockSpec((2, None, tt, rows, LANES), lambda b, j: (0, b, j, 0, 0)),
                  pl.BlockSpec((2, rows, LANES), lambda b, j: (0, 0, 0)),
                  pl.BlockSpec((2, None, rows, LANES), lambda b, j: (0, b, 0, 0))],
        out_specs=[pl.BlockSpec((2, None, tt, rows, LANES), lambda b, j: (0, b, j, 0, 0)),
                   pl.BlockSpec((2, None, rows, LANES), lambda b, j: (0, b, 0, 0))],
        out_shape=[jax.ShapeDtypeStruct((2, bsz, t, rows, LANES), F32),
                   jax.ShapeDtypeStruct((2, bsz, rows, LANES), F32)],
        scratch_shapes=[pltpu.VMEM((2, rows, LANES), F32)],
        compiler_params=_params("parallel", "arbitrary"),
        name="s5_scan",
    )(bu5, a_bar.reshape(2, rows, LANES), h0.reshape(2, bsz, rows, LANES))
    return h.reshape(2, bsz * t, S5_STATE), h_last.reshape(2, bsz, G_A, N_A)


def _s5_y_kernel(h_ref, u_ref, g_ref, wc_ref, d_ref, wg_ref, bg_ref, o_ref):
    nw = S5_GROUPS_PER_DOT * N_A
    parts = []
    for k in range(S5_CHUNKS):
        h_re = h_ref[0, :, k * nw:(k + 1) * nw].astype(BF16)
        h_im = h_ref[1, :, k * nw:(k + 1) * nw].astype(BF16)
        parts.append(jnp.dot(h_re, wc_ref[0, k], preferred_element_type=F32)
                     + jnp.dot(h_im, wc_ref[1, k], preferred_element_type=F32))
    y = jnp.concatenate(parts, axis=-1) + d_ref[...] * u_ref[...]
    y = jax.nn.gelu(y)
    glu = jnp.dot(y.astype(BF16), wg_ref[...], preferred_element_type=F32) + bg_ref[...]
    y = y * jax.nn.sigmoid(glu)
    o_ref[...] = (y * _silu(g_ref[...])).astype(o_ref.dtype)


def s5_y(h, z, wc, d, w_glu, b_glu, out_dtype):
    m = z.shape[1]
    tm = _row_tile(m, 256)
    kw = S5_GROUPS_PER_DOT * C_A
    nw = S5_GROUPS_PER_DOT * N_A
    return pl.pallas_call(
        _s5_y_kernel,
        grid=(m // tm,),
        in_specs=[pl.BlockSpec((2, tm, S5_STATE), lambda i: (0, i, 0)),
                  pl.BlockSpec((None, tm, GROUP_W), lambda i: (G_UA, i, 0)),
                  pl.BlockSpec((None, tm, GROUP_W), lambda i: (G_GA, i, 0)),
                  pl.BlockSpec((2, S5_CHUNKS, nw, kw), lambda i: (0, 0, 0, 0)),
                  pl.BlockSpec((1, GROUP_W), lambda i: (0, 0)),
                  pl.BlockSpec((GROUP_W, GROUP_W), lambda i: (0, 0)),
                  pl.BlockSpec((1, GROUP_W), lambda i: (0, 0))],
        out_specs=pl.BlockSpec((tm, GROUP_W), lambda i: (i, 0)),
        out_shape=jax.ShapeDtypeStruct((m, GROUP_W), out_dtype),
        compiler_params=_params("parallel"),
        name="s5_y",
    )(h, z, z, wc, d.reshape(1, GROUP_W), w_glu, b_glu.reshape(1, GROUP_W))


def _s5_fused_kernel(u_ref, g_ref, wbu_ref, pw_ref, h0_ref, wc_ref, d_ref, wg_ref, bg_ref,
                     o_ref, hl_ref, st_ref, h_scr, *, tt):
    j = pl.program_id(1)
    nw = S5_GROUPS_PER_DOT * N_A
    kw = S5_GROUPS_PER_DOT * C_A
    n_tiles = tt // SUBLANES

    @pl.when(j == 0)
    def _():
        st_ref[...] = h0_ref[...]

    def axpy(x_re, x_im, a_re, a_im, s_re, s_im):
        return x_re + (a_re * s_re - a_im * s_im), x_im + (a_re * s_im + a_im * s_re)

    u = u_ref[...]
    y_parts = []
    for k in range(S5_CHUNKS):
        cols = slice(k * nw, (k + 1) * nw)
        bu = jnp.dot(u[:, k * kw:(k + 1) * kw].astype(BF16), wbu_ref[k], preferred_element_type=F32)
        h_scr[0] = bu[:, :nw].reshape(n_tiles, SUBLANES, nw)
        h_scr[1] = bu[:, nw:].reshape(n_tiles, SUBLANES, nw)

        def tile_step(g, carry, cols=cols):
            c_re, c_im = carry
            x_re, x_im = h_scr[0, g], h_scr[1, g]
            for p, shift in enumerate((1, 2, 4)):
                x_re, x_im = axpy(x_re, x_im, pw_ref[0, p, :, cols], pw_ref[1, p, :, cols],
                                  pltpu.roll(x_re, shift, axis=0), pltpu.roll(x_im, shift, axis=0))
            x_re, x_im = axpy(x_re, x_im, pw_ref[0, 3, :, cols], pw_ref[1, 3, :, cols], c_re, c_im)
            h_scr[0, g] = x_re
            h_scr[1, g] = x_im
            return x_re[SUBLANES - 1:SUBLANES, :], x_im[SUBLANES - 1:SUBLANES, :]

        c_re, c_im = lax.fori_loop(0, n_tiles, tile_step, (st_ref[0, :, cols], st_ref[1, :, cols]),
                                   unroll=min(n_tiles, 4))
        st_ref[0, :, cols] = c_re
        st_ref[1, :, cols] = c_im
        h_re = h_scr[0].reshape(tt, nw).astype(BF16)
        h_im = h_scr[1].reshape(tt, nw).astype(BF16)
        y_parts.append(jnp.dot(h_re, wc_ref[0, k], preferred_element_type=F32)
                       + jnp.dot(h_im, wc_ref[1, k], preferred_element_type=F32))
    y = jnp.concatenate(y_parts, axis=-1) + d_ref[...] * u
    y = jax.nn.gelu(y)
    glu = jnp.dot(y.astype(BF16), wg_ref[...], preferred_element_type=F32) + bg_ref[...]
    y = y * jax.nn.sigmoid(glu)
    o_ref[...] = (y * _silu(g_ref[...])).astype(o_ref.dtype)

    @pl.when(j == pl.num_programs(1) - 1)
    def _():
        hl_ref[...] = st_ref[...]


def s5_fused(z4, lw, h0, out_dtype):
    _, bsz, t, _ = z4.shape
    tt = _row_tile(t, 512)
    kw = S5_GROUPS_PER_DOT * C_A
    nw = S5_GROUPS_PER_DOT * N_A

    def tok(g):
        return pl.BlockSpec((None, None, tt, GROUP_W), lambda b, j: (g, b, j, 0))

    def whole(shape):
        return pl.BlockSpec(shape, lambda b, j: (0,) * len(shape))

    state_spec = pl.BlockSpec((2, None, 1, S5_STATE), lambda b, j: (0, b, 0, 0))
    y, h_last = pl.pallas_call(
        functools.partial(_s5_fused_kernel, tt=tt),
        grid=(bsz, t // tt),
        in_specs=[tok(G_UA), tok(G_GA),
                  whole((S5_CHUNKS, kw, 2 * nw)),
                  whole((2, 4, SUBLANES, S5_STATE)),
                  state_spec,
                  whole((2, S5_CHUNKS, nw, kw)),
                  whole((1, GROUP_W)), whole((GROUP_W, GROUP_W)), whole((1, GROUP_W))],
        out_specs=[pl.BlockSpec((None, tt, GROUP_W), lambda b, j: (b, j, 0)), state_spec],
        out_shape=[jax.ShapeDtypeStruct((bsz, t, GROUP_W), out_dtype),
                   jax.ShapeDtypeStruct((2, bsz, 1, S5_STATE), F32)],
        scratch_shapes=[pltpu.VMEM((2, 1, S5_STATE), F32),
                        pltpu.VMEM((2, tt // SUBLANES, SUBLANES, nw), F32)],
        compiler_params=_params("parallel", "arbitrary"),
        name="s5_fused",
    )(z4, z4, lw['w_bu'], lw['a_pow'], h0.reshape(2, bsz, 1, S5_STATE), lw['wc'],
      lw['s5_d'].reshape(1, GROUP_W), lw['w_glu'], lw['b_glu'].reshape(1, GROUP_W))
    return y, h_last.reshape(2, bsz, G_A, N_A)


def s5_scan_multipliers(a_pair):
    a = lax.complex(a_pair[0], a_pair[1]).reshape(S5_STATE)
    powers = [a]
    for _ in range(SUBLANES - 1):
        powers.append(powers[-1] * a)
    row = jnp.arange(SUBLANES)[:, None]
    planes = [jnp.where(row >= s, powers[s - 1][None, :], 0.0) for s in (1, 2, 4)]
    planes.append(jnp.stack(powers))
    pw = jnp.stack(planes)
    return jnp.stack([pw.real, pw.imag])


def s5_weights(lam_re, lam_im, log_dt, b_re, b_im, c_re, c_im):
    lam = lax.complex(lam_re.astype(F32), lam_im.astype(F32))
    dt = jnp.exp(log_dt.astype(F32))[:, None]
    a_bar = jnp.exp(lam * dt)
    b_bar = ((a_bar - 1.0) / lam)[..., None] * lax.complex(b_re.astype(F32), b_im.astype(F32))
    eye = jnp.eye(S5_GROUPS_PER_DOT, dtype=F32)
    bb = jnp.stack([b_bar.real, b_bar.imag]).reshape(2, S5_CHUNKS, S5_GROUPS_PER_DOT, N_A, C_A)
    w_bu = jnp.einsum('pkgnc,gh->kgcphn', bb, eye).reshape(
        S5_CHUNKS, S5_GROUPS_PER_DOT * C_A, 2 * S5_GROUPS_PER_DOT * N_A).astype(BF16)
    cc = jnp.stack([c_re.astype(F32), -c_im.astype(F32)]).reshape(
        2, S5_CHUNKS, S5_GROUPS_PER_DOT, C_A, N_A)
    wc = jnp.einsum('pkgcn,gh->pkgnhc', cc, eye).reshape(
        2, S5_CHUNKS, S5_GROUPS_PER_DOT * N_A, S5_GROUPS_PER_DOT * C_A).astype(BF16)
    a_pair = jnp.stack([a_bar.real, a_bar.imag])
    return a_pair, w_bu, wc


def _hgrn_kernel(q_ref, f_ref, i_ref, g_ref, lb_ref, ng_ref, s0_ref, y_ref, sl_ref,
                 st_ref, pad_scr, *, tc, chunk, t_valid):
    sub = HGRN_SUB
    n_sb = chunk // sub
    j = pl.program_id(2)
    nt = (((1,), (1,)), ((), ()))

    @pl.when(j == 0)
    def _():
        st_ref[...] = s0_ref[...].T

    lb = lb_ref[...]
    padded = tc < chunk
    if padded:
        pad_scr[...] = jnp.zeros_like(pad_scr)
        pad_scr[0, 0:tc, :] = q_ref[...]
        pad_scr[1, 0:tc, :] = f_ref[...]
        pad_scr[2, 0:tc, :] = i_ref[...]
    row_c = lax.broadcasted_iota(jnp.int32, (chunk, LANES), 0)
    row_s = lax.broadcasted_iota(jnp.int32, (sub, LANES), 0)
    lane_s = lax.broadcasted_iota(jnp.int32, (sub, chunk), 1)
    tri = (lax.broadcasted_iota(jnp.int32, (chunk, chunk), 1)
           <= lax.broadcasted_iota(jnp.int32, (chunk, chunk), 0)).astype(F32)

    def chunk_step(ci, carry):
        r0 = pl.multiple_of(ci * chunk, chunk)
        if padded:
            q, f_pre, inp = pad_scr[0], pad_scr[1], pad_scr[2]
        else:
            q = q_ref[pl.ds(r0, chunk), :]
            f_pre = f_ref[pl.ds(r0, chunk), :]
            inp = i_ref[pl.ds(r0, chunk), :]
        f = lb + (1.0 - lb) * jax.nn.sigmoid(f_pre)
        log_f = jnp.log(f)
        kk = 1.0 - f
        if padded:
            log_f = jnp.where(row_c < t_valid, log_f, 0.0)
            kk = jnp.where(row_c < t_valid, kk, 0.0)
        cum = jnp.dot(tri, log_f, precision=lax.Precision.HIGHEST, preferred_element_type=F32)
        last = cum[chunk - 1:chunk, :]
        st = st_ref[...]
        o = lax.dot_general((q * jnp.exp(cum)).astype(BF16), st.astype(BF16), nt,
                            preferred_element_type=F32)
        inp_b = inp.astype(BF16)
        blocks = []
        for i in range(n_sb):
            rows = slice(i * sub, (i + 1) * sub)
            q_i, kk_i, cum_i = q[rows], kk[rows], cum[rows]
            if i > 0:
                edge = cum[i * sub - 1:i * sub, :]
                a_i = q_i * jnp.exp(cum_i - edge)
                kt_i = kk * jnp.exp(jnp.where(row_c < i * sub, edge - cum, -jnp.inf))
                sc = lax.dot_general(a_i.astype(BF16), kt_i.astype(BF16), nt, preferred_element_type=F32)
            else:
                sc = jnp.zeros((sub, chunk), F32)
            for s in range(min(sub, t_valid)):
                decay = jnp.exp(jnp.where(row_s >= s, cum_i - cum_i[s:s + 1, :], -jnp.inf))
                w = (q_i * kk_i[s:s + 1, :]) * decay
                sc = jnp.where(lane_s == i * sub + s, jnp.sum(w, axis=-1, keepdims=True), sc)
            blocks.append(sc)
        scores = blocks[0] if n_sb == 1 else jnp.concatenate(blocks, axis=0)
        o = o + jnp.dot(scores.astype(BF16), inp_b, preferred_element_type=F32)
        kt = (kk * jnp.exp(last - cum)).astype(BF16)
        upd = lax.dot_general(inp_b, kt, (((0,), (0,)), ((), ())), preferred_element_type=F32)
        st_ref[...] = st * jnp.exp(last) + upd
        y = o * lax.rsqrt(jnp.mean(o * o, axis=-1, keepdims=True) + EPS) * ng_ref[...]
        if padded:
            y_ref[...] = (y[0:tc] * _silu(g_ref[...])).astype(y_ref.dtype)
        else:
            y_ref[pl.ds(r0, chunk), :] = (y * _silu(g_ref[pl.ds(r0, chunk), :])).astype(y_ref.dtype)
        return carry

    lax.fori_loop(0, max(tc // chunk, 1), chunk_step, 0, unroll=True)

    @pl.when(j == pl.num_programs(2) - 1)
    def _():
        sl_ref[...] = st_ref[...].T


def hgrn(z4, lb, norm_g, s0, out_dtype):
    _, bsz, t, _ = z4.shape
    chunk = HGRN_CHUNK if t % HGRN_CHUNK == 0 else HGRN_SUB
    tc = _row_tile(t, 512) if t >= chunk else t
    t_valid = min(chunk, tc)

    def col(g):
        return pl.BlockSpec((None, None, tc, DK_B), lambda b, h, j: (g, b, j, h))

    state_spec = pl.BlockSpec((None, None, DK_B, DV_B), lambda b, h, j: (b, h, 0, 0))
    return pl.pallas_call(
        functools.partial(_hgrn_kernel, tc=tc, chunk=chunk, t_valid=t_valid),
        grid=(bsz, H_B, t // tc),
        in_specs=[col(G_QB), col(G_FB), col(G_IB), col(G_GB),
                  pl.BlockSpec((1, DK_B), lambda b, h, j: (0, h)),
                  pl.BlockSpec((1, DV_B), lambda b, h, j: (0, 0)),
                  state_spec],
        out_specs=[pl.BlockSpec((None, tc, DV_B), lambda b, h, j: (b, j, h)), state_spec],
        out_shape=[jax.ShapeDtypeStruct((bsz, t, GROUP_W), out_dtype),
                   jax.ShapeDtypeStruct((bsz, H_B, DK_B, DV_B), F32)],
        scratch_shapes=[pltpu.VMEM((DV_B, DK_B), F32),
                        pltpu.VMEM((3, chunk, LANES), F32)],
        compiler_params=_params("parallel", "parallel", "arbitrary"),
        name="hgrn",
    )(z4, z4, z4, z4, lb.reshape(1, GROUP_W), norm_g.reshape(1, DV_B), s0)


def _diff_norm_gate(o, ng, gate, out_scale):
    y = o * lax.rsqrt(jnp.mean(o * o, axis=-1, keepdims=True) + EPS) * ng
    return (y * out_scale) * _silu(gate)


def _dattn_prompt_kernel(lam_ref, q_ref, k_ref, v_ref, g_ref, ng_ref, o_ref,
                         kb_scr, vb_scr, *, tq, n_blk, out_scale):
    i = pl.program_id(2)

    @pl.when(i == 0)
    def _():
        kb_scr[...] = k_ref[...].astype(BF16)
        vb_scr[...] = v_ref[...].astype(BF16)

    nt = (((1,), (1,)), ((), ()))

    def attend(n_past):
        past = n_past * tq
        q = q_ref[...] * (DH_C ** -0.5)
        lane = lax.broadcasted_iota(jnp.int32, (tq, DV_C), 1)
        causal = (lax.broadcasted_iota(jnp.int32, (tq, tq), 1)
                  <= lax.broadcasted_iota(jnp.int32, (tq, tq), 0))
        k_diag = kb_scr[past:past + tq, :]
        v_diag = vb_scr[past:past + tq, :]

        def softmax_pv(qj):
            s_d = jnp.where(causal, lax.dot_general(qj, k_diag, nt, preferred_element_type=F32), -jnp.inf)
            m = jnp.max(s_d, axis=-1, keepdims=True)
            if n_past:
                s_p = lax.dot_general(qj, kb_scr[0:past, :], nt, preferred_element_type=F32)
                m = jnp.maximum(m, jnp.max(s_p, axis=-1, keepdims=True))
            e_d = jnp.exp(s_d - m)
            l = jnp.sum(e_d, axis=-1, keepdims=True)
            pv = jnp.dot(e_d.astype(BF16), v_diag, preferred_element_type=F32)
            if n_past:
                e_p = jnp.exp(s_p - m)
                l = l + jnp.sum(e_p, axis=-1, keepdims=True)
                pv = pv + jnp.dot(e_p.astype(BF16), vb_scr[0:past, :], preferred_element_type=F32)
            return pv / l

        o = (softmax_pv(jnp.where(lane < DH_C, q, 0.0).astype(BF16))
             - lam_ref[...] * softmax_pv(jnp.where(lane >= DH_C, q, 0.0).astype(BF16)))
        o_ref[...] = _diff_norm_gate(o, ng_ref[...], g_ref[...], out_scale).astype(o_ref.dtype)

    for n_past in range(n_blk):
        pl.when(i == n_past)(functools.partial(attend, n_past))


def dattn_prompt(z4, lam, norm_g, lam_init, out_dtype):
    _, bsz, t, _ = z4.shape
    tq = _row_tile(t, 256)

    def q_col(g):
        return pl.BlockSpec((None, None, tq, DV_C), lambda b, h, i: (g, b, i, h))

    def kv_col(g):
        return pl.BlockSpec((None, None, t, DV_C), lambda b, h, i: (g, b, 0, h))

    vec = pl.BlockSpec((1, DV_C), lambda b, h, i: (0, 0))
    return pl.pallas_call(
        functools.partial(_dattn_prompt_kernel, tq=tq, n_blk=t // tq, out_scale=1.0 - lam_init),
        grid=(bsz, H_C, t // tq),
        in_specs=[vec, q_col(G_QC), kv_col(G_KC), kv_col(G_VC), q_col(G_GC), vec],
        out_specs=pl.BlockSpec((None, tq, DV_C), lambda b, h, i: (b, i, h)),
        out_shape=jax.ShapeDtypeStruct((bsz, t, GROUP_W), out_dtype),
        scratch_shapes=[pltpu.VMEM((t, DV_C), BF16),
                        pltpu.VMEM((t, DV_C), BF16)],
        compiler_params=_params("parallel", "parallel", "arbitrary"),
        name="dattn_prompt",
    )(jnp.full((1, DV_C), lam, F32), z4, z4, z4, z4, norm_g.reshape(1, DV_C))


def _dattn_sample_kernel(pt_ref, lam_ref, q_ref, kn_ref, vn_ref, g_ref, ng_ref, *rest,
                         pps, t, out_scale):
    k_refs = rest[:pps]
    v_refs = rest[pps:2 * pps]
    o_ref = rest[2 * pps]
    qm_scr, m_scr, l_scr, acc_scr, kn_scr, vn_scr = rest[2 * pps + 1:]
    del pt_ref
    j = pl.program_id(1)
    half = t * H_C
    n_rows = 2 * half
    page_rows = k_refs[0].shape[0] * H_C
    nt = (((1,), (1,)), ((), ()))

    @pl.when(j == 0)
    def _():
        q = q_ref[...].reshape(half, DV_C) * (DH_C ** -0.5)
        lane = lax.broadcasted_iota(jnp.int32, (half, DV_C), 1)
        qm_scr[0:half, :] = jnp.where(lane < DH_C, q, 0.0).astype(BF16)
        qm_scr[half:n_rows, :] = jnp.where(lane >= DH_C, q, 0.0).astype(BF16)
        m_scr[...] = jnp.full_like(m_scr, -jnp.inf)
        l_scr[...] = jnp.zeros_like(l_scr)
        acc_scr[...] = jnp.zeros_like(acc_scr)

    def online_update(s, v_blocks, width):
        m_old = m_scr[...]
        m_new = jnp.maximum(m_old, jnp.max(s, axis=-1, keepdims=True))
        alpha = jnp.exp(m_old - m_new)
        e = jnp.exp(s - m_new)
        l_scr[...] = alpha * l_scr[...] + jnp.sum(e, axis=-1, keepdims=True)
        pv = None
        for r, vb in enumerate(v_blocks):
            d = jnp.dot(e[:, r * width:(r + 1) * width].astype(BF16), vb, preferred_element_type=F32)
            pv = d if pv is None else pv + d
        acc_scr[...] = alpha * acc_scr[...] + pv
        m_scr[...] = m_new

    qm = qm_scr[...]
    same_head = (lax.broadcasted_iota(jnp.int32, (n_rows, page_rows), 0) % H_C
                 == lax.broadcasted_iota(jnp.int32, (n_rows, page_rows), 1) % H_C)
    s_parts = []
    for r in range(pps):
        kp = k_refs[r][...].reshape(page_rows, DV_C).astype(BF16)
        s_parts.append(jnp.where(same_head, lax.dot_general(qm, kp, nt, preferred_element_type=F32),
                                 -jnp.inf))
    online_update(jnp.concatenate(s_parts, axis=-1),
                  [v_refs[r][...].reshape(page_rows, DV_C).astype(BF16) for r in range(pps)], page_rows)

    @pl.when(j == pl.num_programs(1) - 1)
    def _():
        kn_scr[...] = jnp.zeros_like(kn_scr)
        vn_scr[...] = jnp.zeros_like(vn_scr)
        kn_scr[0:half, :] = kn_ref[...].reshape(half, DV_C)
        vn_scr[0:half, :] = vn_ref[...].reshape(half, DV_C)
        s_new = lax.dot_general(qm, kn_scr[...].astype(BF16), nt, preferred_element_type=F32)
        row = lax.broadcasted_iota(jnp.int32, (n_rows, LANES), 0)
        col = lax.broadcasted_iota(jnp.int32, (n_rows, LANES), 1)
        visible = (col % H_C == row % H_C) & (col // H_C <= (row // H_C) % t)
        online_update(jnp.where(visible, s_new, -jnp.inf), [vn_scr[...].astype(BF16)], LANES)

        o = (acc_scr[0:half, :] / l_scr[0:half, :]
             - lam_ref[...] * (acc_scr[half:n_rows, :] / l_scr[half:n_rows, :]))
        y = _diff_norm_gate(o, ng_ref[...], g_ref[...].reshape(half, DV_C), out_scale)
        o_ref[...] = y.reshape(t, H_C, DV_C).astype(o_ref.dtype)


def dattn_sample(z4, cache_k, cache_v, layer, page_table, lam, norm_g, lam_init, out_dtype):
    _, bsz, t, _ = z4.shape
    page = cache_k.shape[2]
    n_pages = page_table.shape[1]
    pps = 8 if n_pages % 8 == 0 else 1
    heads = z4[G_QC:G_GC + 1].reshape(4, bsz, t, H_C, DV_C)

    def tok(g):
        return pl.BlockSpec((None, None, t, H_C, DV_C), lambda b, j, pt: (g, b, 0, 0, 0))

    def page_spec(r):
        return pl.BlockSpec((None, None, page, H_C, DV_C),
                            lambda b, j, pt: (layer, pt[b * n_pages + j * pps + r], 0, 0, 0))

    vec = pl.BlockSpec((1, DV_C), lambda b, j, pt: (0, 0))
    n_rows = 2 * t * H_C
    grid_spec = pltpu.PrefetchScalarGridSpec(
        num_scalar_prefetch=1,
        grid=(bsz, n_pages // pps),
        in_specs=[vec, tok(0), tok(1), tok(2), tok(3), vec]
        + [page_spec(r) for r in range(pps)] + [page_spec(r) for r in range(pps)],
        out_specs=pl.BlockSpec((None, t, H_C, DV_C), lambda b, j, pt: (b, 0, 0, 0)),
        scratch_shapes=[pltpu.VMEM((n_rows, DV_C), BF16),
                        pltpu.VMEM((n_rows, 1), F32),
                        pltpu.VMEM((n_rows, 1), F32),
                        pltpu.VMEM((n_rows, DV_C), F32),
                        pltpu.VMEM((LANES, DV_C), F32),
                        pltpu.VMEM((LANES, DV_C), F32)],
    )
    y = pl.pallas_call(
        functools.partial(_dattn_sample_kernel, pps=pps, t=t, out_scale=1.0 - lam_init),
        grid_spec=grid_spec,
        out_shape=jax.ShapeDtypeStruct((bsz, t, H_C, DV_C), out_dtype),
        compiler_params=_params("parallel", "arbitrary"),
        name="dattn_sample",
    )(page_table.reshape(-1), jnp.full((1, DV_C), lam, F32), heads, heads, heads, heads,
      norm_g.reshape(1, DV_C), *([cache_k] * pps), *([cache_v] * pps))
    return y.reshape(bsz, t, H_C * DV_C)


def _mem_attn_kernel(q_ref, g_ref, mk_ref, mv_ref, o_ref, pad_scr, *, tq):
    rows = pad_scr.shape[0]
    if tq < rows:
        pad_scr[...] = jnp.zeros_like(pad_scr)
        pad_scr[0:tq, :] = q_ref[...]
        q_all = pad_scr[...]
    else:
        q_all = q_ref[...]
    nt = (((1,), (1,)), ((), ()))
    for h in range(H_M):
        cols = slice(h * DH_M, (h + 1) * DH_M)
        q = (q_all[:, cols] * (DH_M ** -0.5)).astype(BF16)
        s = lax.dot_general(q, mk_ref[:, cols].astype(BF16), nt, preferred_element_type=F32)
        e = jnp.exp(s - jnp.max(s, axis=-1, keepdims=True))
        l = jnp.sum(e, axis=-1, keepdims=True)
        o = jnp.dot(e.astype(BF16), mv_ref[:, cols].astype(BF16), preferred_element_type=F32) / l
        o_ref[:, cols] = (o[0:tq, :] * _silu(g_ref[:, cols])).astype(o_ref.dtype)


def mem_attn(z4, mem_k, mem_v, out_dtype):
    _, bsz, t, _ = z4.shape
    n_mem = mem_k.shape[1]
    tq = _row_tile(t, 256)
    rows = max(tq, 2 * SUBLANES)

    def tok(g):
        return pl.BlockSpec((None, None, tq, GROUP_W), lambda b, i: (g, b, i, 0))

    mem_spec = pl.BlockSpec((None, n_mem, GROUP_W), lambda b, i: (b, 0, 0))
    return pl.pallas_call(
        functools.partial(_mem_attn_kernel, tq=tq),
        grid=(bsz, t // tq),
        in_specs=[tok(G_QM), tok(G_GM), mem_spec, mem_spec],
        out_specs=pl.BlockSpec((None, tq, GROUP_W), lambda b, i: (b, i, 0)),
        out_shape=jax.ShapeDtypeStruct((bsz, t, GROUP_W), out_dtype),
        scratch_shapes=[pltpu.VMEM((rows, GROUP_W), F32)],
        compiler_params=_params("parallel", "parallel"),
        name="mem_attn",
    )(z4, z4, mem_k.reshape(bsz, n_mem, GROUP_W), mem_v.reshape(bsz, n_mem, GROUP_W))


def input_projection(x_p, x_s, lw):
    d = x_p.shape[-1]
    h_p = rmsnorm(x_p.reshape(-1, d), lw['norm_g'], BF16)
    h_s = rmsnorm(x_s.reshape(-1, d), lw['norm_g'], BF16)
    return matmul_groups(h_p, lw['w_in'], lw['layer'], a_small=h_s)


def trunk_layer(x, z, lw, attend, mem_k, mem_v, s5_h0, hgrn_s0, part_dtype):
    bsz, t, d = x.shape
    m = bsz * t
    x2 = x.reshape(m, d)
    z4 = z.reshape(N_GROUPS_IN, bsz, t, GROUP_W)

    if t % SUBLANES == 0:
        y_a, s5_last = s5_fused(z4, lw, s5_h0, part_dtype)
        y_a = y_a.reshape(m, GROUP_W)
    else:
        bu = s5_bu(z, lw['w_bu'])
        h_all, s5_last = s5_scan(bu, lw['a_bar'], s5_h0, bsz, t)
        y_a = s5_y(h_all, z, lw['wc'], lw['s5_d'], lw['w_glu'], lw['b_glu'], part_dtype)

    y_b, hgrn_s = hgrn(z4, lw['lb'], lw['hgrn_norm_g'], hgrn_s0, part_dtype)
    y_c = attend(z4)
    y_m = mem_attn(z4, mem_k, mem_v, part_dtype)

    parts = (y_a, y_b.reshape(m, GROUP_W), y_c.reshape(m, GROUP_W), y_m.reshape(m, GROUP_W))
    x_out = outproj(parts, lw['w_out'], lw['layer'], x2).reshape(bsz, t, d)
    return x_out, z4, s5_last, hgrn_s


def kernel(x_prompt, x_sample, cache_k, cache_v, cache_mem_k, cache_mem_v, state_s5_re, state_s5_im, state_hgrn, page_table, mem_prompt, norm_g, w_in, s5_lambda_re, s5_lambda_im, s5_log_dt, s5_b_re, s5_b_im, s5_c_re, s5_c_im, s5_d, s5_w_glu, s5_b_glu, hgrn_lower_bounds, hgrn_norm_g, diff_lq1, diff_lk1, diff_lq2, diff_lk2, diff_norm_g, mem_norm_g, w_mem_kv, w_out, final_norm_g):
    depth = w_in.shape[0]
    bsz_p, t_p, d = x_prompt.shape
    bsz_s, t_s, _ = x_sample.shape
    n_mem = mem_prompt.shape[1]
    lb_all = jnp.cumsum(jax.nn.softmax(hgrn_lower_bounds.astype(F32), axis=0), axis=0)
    lb_all = lb_all - lb_all[0]
    zeros_s5 = jnp.zeros((2, bsz_p, G_A, N_A), F32)
    zeros_hgrn = jnp.zeros((bsz_p, H_B, DK_B, DV_B), F32)

    lam_inits = [0.8 - 0.6 * math.exp(-0.3 * l) for l in range(depth)]
    lams = (jnp.exp(jnp.sum(diff_lq1.astype(F32) * diff_lk1.astype(F32), axis=-1))
            - jnp.exp(jnp.sum(diff_lq2.astype(F32) * diff_lk2.astype(F32), axis=-1))
            + jnp.asarray(lam_inits, F32))
    a_bars, w_bus, wcs = jax.vmap(s5_weights)(s5_lambda_re, s5_lambda_im, s5_log_dt,
                                              s5_b_re, s5_b_im, s5_c_re, s5_c_im)
    a_pows = jax.vmap(s5_scan_multipliers)(a_bars)
    w_glus = s5_w_glu.astype(BF16)

    xp, xs = x_prompt, x_sample
    outs = {k: [] for k in ('kp', 'vp', 'ks', 'vs', 'mkp', 'mvp', 's5p', 's5s', 'hgp', 'hgs')}
    for l in range(depth):
        lam_init = lam_inits[l]
        lam = lams[l]
        lw = {'layer': l, 'norm_g': norm_g[l], 'w_in': w_in, 'w_out': w_out,
              'a_bar': a_bars[l], 'a_pow': a_pows[l], 'w_bu': w_bus[l], 'wc': wcs[l], 's5_d': s5_d[l],
              'w_glu': w_glus[l], 'b_glu': s5_b_glu[l],
              'lb': lb_all[l], 'hgrn_norm_g': hgrn_norm_g[l]}
        z_p, z_s = input_projection(xp, xs, lw)

        hm = rmsnorm(mem_prompt.reshape(bsz_p * n_mem, d), mem_norm_g[l], BF16)
        mkv = matmul_groups(hm, w_mem_kv, l)
        mk_p = mkv[0].reshape(bsz_p, n_mem, H_M, DH_M)
        mv_p = mkv[1].reshape(bsz_p, n_mem, H_M, DH_M)
        attend_p = functools.partial(dattn_prompt, lam=lam, norm_g=diff_norm_g[l],
                                     lam_init=lam_init, out_dtype=BF16)
        xp, z4, s5_last, hg = trunk_layer(xp, z_p, lw, attend_p, mk_p, mv_p, zeros_s5, zeros_hgrn, BF16)
        outs['kp'].append(z4[G_KC].reshape(bsz_p, t_p, H_C, 2 * DH_C))
        outs['vp'].append(z4[G_VC].reshape(bsz_p, t_p, H_C, DV_C))
        outs['mkp'].append(mk_p)
        outs['mvp'].append(mv_p)
        outs['s5p'].append(s5_last)
        outs['hgp'].append(hg)

        attend_s = functools.partial(dattn_sample, cache_k=cache_k, cache_v=cache_v, layer=l,
                                     page_table=page_table, lam=lam, norm_g=diff_norm_g[l],
                                     lam_init=lam_init, out_dtype=F32)
        s5_h0 = jnp.stack([state_s5_re[l].astype(F32), state_s5_im[l].astype(F32)])
        xs, z4, s5_last, hg = trunk_layer(xs, z_s, lw, attend_s, cache_mem_k[l], cache_mem_v[l],
                                          s5_h0, state_hgrn[l], F32)
        outs['ks'].append(z4[G_KC].reshape(bsz_s, t_s, H_C, 2 * DH_C))
        outs['vs'].append(z4[G_VC].reshape(bsz_s, t_s, H_C, DV_C))
        outs['s5s'].append(s5_last)
        outs['hgs'].append(hg)

    y_prompt = rmsnorm(xp.reshape(bsz_p * t_p, d), final_norm_g, F32).reshape(bsz_p, t_p, d)
    y_sample = rmsnorm(xs.reshape(bsz_s * t_s, d), final_norm_g, F32).reshape(bsz_s, t_s, d)
    s5p = jnp.stack(outs['s5p'])
    s5s = jnp.stack(outs['s5s'])
    return (y_prompt, y_sample,
            jnp.stack(outs['kp']), jnp.stack(outs['vp']), jnp.stack(outs['ks']), jnp.stack(outs['vs']),
            jnp.stack(outs['mkp']), jnp.stack(outs['mvp']),
            s5p[:, 0], s5p[:, 1], s5s[:, 0], s5s[:, 1],
            jnp.stack(outs['hgp']), jnp.stack(outs['hgs']))
```

```python
import functools
import math

import jax
import jax.numpy as jnp
from jax import lax
from jax.experimental import pallas as pl
from jax.experimental.pallas import tpu as pltpu

F32 = jnp.float32
BF16 = jnp.bfloat16

D_MODEL = 4096
GROUP_W = 1024
N_GROUPS_IN = 12
C_A = 16
G_A = GROUP_W // C_A
N_A = 64
S5_GROUPS_PER_DOT = 16
S5_CHUNKS = G_A // S5_GROUPS_PER_DOT
S5_STATE = G_A * N_A
DK_B = 128
DV_B = 128
H_B = GROUP_W // DK_B
DV_C = 128
DH_C = DV_C // 2
H_C = GROUP_W // DV_C
H_M = 4
DH_M = GROUP_W // H_M
EPS = 1e-6
LOG2_E = math.log2(math.e)
HGRN_CHUNK = 64
HGRN_SUB = 16
SUBLANES = 8
LANES = 128
MXU_N = 256
VMEM_LIMIT = 56 * 1024 * 1024

(G_UA, G_GA, G_QB, G_FB, G_IB, G_GB, G_QC, G_KC, G_VC, G_GC, G_QM, G_GM) = range(N_GROUPS_IN)


def _params(*sem):
    return pltpu.CompilerParams(dimension_semantics=sem, vmem_limit_bytes=VMEM_LIMIT)


def _silu(x):
    return x * jax.nn.sigmoid(x)


def _row_tile(m, cap):
    return cap if m % cap == 0 else m


def _rmsnorm_kernel(x_ref, g_ref, o_ref):
    x = x_ref[...]
    y = x * lax.rsqrt(jnp.mean(x * x, axis=-1, keepdims=True) + EPS)
    o_ref[...] = (y * g_ref[...]).astype(o_ref.dtype)


def rmsnorm(x, g, out_dtype):
    m, d = x.shape
    tm = _row_tile(m, 256)
    return pl.pallas_call(
        _rmsnorm_kernel,
        grid=(m // tm,),
        in_specs=[pl.BlockSpec((tm, d), lambda i: (i, 0)),
                  pl.BlockSpec((1, d), lambda i: (0, 0))],
        out_specs=pl.BlockSpec((tm, d), lambda i: (i, 0)),
        out_shape=jax.ShapeDtypeStruct((m, d), out_dtype),
        compiler_params=_params("parallel"),
        name="rmsnorm",
    )(x, g.reshape(1, d))


def _matmul_kernel(*refs):
    n_lhs = len(refs) // 2
    w = refs[n_lhs][...].astype(BF16)
    for a_ref, o_ref in zip(refs[:n_lhs], refs[n_lhs + 1:]):
        o_ref[...] = jnp.dot(a_ref[...], w, preferred_element_type=F32)


def matmul_groups(a, b, layer, a_small=None):
    m, kd = a.shape
    n = b.shape[2]
    tm = _row_tile(m, 2048)
    tn = MXU_N
    per_group = GROUP_W // tn
    lhs = [a] if a_small is None else [a, a_small]
    in_specs = [pl.BlockSpec((tm, kd), lambda i, j: (i, 0))]
    out_specs = [pl.BlockSpec((None, tm, tn), lambda i, j: (j // per_group, i, j % per_group))]
    out_shape = [jax.ShapeDtypeStruct((n // GROUP_W, m, GROUP_W), F32)]
    if a_small is not None:
        m2 = a_small.shape[0]
        in_specs.append(pl.BlockSpec((m2, kd), lambda i, j: (0, 0)))
        out_specs.append(pl.BlockSpec((None, m2, tn), lambda i, j: (j // per_group, 0, j % per_group)))
        out_shape.append(jax.ShapeDtypeStruct((n // GROUP_W, m2, GROUP_W), F32))
    in_specs.append(pl.BlockSpec((None, kd, tn), lambda i, j: (layer, 0, j)))
    outs = pl.pallas_call(
        _matmul_kernel,
        grid=(m // tm, n // tn),
        in_specs=in_specs,
        out_specs=out_specs,
        out_shape=out_shape,
        compiler_params=_params("arbitrary", "arbitrary"),
        name="matmul_groups",
    )(*lhs, b)
    return outs[0] if a_small is None else outs


def _outproj_kernel(a0_ref, a1_ref, a2_ref, a3_ref, w_ref, x_ref, o_ref):
    acc = x_ref[...]
    for g, a_ref in enumerate((a0_ref, a1_ref, a2_ref, a3_ref)):
        acc += jnp.dot(a_ref[...].astype(BF16), w_ref[g * GROUP_W:(g + 1) * GROUP_W, :].astype(BF16),
                       preferred_element_type=F32)
    o_ref[...] = acc


def outproj(parts, w, layer, x):
    m, d = x.shape
    tm = _row_tile(m, 1024)
    tn = 2 * MXU_N
    part_spec = pl.BlockSpec((tm, GROUP_W), lambda i, j: (i, 0))
    return pl.pallas_call(
        _outproj_kernel,
        grid=(m // tm, d // tn),
        in_specs=[part_spec, part_spec, part_spec, part_spec,
                  pl.BlockSpec((None, 4 * GROUP_W, tn), lambda i, j: (layer, 0, j)),
                  pl.BlockSpec((tm, tn), lambda i, j: (i, j))],
        out_specs=pl.BlockSpec((tm, tn), lambda i, j: (i, j)),
        out_shape=jax.ShapeDtypeStruct((m, d), F32),
        compiler_params=_params("parallel", "arbitrary"),
        name="outproj",
    )(*parts, w, x)


def _s5_bu_kernel(u_ref, w_ref, o_ref):
    r = jnp.dot(u_ref[...].astype(BF16), w_ref[...], preferred_element_type=F32)
    half = r.shape[1] // 2
    o_ref[0] = r[:, :half]
    o_ref[1] = r[:, half:]


def s5_bu(z, w_bu):
    m = z.shape[1]
    tm = _row_tile(m, 512)
    kw = S5_GROUPS_PER_DOT * C_A
    nw = S5_GROUPS_PER_DOT * N_A
    return pl.pallas_call(
        _s5_bu_kernel,
        grid=(m // tm, S5_CHUNKS),
        in_specs=[pl.BlockSpec((None, tm, kw), lambda i, k: (G_UA, i, k)),
                  pl.BlockSpec((None, kw, 2 * nw), lambda i, k: (k, 0, 0))],
        out_specs=pl.BlockSpec((2, tm, nw), lambda i, k: (0, i, k)),
        out_shape=jax.ShapeDtypeStruct((2, m, S5_STATE), F32),
        compiler_params=_params("parallel", "parallel"),
        name="s5_bu",
    )(z, w_bu)


def _s5_scan_kernel(bu_ref, a_ref, h0_ref, h_ref, hl_ref, st_ref, *, tt):
    j = pl.program_id(1)

    @pl.when(j == 0)
    def _():
        st_ref[...] = h0_ref[...]

    a_re = a_ref[0]
    a_im = a_ref[1]

    def body(t, carry):
        h_re, h_im = carry
        n_re = a_re * h_re - a_im * h_im + bu_ref[0, t]
        n_im = a_re * h_im + a_im * h_re + bu_ref[1, t]
        h_ref[0, t] = n_re
        h_ref[1, t] = n_im
        return n_re, n_im

    h_re, h_im = lax.fori_loop(0, tt, body, (st_ref[0], st_ref[1]), unroll=min(tt, 8))
    st_ref[0] = h_re
    st_ref[1] = h_im

    @pl.when(j == pl.num_programs(1) - 1)
    def _():
        hl_ref[0] = h_re
        hl_ref[1] = h_im


def s5_scan(bu, a_bar, h0, bsz, t):
    rows = S5_STATE // LANES
    bu5 = bu.reshape(2, bsz, t, rows, LANES)
    tt = _row_tile(t, 128)
    h, h_last = pl.pallas_call(
        functools.partial(_s5_scan_kernel, tt=tt),
        grid=(bsz, t // tt),
        in_specs=[pl.BlockSpec((2, None, tt, rows, LANES), lambda b, j: (0, b, j, 0, 0)),
                  pl.BlockSpec((2, rows, LANES), lambda b, j: (0, 0, 0)),
                  pl.BlockSpec((2, None, rows, LANES), lambda b, j: (0, b, 0, 0))],
        out_specs=[pl.BlockSpec((2, None, tt, rows, LANES), lambda b, j: (0, b, j, 0, 0)),
                   pl.BlockSpec((2, None, rows, LANES), lambda b, j: (0, b, 0, 0))],
        out_shape=[jax.ShapeDtypeStruct((2, bsz, t, rows, LANES), F32),
                   jax.ShapeDtypeStruct((2, bsz, rows, LANES), F32)],
        scratch_shapes=[pltpu.VMEM((2, rows, LANES), F32)],
        compiler_params=_params("parallel", "arbitrary"),
        name="s5_scan",
    )(bu5, a_bar.reshape(2, rows, LANES), h0.reshape(2, bsz, rows, LANES))
    return h.reshape(2, bsz * t, S5_STATE), h_last.reshape(2, bsz, G_A, N_A)


def _s5_y_kernel(h_ref, u_ref, g_ref, wc_ref, d_ref, wg_ref, bg_ref, o_ref):
    nw = S5_GROUPS_PER_DOT * N_A
    parts = []
    for k in range(S5_CHUNKS):
        h_re = h_ref[0, :, k * nw:(k + 1) * nw].astype(BF16)
        h_im = h_ref[1, :, k * nw:(k + 1) * nw].astype(BF16)
        parts.append(jnp.dot(h_re, wc_ref[0, k], preferred_element_type=F32)
                     + jnp.dot(h_im, wc_ref[1, k], preferred_element_type=F32))
    y = jnp.concatenate(parts, axis=-1) + d_ref[...] * u_ref[...]
    y = jax.nn.gelu(y)
    glu = jnp.dot(y.astype(BF16), wg_ref[...], preferred_element_type=F32) + bg_ref[...]
    y = y * jax.nn.sigmoid(glu)
    o_ref[...] = (y * _silu(g_ref[...])).astype(o_ref.dtype)


def s5_y(h, z, wc, d, w_glu, b_glu, out_dtype):
    m = z.shape[1]
    tm = _row_tile(m, 256)
    kw = S5_GROUPS_PER_DOT * C_A
    nw = S5_GROUPS_PER_DOT * N_A
    return pl.pallas_call(
        _s5_y_kernel,
        grid=(m // tm,),
        in_specs=[pl.BlockSpec((2, tm, S5_STATE), lambda i: (0, i, 0)),
                  pl.BlockSpec((None, tm, GROUP_W), lambda i: (G_UA, i, 0)),
                  pl.BlockSpec((None, tm, GROUP_W), lambda i: (G_GA, i, 0)),
                  pl.BlockSpec((2, S5_CHUNKS, nw, kw), lambda i: (0, 0, 0, 0)),
                  pl.BlockSpec((1, GROUP_W), lambda i: (0, 0)),
                  pl.BlockSpec((GROUP_W, GROUP_W), lambda i: (0, 0)),
                  pl.BlockSpec((1, GROUP_W), lambda i: (0, 0))],
        out_specs=pl.BlockSpec((tm, GROUP_W), lambda i: (i, 0)),
        out_shape=jax.ShapeDtypeStruct((m, GROUP_W), out_dtype),
        compiler_params=_params("parallel"),
        name="s5_y",
    )(h, z, z, wc, d.reshape(1, GROUP_W), w_glu, b_glu.reshape(1, GROUP_W))


def _s5_fused_kernel(u_ref, g_ref, wbu_ref, pw_ref, h0_ref, wc_ref, d_ref, wg_ref, bg_ref,
                     o_ref, hl_ref, st_ref, h_scr, *, tt):
    j = pl.program_id(1)
    nw = S5_GROUPS_PER_DOT * N_A
    kw = S5_GROUPS_PER_DOT * C_A
    n_tiles = tt // SUBLANES

    @pl.when(j == 0)
    def _():
        st_ref[...] = h0_ref[...]

    def axpy(x_re, x_im, a_re, a_im, s_re, s_im):
        return x_re + (a_re * s_re - a_im * s_im), x_im + (a_re * s_im + a_im * s_re)

    u = u_ref[...]
    y_parts = []
    for k in range(S5_CHUNKS):
        cols = slice(k * nw, (k + 1) * nw)
        bu = jnp.dot(u[:, k * kw:(k + 1) * kw].astype(BF16), wbu_ref[k], preferred_element_type=F32)
        h_scr[0] = bu[:, :nw].reshape(n_tiles, SUBLANES, nw)
        h_scr[1] = bu[:, nw:].reshape(n_tiles, SUBLANES, nw)

        def tile_step(g, carry, cols=cols):
            c_re, c_im = carry
            x_re, x_im = h_scr[0, g], h_scr[1, g]
            for p, shift in enumerate((1, 2, 4)):
                x_re, x_im = axpy(x_re, x_im, pw_ref[0, p, :, cols], pw_ref[1, p, :, cols],
                                  pltpu.roll(x_re, shift, axis=0), pltpu.roll(x_im, shift, axis=0))
            x_re, x_im = axpy(x_re, x_im, pw_ref[0, 3, :, cols], pw_ref[1, 3, :, cols], c_re, c_im)
            h_scr[0, g] = x_re
            h_scr[1, g] = x_im
            return x_re[SUBLANES - 1:SUBLANES, :], x_im[SUBLANES - 1:SUBLANES, :]

        c_re, c_im = lax.fori_loop(0, n_tiles, tile_step, (st_ref[0, :, cols], st_ref[1, :, cols]),
                                   unroll=min(n_tiles, 4))
        st_ref[0, :, cols] = c_re
        st_ref[1, :, cols] = c_im
        h_re = h_scr[0].reshape(tt, nw).astype(BF16)
        h_im = h_scr[1].reshape(tt, nw).astype(BF16)
        y_parts.append(jnp.dot(h_re, wc_ref[0, k], preferred_element_type=F32)
                       + jnp.dot(h_im, wc_ref[1, k], preferred_element_type=F32))
    y = jnp.concatenate(y_parts, axis=-1) + d_ref[...] * u
    y = jax.nn.gelu(y)
    glu = jnp.dot(y.astype(BF16), wg_ref[...], preferred_element_type=F32) + bg_ref[...]
    y = y * jax.nn.sigmoid(glu)
    o_ref[...] = (y * _silu(g_ref[...])).astype(o_ref.dtype)

    @pl.when(j == pl.num_programs(1) - 1)
    def _():
        hl_ref[...] = st_ref[...]


def s5_fused(z4, lw, h0, out_dtype):
    _, bsz, t, _ = z4.shape
    tt = _row_tile(t, 512)
    kw = S5_GROUPS_PER_DOT * C_A
    nw = S5_GROUPS_PER_DOT * N_A

    def tok(g):
        return pl.BlockSpec((None, None, tt, GROUP_W), lambda b, j: (g, b, j, 0))

    def whole(shape):
        return pl.BlockSpec(shape, lambda b, j: (0,) * len(shape))

    state_spec = pl.BlockSpec((2, None, 1, S5_STATE), lambda b, j: (0, b, 0, 0))
    y, h_last = pl.pallas_call(
        functools.partial(_s5_fused_kernel, tt=tt),
        grid=(bsz, t // tt),
        in_specs=[tok(G_UA), tok(G_GA),
                  whole((S5_CHUNKS, kw, 2 * nw)),
                  whole((2, 4, SUBLANES, S5_STATE)),
                  state_spec,
                  whole((2, S5_CHUNKS, nw, kw)),
                  whole((1, GROUP_W)), whole((GROUP_W, GROUP_W)), whole((1, GROUP_W))],
        out_specs=[pl.BlockSpec((None, tt, GROUP_W), lambda b, j: (b, j, 0)), state_spec],
        out_shape=[jax.ShapeDtypeStruct((bsz, t, GROUP_W), out_dtype),
                   jax.ShapeDtypeStruct((2, bsz, 1, S5_STATE), F32)],
        scratch_shapes=[pltpu.VMEM((2, 1, S5_STATE), F32),
                        pltpu.VMEM((2, tt // SUBLANES, SUBLANES, nw), F32)],
        compiler_params=_params("parallel", "arbitrary"),
        name="s5_fused",
    )(z4, z4, lw['w_bu'], lw['a_pow'], h0.reshape(2, bsz, 1, S5_STATE), lw['wc'],
      lw['s5_d'].reshape(1, GROUP_W), lw['w_glu'], lw['b_glu'].reshape(1, GROUP_W))
    return y, h_last.reshape(2, bsz, G_A, N_A)


def s5_scan_multipliers(a_pair):
    a = lax.complex(a_pair[0], a_pair[1]).reshape(S5_STATE)
    powers = [a]
    for _ in range(SUBLANES - 1):
        powers.append(powers[-1] * a)
    row = jnp.arange(SUBLANES)[:, None]
    planes = [jnp.where(row >= s, powers[s - 1][None, :], 0.0) for s in (1, 2, 4)]
    planes.append(jnp.stack(powers))
    pw = jnp.stack(planes)
    return jnp.stack([pw.real, pw.imag])


def s5_weights(lam_re, lam_im, log_dt, b_re, b_im, c_re, c_im):
    lam = lax.complex(lam_re.astype(F32), lam_im.astype(F32))
    dt = jnp.exp(log_dt.astype(F32))[:, None]
    a_bar = jnp.exp(lam * dt)
    b_bar = ((a_bar - 1.0) / lam)[..., None] * lax.complex(b_re.astype(F32), b_im.astype(F32))
    gpd, kw, nw = S5_GROUPS_PER_DOT, S5_GROUPS_PER_DOT * C_A, S5_GROUPS_PER_DOT * N_A
    bb = jnp.stack([b_bar.real, b_bar.imag]).reshape(2, S5_CHUNKS, gpd, N_A, C_A)
    b_cols = bb.transpose(1, 4, 0, 2, 3).reshape(S5_CHUNKS, C_A, 2 * nw)
    row_g = jnp.arange(kw)[:, None] // C_A
    col_h = (jnp.arange(2 * nw)[None, :] % nw) // N_A
    w_bu = jnp.where(row_g == col_h, jnp.tile(b_cols, (1, gpd, 1)), 0.0).astype(BF16)
    cc = jnp.stack([c_re.astype(F32), -c_im.astype(F32)]).reshape(2, S5_CHUNKS, gpd, C_A, N_A)
    c_rows = cc.transpose(0, 1, 2, 4, 3).reshape(2, S5_CHUNKS, nw, C_A)
    wc = jnp.where(jnp.arange(nw)[:, None] // N_A == jnp.arange(kw)[None, :] // C_A,
                   jnp.tile(c_rows, (1, 1, 1, gpd)), 0.0).astype(BF16)
    a_pair = jnp.stack([a_bar.real, a_bar.imag])
    return a_pair, w_bu, wc


def _hgrn_kernel(q_ref, f_ref, i_ref, g_ref, lb_ref, ng_ref, s0_ref, y_ref, sl_ref,
                 st_ref, pad_scr, *, tc, chunk, t_valid):
    sub = HGRN_SUB
    n_sb = chunk // sub
    j = pl.program_id(2)
    nt = (((1,), (1,)), ((), ()))

    @pl.when(j == 0)
    def _():
        st_ref[...] = s0_ref[...].T

    lb = lb_ref[...]
    padded = tc < chunk
    if padded:
        pad_scr[...] = jnp.zeros_like(pad_scr)
        pad_scr[0, 0:tc, :] = q_ref[...]
        pad_scr[1, 0:tc, :] = f_ref[...]
        pad_scr[2, 0:tc, :] = i_ref[...]
    row_c = lax.broadcasted_iota(jnp.int32, (chunk, LANES), 0)
    row_t = lax.broadcasted_iota(jnp.int32, (sub, chunk), 0)
    lane_s = lax.broadcasted_iota(jnp.int32, (sub, chunk), 1)
    tri = (lax.broadcasted_iota(jnp.int32, (chunk, chunk), 1)
           <= lax.broadcasted_iota(jnp.int32, (chunk, chunk), 0)).astype(F32)

    def chunk_step(ci, carry):
        r0 = pl.multiple_of(ci * chunk, chunk)
        if padded:
            q, f_pre, inp = pad_scr[0], pad_scr[1], pad_scr[2]
        else:
            q = q_ref[pl.ds(r0, chunk), :]
            f_pre = f_ref[pl.ds(r0, chunk), :]
            inp = i_ref[pl.ds(r0, chunk), :]
        f = lb + (1.0 - lb) * jax.nn.sigmoid(f_pre)
        log_f = jnp.log2(f)
        kk = 1.0 - f
        if padded:
            log_f = jnp.where(row_c < t_valid, log_f, 0.0)
            kk = jnp.where(row_c < t_valid, kk, 0.0)
        cum = jnp.dot(tri, log_f, precision=lax.Precision.HIGHEST, preferred_element_type=F32)
        last = cum[chunk - 1:chunk, :]
        st = st_ref[...]
        o = lax.dot_general((q * jnp.exp2(cum)).astype(BF16), st.astype(BF16), nt,
                            preferred_element_type=F32)
        inp_b = inp.astype(BF16)
        blocks = []
        for i in range(n_sb):
            rows = slice(i * sub, (i + 1) * sub)
            q_i, kk_i, cum_i = q[rows], kk[rows], cum[rows]
            if i > 0:
                edge = cum[i * sub - 1:i * sub, :]
                a_i = q_i * jnp.exp2(cum_i - edge)
                kt_i = kk * jnp.exp2(jnp.where(row_c < i * sub, edge - cum, -jnp.inf))
                sc = lax.dot_general(a_i.astype(BF16), kt_i.astype(BF16), nt, preferred_element_type=F32)
            else:
                sc = jnp.zeros((sub, chunk), F32)
            for s in range(min(sub, t_valid)):
                w = (q_i * kk_i[s:s + 1, :]) * jnp.exp2(cum_i - cum_i[s:s + 1, :])
                sc = jnp.where(lane_s == i * sub + s, jnp.sum(w, axis=-1, keepdims=True), sc)
            blocks.append(jnp.where(lane_s - i * sub > row_t, 0.0, sc))
        scores = blocks[0] if n_sb == 1 else jnp.concatenate(blocks, axis=0)
        o = o + jnp.dot(scores.astype(BF16), inp_b, preferred_element_type=F32)
        kt = (kk * jnp.exp2(last - cum)).astype(BF16)
        upd = lax.dot_general(inp_b, kt, (((0,), (0,)), ((), ())), preferred_element_type=F32)
        st_ref[...] = st * jnp.exp2(last) + upd
        y = o * lax.rsqrt(jnp.mean(o * o, axis=-1, keepdims=True) + EPS) * ng_ref[...]
        if padded:
            y_ref[...] = (y[0:tc] * _silu(g_ref[...])).astype(y_ref.dtype)
        else:
            y_ref[pl.ds(r0, chunk), :] = (y * _silu(g_ref[pl.ds(r0, chunk), :])).astype(y_ref.dtype)
        return carry

    lax.fori_loop(0, max(tc // chunk, 1), chunk_step, 0, unroll=True)

    @pl.when(j == pl.num_programs(2) - 1)
    def _():
        sl_ref[...] = st_ref[...].T


def hgrn(z4, lb, norm_g, s0, out_dtype):
    _, bsz, t, _ = z4.shape
    chunk = HGRN_CHUNK if t % HGRN_CHUNK == 0 else HGRN_SUB
    tc = _row_tile(t, 512) if t >= chunk else t
    t_valid = min(chunk, tc)

    def col(g):
        return pl.BlockSpec((None, None, tc, DK_B), lambda b, h, j: (g, b, j, h))

    state_spec = pl.BlockSpec((None, None, DK_B, DV_B), lambda b, h, j: (b, h, 0, 0))
    return pl.pallas_call(
        functools.partial(_hgrn_kernel, tc=tc, chunk=chunk, t_valid=t_valid),
        grid=(bsz, H_B, t // tc),
        in_specs=[col(G_QB), col(G_FB), col(G_IB), col(G_GB),
                  pl.BlockSpec((1, DK_B), lambda b, h, j: (0, h)),
                  pl.BlockSpec((1, DV_B), lambda b, h, j: (0, 0)),
                  state_spec],
        out_specs=[pl.BlockSpec((None, tc, DV_B), lambda b, h, j: (b, j, h)), state_spec],
        out_shape=[jax.ShapeDtypeStruct((bsz, t, GROUP_W), out_dtype),
                   jax.ShapeDtypeStruct((bsz, H_B, DK_B, DV_B), F32)],
        scratch_shapes=[pltpu.VMEM((DV_B, DK_B), F32),
                        pltpu.VMEM((3, chunk, LANES), F32)],
        compiler_params=_params("parallel", "parallel", "arbitrary"),
        name="hgrn",
    )(z4, z4, z4, z4, lb.reshape(1, GROUP_W), norm_g.reshape(1, DV_B), s0)


def _diff_norm_gate(o, ng, gate, out_scale):
    y = o * lax.rsqrt(jnp.mean(o * o, axis=-1, keepdims=True) + EPS) * ng
    return (y * out_scale) * _silu(gate)


def _dattn_prompt_kernel(lam_ref, q_ref, k_ref, v_ref, g_ref, ng_ref, o_ref,
                         kb_scr, vb_scr, *, tq, n_blk, out_scale):
    i = pl.program_id(2)

    @pl.when(i == 0)
    def _():
        kb_scr[...] = k_ref[...].astype(BF16)
        vb_scr[...] = v_ref[...].astype(BF16)

    nt = (((1,), (1,)), ((), ()))

    def attend(n_past):
        past = n_past * tq
        q = q_ref[...] * (DH_C ** -0.5 * LOG2_E)
        lane = lax.broadcasted_iota(jnp.int32, (tq, DV_C), 1)
        causal = (lax.broadcasted_iota(jnp.int32, (tq, tq), 1)
                  <= lax.broadcasted_iota(jnp.int32, (tq, tq), 0))
        k_diag = kb_scr[past:past + tq, :]
        v_diag = vb_scr[past:past + tq, :]

        def softmax_pv(qj):
            s_d = jnp.where(causal, lax.dot_general(qj, k_diag, nt, preferred_element_type=F32), -jnp.inf)
            m = jnp.max(s_d, axis=-1, keepdims=True)
            if n_past:
                s_p = lax.dot_general(qj, kb_scr[0:past, :], nt, preferred_element_type=F32)
                m = jnp.maximum(m, jnp.max(s_p, axis=-1, keepdims=True))
            e_d = jnp.exp2(s_d - m)
            l = jnp.sum(e_d, axis=-1, keepdims=True)
            pv = jnp.dot(e_d.astype(BF16), v_diag, preferred_element_type=F32)
            if n_past:
                e_p = jnp.exp2(s_p - m)
                l = l + jnp.sum(e_p, axis=-1, keepdims=True)
                pv = pv + jnp.dot(e_p.astype(BF16), vb_scr[0:past, :], preferred_element_type=F32)
            return pv / l

        o = (softmax_pv(jnp.where(lane < DH_C, q, 0.0).astype(BF16))
             - lam_ref[...] * softmax_pv(jnp.where(lane >= DH_C, q, 0.0).astype(BF16)))
        o_ref[...] = _diff_norm_gate(o, ng_ref[...], g_ref[...], out_scale).astype(o_ref.dtype)

    for n_past in range(n_blk):
        pl.when(i == n_past)(functools.partial(attend, n_past))


def dattn_prompt(z4, lam, norm_g, lam_init, out_dtype):
    _, bsz, t, _ = z4.shape
    tq = _row_tile(t, 512)

    def q_col(g):
        return pl.BlockSpec((None, None, tq, DV_C), lambda b, h, i: (g, b, i, h))

    def kv_col(g):
        return pl.BlockSpec((None, None, t, DV_C), lambda b, h, i: (g, b, 0, h))

    vec = pl.BlockSpec((1, DV_C), lambda b, h, i: (0, 0))
    return pl.pallas_call(
        functools.partial(_dattn_prompt_kernel, tq=tq, n_blk=t // tq, out_scale=1.0 - lam_init),
        grid=(bsz, H_C, t // tq),
        in_specs=[vec, q_col(G_QC), kv_col(G_KC), kv_col(G_VC), q_col(G_GC), vec],
        out_specs=pl.BlockSpec((None, tq, DV_C), lambda b, h, i: (b, i, h)),
        out_shape=jax.ShapeDtypeStruct((bsz, t, GROUP_W), out_dtype),
        scratch_shapes=[pltpu.VMEM((t, DV_C), BF16),
                        pltpu.VMEM((t, DV_C), BF16)],
        compiler_params=_params("parallel", "parallel", "arbitrary"),
        name="dattn_prompt",
    )(jnp.full((1, DV_C), lam, F32), z4, z4, z4, z4, norm_g.reshape(1, DV_C))


def _dattn_sample_kernel(pt_ref, lam_ref, q_ref, kn_ref, vn_ref, g_ref, ng_ref, *rest,
                         pps, t, out_scale):
    k_refs = rest[:pps]
    v_refs = rest[pps:2 * pps]
    o_ref = rest[2 * pps]
    qm_scr, m_scr, l_scr, acc_scr, kn_scr, vn_scr = rest[2 * pps + 1:]
    del pt_ref
    j = pl.program_id(1)
    half = t * H_C
    n_rows = 2 * half
    page_rows = k_refs[0].shape[0] * H_C
    nt = (((1,), (1,)), ((), ()))

    @pl.when(j == 0)
    def _():
        q = q_ref[...].reshape(half, DV_C) * (DH_C ** -0.5)
        lane = lax.broadcasted_iota(jnp.int32, (half, DV_C), 1)
        qm_scr[0:half, :] = jnp.where(lane < DH_C, q, 0.0).astype(BF16)
        qm_scr[half:n_rows, :] = jnp.where(lane >= DH_C, q, 0.0).astype(BF16)
        m_scr[...] = jnp.full_like(m_scr, -jnp.inf)
        l_scr[...] = jnp.zeros_like(l_scr)
        acc_scr[...] = jnp.zeros_like(acc_scr)

    def online_update(s, v_blocks, width):
        m_old = m_scr[...]
        m_new = jnp.maximum(m_old, jnp.max(s, axis=-1, keepdims=True))
        alpha = jnp.exp(m_old - m_new)
        e = jnp.exp(s - m_new)
        l_scr[...] = alpha * l_scr[...] + jnp.sum(e, axis=-1, keepdims=True)
        pv = None
        for r, vb in enumerate(v_blocks):
            d = jnp.dot(e[:, r * width:(r + 1) * width].astype(BF16), vb, preferred_element_type=F32)
            pv = d if pv is None else pv + d
        acc_scr[...] = alpha * acc_scr[...] + pv
        m_scr[...] = m_new

    qm = qm_scr[...]
    same_head = (lax.broadcasted_iota(jnp.int32, (n_rows, page_rows), 0) % H_C
                 == lax.broadcasted_iota(jnp.int32, (n_rows, page_rows), 1) % H_C)
    s_parts = []
    for r in range(pps):
        kp = k_refs[r][...].reshape(page_rows, DV_C).astype(BF16)
        s_parts.append(jnp.where(same_head, lax.dot_general(qm, kp, nt, preferred_element_type=F32),
                                 -jnp.inf))
    online_update(jnp.concatenate(s_parts, axis=-1),
                  [v_refs[r][...].reshape(page_rows, DV_C).astype(BF16) for r in range(pps)], page_rows)

    @pl.when(j == pl.num_programs(1) - 1)
    def _():
        kn_scr[...] = jnp.zeros_like(kn_scr)
        vn_scr[...] = jnp.zeros_like(vn_scr)
        kn_scr[0:half, :] = kn_ref[...].reshape(half, DV_C)
        vn_scr[0:half, :] = vn_ref[...].reshape(half, DV_C)
        s_new = lax.dot_general(qm, kn_scr[...].astype(BF16), nt, preferred_element_type=F32)
        row = lax.broadcasted_iota(jnp.int32, (n_rows, LANES), 0)
        col = lax.broadcasted_iota(jnp.int32, (n_rows, LANES), 1)
        visible = (col % H_C == row % H_C) & (col // H_C <= (row // H_C) % t)
        online_update(jnp.where(visible, s_new, -jnp.inf), [vn_scr[...].astype(BF16)], LANES)

        o = (acc_scr[0:half, :] / l_scr[0:half, :]
             - lam_ref[...] * (acc_scr[half:n_rows, :] / l_scr[half:n_rows, :]))
        y = _diff_norm_gate(o, ng_ref[...], g_ref[...].reshape(half, DV_C), out_scale)
        o_ref[...] = y.reshape(t, H_C, DV_C).astype(o_ref.dtype)


def dattn_sample(z4, cache_k, cache_v, layer, page_table, lam, norm_g, lam_init, out_dtype):
    _, bsz, t, _ = z4.shape
    page = cache_k.shape[2]
    n_pages = page_table.shape[1]
    pps = 8 if n_pages % 8 == 0 else 1
    heads = z4[G_QC:G_GC + 1].reshape(4, bsz, t, H_C, DV_C)

    def tok(g):
        return pl.BlockSpec((None, None, t, H_C, DV_C), lambda b, j, pt: (g, b, 0, 0, 0))

    def page_spec(r):
        return pl.BlockSpec((None, None, page, H_C, DV_C),
                            lambda b, j, pt: (layer, pt[b * n_pages + j * pps + r], 0, 0, 0))

    vec = pl.BlockSpec((1, DV_C), lambda b, j, pt: (0, 0))
    n_rows = 2 * t * H_C
    grid_spec = pltpu.PrefetchScalarGridSpec(
        num_scalar_prefetch=1,
        grid=(bsz, n_pages // pps),
        in_specs=[vec, tok(0), tok(1), tok(2), tok(3), vec]
        + [page_spec(r) for r in range(pps)] + [page_spec(r) for r in range(pps)],
        out_specs=pl.BlockSpec((None, t, H_C, DV_C), lambda b, j, pt: (b, 0, 0, 0)),
        scratch_shapes=[pltpu.VMEM((n_rows, DV_C), BF16),
                        pltpu.VMEM((n_rows, 1), F32),
                        pltpu.VMEM((n_rows, 1), F32),
                        pltpu.VMEM((n_rows, DV_C), F32),
                        pltpu.VMEM((LANES, DV_C), F32),
                        pltpu.VMEM((LANES, DV_C), F32)],
    )
    y = pl.pallas_call(
        functools.partial(_dattn_sample_kernel, pps=pps, t=t, out_scale=1.0 - lam_init),
        grid_spec=grid_spec,
        out_shape=jax.ShapeDtypeStruct((bsz, t, H_C, DV_C), out_dtype),
        compiler_params=_params("parallel", "arbitrary"),
        name="dattn_sample",
    )(page_table.reshape(-1), jnp.full((1, DV_C), lam, F32), heads, heads, heads, heads,
      norm_g.reshape(1, DV_C), *([cache_k] * pps), *([cache_v] * pps))
    return y.reshape(bsz, t, H_C * DV_C)


def _mem_attn_kernel(q_ref, g_ref, mk_ref, mv_ref, o_ref, pad_scr, *, tq):
    rows = pad_scr.shape[0]
    if tq < rows:
        pad_scr[...] = jnp.zeros_like(pad_scr)
        pad_scr[0:tq, :] = q_ref[...]
        q_all = pad_scr[...]
    else:
        q_all = q_ref[...]
    nt = (((1,), (1,)), ((), ()))
    for h in range(H_M):
        cols = slice(h * DH_M, (h + 1) * DH_M)
        q = (q_all[:, cols] * (DH_M ** -0.5)).astype(BF16)
        s = lax.dot_general(q, mk_ref[:, cols].astype(BF16), nt, preferred_element_type=F32)
        e = jnp.exp(s - jnp.max(s, axis=-1, keepdims=True))
        l = jnp.sum(e, axis=-1, keepdims=True)
        o = jnp.dot(e.astype(BF16), mv_ref[:, cols].astype(BF16), preferred_element_type=F32) / l
        o_ref[:, cols] = (o[0:tq, :] * _silu(g_ref[:, cols])).astype(o_ref.dtype)


def mem_attn(z4, mem_k, mem_v, out_dtype):
    _, bsz, t, _ = z4.shape
    n_mem = mem_k.shape[1]
    tq = _row_tile(t, 256)
    rows = max(tq, 2 * SUBLANES)

    def tok(g):
        return pl.BlockSpec((None, None, tq, GROUP_W), lambda b, i: (g, b, i, 0))

    mem_spec = pl.BlockSpec((None, n_mem, GROUP_W), lambda b, i: (b, 0, 0))
    return pl.pallas_call(
        functools.partial(_mem_attn_kernel, tq=tq),
        grid=(bsz, t // tq),
        in_specs=[tok(G_QM), tok(G_GM), mem_spec, mem_spec],
        out_specs=pl.BlockSpec((None, tq, GROUP_W), lambda b, i: (b, i, 0)),
        out_shape=jax.ShapeDtypeStruct((bsz, t, GROUP_W), out_dtype),
        scratch_shapes=[pltpu.VMEM((rows, GROUP_W), F32)],
        compiler_params=_params("parallel", "parallel"),
        name="mem_attn",
    )(z4, z4, mem_k.reshape(bsz, n_mem, GROUP_W), mem_v.reshape(bsz, n_mem, GROUP_W))


def input_projection(x_p, x_s, lw):
    d = x_p.shape[-1]
    h_p = rmsnorm(x_p.reshape(-1, d), lw['norm_g'], BF16)
    h_s = rmsnorm(x_s.reshape(-1, d), lw['norm_g'], BF16)
    return matmul_groups(h_p, lw['w_in'], lw['layer'], a_small=h_s)


def trunk_layer(x, z, lw, attend, mem_k, mem_v, s5_h0, hgrn_s0, part_dtype):
    bsz, t, d = x.shape
    m = bsz * t
    x2 = x.reshape(m, d)
    z4 = z.reshape(N_GROUPS_IN, bsz, t, GROUP_W)

    if t % SUBLANES == 0:
        y_a, s5_last = s5_fused(z4, lw, s5_h0, part_dtype)
        y_a = y_a.reshape(m, GROUP_W)
    else:
        bu = s5_bu(z, lw['w_bu'])
        h_all, s5_last = s5_scan(bu, lw['a_bar'], s5_h0, bsz, t)
        y_a = s5_y(h_all, z, lw['wc'], lw['s5_d'], lw['w_glu'], lw['b_glu'], part_dtype)

    y_b, hgrn_s = hgrn(z4, lw['lb'], lw['hgrn_norm_g'], hgrn_s0, part_dtype)
    y_c = attend(z4)
    y_m = mem_attn(z4, mem_k, mem_v, part_dtype)

    parts = (y_a, y_b.reshape(m, GROUP_W), y_c.reshape(m, GROUP_W), y_m.reshape(m, GROUP_W))
    x_out = outproj(parts, lw['w_out'], lw['layer'], x2).reshape(bsz, t, d)
    return x_out, z4, s5_last, hgrn_s


def kernel(x_prompt, x_sample, cache_k, cache_v, cache_mem_k, cache_mem_v, state_s5_re, state_s5_im, state_hgrn, page_table, mem_prompt, norm_g, w_in, s5_lambda_re, s5_lambda_im, s5_log_dt, s5_b_re, s5_b_im, s5_c_re, s5_c_im, s5_d, s5_w_glu, s5_b_glu, hgrn_lower_bounds, hgrn_norm_g, diff_lq1, diff_lk1, diff_lq2, diff_lk2, diff_norm_g, mem_norm_g, w_mem_kv, w_out, final_norm_g):
    depth = w_in.shape[0]
    bsz_p, t_p, d = x_prompt.shape
    bsz_s, t_s, _ = x_sample.shape
    n_mem = mem_prompt.shape[1]
    lb_all = jnp.cumsum(jax.nn.softmax(hgrn_lower_bounds.astype(F32), axis=0), axis=0)
    lb_all = lb_all - lb_all[0]
    zeros_s5 = jnp.zeros((2, bsz_p, G_A, N_A), F32)
    zeros_hgrn = jnp.zeros((bsz_p, H_B, DK_B, DV_B), F32)

    lam_inits = [0.8 - 0.6 * math.exp(-0.3 * l) for l in range(depth)]
    lams = (jnp.exp(jnp.sum(diff_lq1.astype(F32) * diff_lk1.astype(F32), axis=-1))
            - jnp.exp(jnp.sum(diff_lq2.astype(F32) * diff_lk2.astype(F32), axis=-1))
            + jnp.asarray(lam_inits, F32))
    a_bars, w_bus, wcs = jax.vmap(s5_weights)(s5_lambda_re, s5_lambda_im, s5_log_dt,
                                              s5_b_re, s5_b_im, s5_c_re, s5_c_im)
    a_pows = jax.vmap(s5_scan_multipliers)(a_bars)
    w_glus = s5_w_glu.astype(BF16)

    xp, xs = x_prompt, x_sample
    outs = {k: [] for k in ('kp', 'vp', 'ks', 'vs', 'mkp', 'mvp', 's5p', 's5s', 'hgp', 'hgs')}
    for l in range(depth):
        lam_init = lam_inits[l]
        lam = lams[l]
        lw = {'layer': l, 'norm_g': norm_g[l], 'w_in': w_in, 'w_out': w_out,
              'a_bar': a_bars[l], 'a_pow': a_pows[l], 'w_bu': w_bus[l], 'wc': wcs[l], 's5_d': s5_d[l],
              'w_glu': w_glus[l], 'b_glu': s5_b_glu[l],
              'lb': lb_all[l], 'hgrn_norm_g': hgrn_norm_g[l]}
        z_p, z_s = input_projection(xp, xs, lw)

        hm = rmsnorm(mem_prompt.reshape(bsz_p * n_mem, d), mem_norm_g[l], BF16)
        mkv = matmul_groups(hm, w_mem_kv, l)
        mk_p = mkv[0].reshape(bsz_p, n_mem, H_M, DH_M)
        mv_p = mkv[1].reshape(bsz_p, n_mem, H_M, DH_M)
        attend_p = functools.partial(dattn_prompt, lam=lam, norm_g=diff_norm_g[l],
                                     lam_init=lam_init, out_dtype=BF16)
        xp, z4, s5_last, hg = trunk_layer(xp, z_p, lw, attend_p, mk_p, mv_p, zeros_s5, zeros_hgrn, BF16)
        outs['kp'].append(z4[G_KC].reshape(bsz_p, t_p, H_C, 2 * DH_C))
        outs['vp'].append(z4[G_VC].reshape(bsz_p, t_p, H_C, DV_C))
        outs['mkp'].append(mk_p)
        outs['mvp'].append(mv_p)
        outs['s5p'].append(s5_last)
        outs['hgp'].append(hg)

        attend_s = functools.partial(dattn_sample, cache_k=cache_k, cache_v=cache_v, layer=l,
                                     page_table=page_table, lam=lam, norm_g=diff_norm_g[l],
                                     lam_init=lam_init, out_dtype=F32)
        s5_h0 = jnp.stack([state_s5_re[l].astype(F32), state_s5_im[l].astype(F32)])
        xs, z4, s5_last, hg = trunk_layer(xs, z_s, lw, attend_s, cache_mem_k[l], cache_mem_v[l],
                                          s5_h0, state_hgrn[l], F32)
        outs['ks'].append(z4[G_KC].reshape(bsz_s, t_s, H_C, 2 * DH_C))
        outs['vs'].append(z4[G_VC].reshape(bsz_s, t_s, H_C, DV_C))
        outs['s5s'].append(s5_last)
        outs['hgs'].append(hg)

    y_prompt = rmsnorm(xp.reshape(bsz_p * t_p, d), final_norm_g, F32).reshape(bsz_p, t_p, d)
    y_sample = rmsnorm(xs.reshape(bsz_s * t_s, d), final_norm_g, F32).reshape(bsz_s, t_s, d)
    s5p = jnp.stack(outs['s5p'])
    s5s = jnp.stack(outs['s5s'])
    return (y_prompt, y_sample,
            jnp.stack(outs['kp']), jnp.stack(outs['vp']), jnp.stack(outs['ks']), jnp.stack(outs['vs']),
            jnp.stack(outs['mkp']), jnp.stack(outs['mvp']),
            s5p[:, 0], s5p[:, 1], s5s[:, 0], s5s[:, 1],
            jnp.stack(outs['hgp']), jnp.stack(outs['hgs']))
```

```python
import functools
import math

import jax
import jax.numpy as jnp
from jax import lax
from jax.experimental import pallas as pl
from jax.experimental.pallas import tpu as pltpu

F32 = jnp.float32
BF16 = jnp.bfloat16

D_MODEL = 4096
GROUP_W = 1024
N_GROUPS_IN = 12
C_A = 16
G_A = GROUP_W // C_A
N_A = 64
S5_GROUPS_PER_DOT = 16
S5_CHUNKS = G_A // S5_GROUPS_PER_DOT
S5_STATE = G_A * N_A
DK_B = 128
DV_B = 128
H_B = GROUP_W // DK_B
DV_C = 128
DH_C = DV_C // 2
H_C = GROUP_W // DV_C
H_M = 4
DH_M = GROUP_W // H_M
EPS = 1e-6
LOG2_E = math.log2(math.e)
HGRN_CHUNK = 64
HGRN_SUB = 16
SUBLANES = 8
LANES = 128
MXU_N = 256
VMEM_LIMIT = 60 * 1024 * 1024

(G_UA, G_GA, G_QB, G_FB, G_IB, G_GB, G_QC, G_KC, G_VC, G_GC, G_QM, G_GM) = range(N_GROUPS_IN)


def _params(*sem):
    return pltpu.CompilerParams(dimension_semantics=sem, vmem_limit_bytes=VMEM_LIMIT)


def _silu(x):
    return x * jax.nn.sigmoid(x)


def _row_tile(m, cap):
    return cap if m % cap == 0 else m


def _rmsnorm_kernel(x_ref, g_ref, o_ref):
    x = x_ref[...]
    y = x * lax.rsqrt(jnp.mean(x * x, axis=-1, keepdims=True) + EPS)
    o_ref[...] = (y * g_ref[...]).astype(o_ref.dtype)


def rmsnorm(x, g, out_dtype):
    m, d = x.shape
    tm = _row_tile(m, 256)
    return pl.pallas_call(
        _rmsnorm_kernel,
        grid=(m // tm,),
        in_specs=[pl.BlockSpec((tm, d), lambda i: (i, 0)),
                  pl.BlockSpec((1, d), lambda i: (0, 0))],
        out_specs=pl.BlockSpec((tm, d), lambda i: (i, 0)),
        out_shape=jax.ShapeDtypeStruct((m, d), out_dtype),
        compiler_params=_params("parallel"),
        name="rmsnorm",
    )(x, g.reshape(1, d))


def _matmul_kernel(*refs):
    if len(refs) == 3:
        a_ref, b_ref, o_ref = refs
        o_ref[...] = jnp.dot(a_ref[...], b_ref[...].astype(BF16), preferred_element_type=F32)
        return
    a_ref, a2_ref, b_ref, o_ref, o2_ref = refs
    w = b_ref[...].astype(BF16)
    o_ref[...] = jnp.dot(a_ref[...], w, preferred_element_type=F32)
    first_pass = pl.program_id(0) == 0

    @pl.when(first_pass)
    def _():
        o2_ref[...] = jnp.dot(a2_ref[...], w, preferred_element_type=F32)

    @pl.when(jnp.logical_not(first_pass))
    def _():
        o2_ref[...] = jnp.zeros_like(o2_ref)


def matmul_groups(a, b, layer, a_small=None):
    m, kd = a.shape
    n = b.shape[2]
    tm = _row_tile(m, 2048)
    tn = MXU_N
    per_group = GROUP_W // tn
    lhs = [a] if a_small is None else [a, a_small]
    in_specs = [pl.BlockSpec((tm, kd), lambda i, j: (i, 0))]
    out_specs = [pl.BlockSpec((None, tm, tn), lambda i, j: (j // per_group, i, j % per_group))]
    out_shape = [jax.ShapeDtypeStruct((n // GROUP_W, m, GROUP_W), F32)]
    if a_small is not None:
        m2 = a_small.shape[0]
        in_specs.append(pl.BlockSpec((m2, kd), lambda i, j: (0, 0)))
        out_specs.append(pl.BlockSpec((None, None, m2, tn),
                                      lambda i, j: (i, j // per_group, 0, j % per_group)))
        out_shape.append(jax.ShapeDtypeStruct((m // tm, n // GROUP_W, m2, GROUP_W), F32))
    in_specs.append(pl.BlockSpec((None, kd, tn), lambda i, j: (layer, 0, j)))
    outs = pl.pallas_call(
        _matmul_kernel,
        grid=(m // tm, n // tn),
        in_specs=in_specs,
        out_specs=out_specs,
        out_shape=out_shape,
        compiler_params=_params("parallel", "arbitrary"),
        name="matmul_groups",
    )(*lhs, b)
    return outs[0] if a_small is None else (outs[0], outs[1][0])


def _outproj_kernel(a0_ref, a1_ref, a2_ref, a3_ref, w_ref, x_ref, o_ref):
    acc = x_ref[...]
    for g, a_ref in enumerate((a0_ref, a1_ref, a2_ref, a3_ref)):
        acc += jnp.dot(a_ref[...].astype(BF16), w_ref[g * GROUP_W:(g + 1) * GROUP_W, :].astype(BF16),
                       preferred_element_type=F32)
    o_ref[...] = acc


def outproj(parts, w, layer, x):
    m, d = x.shape
    tm = _row_tile(m, 1024)
    tn = 2 * MXU_N
    part_spec = pl.BlockSpec((tm, GROUP_W), lambda i, j: (i, 0))
    return pl.pallas_call(
        _outproj_kernel,
        grid=(m // tm, d // tn),
        in_specs=[part_spec, part_spec, part_spec, part_spec,
                  pl.BlockSpec((None, 4 * GROUP_W, tn), lambda i, j: (layer, 0, j)),
                  pl.BlockSpec((tm, tn), lambda i, j: (i, j))],
        out_specs=pl.BlockSpec((tm, tn), lambda i, j: (i, j)),
        out_shape=jax.ShapeDtypeStruct((m, d), F32),
        compiler_params=_params("parallel", "arbitrary"),
        name="outproj",
    )(*parts, w, x)


def _s5_bu_kernel(u_ref, w_ref, o_ref):
    r = jnp.dot(u_ref[...].astype(BF16), w_ref[...], preferred_element_type=F32)
    half = r.shape[1] // 2
    o_ref[0] = r[:, :half]
    o_ref[1] = r[:, half:]


def s5_bu(z, w_bu):
    m = z.shape[1]
    tm = _row_tile(m, 512)
    kw = S5_GROUPS_PER_DOT * C_A
    nw = S5_GROUPS_PER_DOT * N_A
    return pl.pallas_call(
        _s5_bu_kernel,
        grid=(m // tm, S5_CHUNKS),
        in_specs=[pl.BlockSpec((None, tm, kw), lambda i, k: (G_UA, i, k)),
                  pl.BlockSpec((None, kw, 2 * nw), lambda i, k: (k, 0, 0))],
        out_specs=pl.BlockSpec((2, tm, nw), lambda i, k: (0, i, k)),
        out_shape=jax.ShapeDtypeStruct((2, m, S5_STATE), F32),
        compiler_params=_params("parallel", "parallel"),
        name="s5_bu",
    )(z, w_bu)


def _s5_scan_kernel(bu_ref, a_ref, h0_ref, h_ref, hl_ref, st_ref, *, tt):
    j = pl.program_id(1)

    @pl.when(j == 0)
    def _():
        st_ref[...] = h0_ref[...]

    a_re = a_ref[0]
    a_im = a_ref[1]

    def body(t, carry):
        h_re, h_im = carry
        n_re = a_re * h_re - a_im * h_im + bu_ref[0, t]
        n_im = a_re * h_im + a_im * h_re + bu_ref[1, t]
        h_ref[0, t] = n_re
        h_ref[1, t] = n_im
        return n_re, n_im

    h_re, h_im = lax.fori_loop(0, tt, body, (st_ref[0], st_ref[1]), unroll=min(tt, 8))
    st_ref[0] = h_re
    st_ref[1] = h_im

    @pl.when(j == pl.num_programs(1) - 1)
    def _():
        hl_ref[0] = h_re
        hl_ref[1] = h_im


def s5_scan(bu, a_bar, h0, bsz, t):
    rows = S5_STATE // LANES
    bu5 = bu.reshape(2, bsz, t, rows, LANES)
    tt = _row_tile(t, 128)
    h, h_last = pl.pallas_call(
        functools.partial(_s5_scan_kernel, tt=tt),
        grid=(bsz, t // tt),
        in_specs=[pl.BlockSpec((2, None, tt, rows, LANES), lambda b, j: (0, b, j, 0, 0)),
                  pl.BlockSpec((2, rows, LANES), lambda b, j: (0, 0, 0)),
                  pl.BlockSpec((2, None, rows, LANES), lambda b, j: (0, b, 0, 0))],
        out_specs=[pl.BlockSpec((2, None, tt, rows, LANES), lambda b, j: (0, b, j, 0, 0)),
                   pl.BlockSpec((2, None, rows, LANES), lambda b, j: (0, b, 0, 0))],
        out_shape=[jax.ShapeDtypeStruct((2, bsz, t, rows, LANES), F32),
                   jax.ShapeDtypeStruct((2, bsz, rows, LANES), F32)],
        scratch_shapes=[pltpu.VMEM((2, rows, LANES), F32)],
        compiler_params=_params("parallel", "arbitrary"),
        name="s5_scan",
    )(bu5, a_bar.reshape(2, rows, LANES), h0.reshape(2, bsz, rows, LANES))
    return h.reshape(2, bsz * t, S5_STATE), h_last.reshape(2, bsz, G_A, N_A)


def _s5_y_kernel(h_ref, u_ref, g_ref, wc_ref, d_ref, wg_ref, bg_ref, o_ref):
    nw = S5_GROUPS_PER_DOT * N_A
    parts = []
    for k in range(S5_CHUNKS):
        h_re = h_ref[0, :, k * nw:(k + 1) * nw].astype(BF16)
        h_im = h_ref[1, :, k * nw:(k + 1) * nw].astype(BF16)
        parts.append(jnp.dot(h_re, wc_ref[0, k], preferred_element_type=F32)
                     + jnp.dot(h_im, wc_ref[1, k], preferred_element_type=F32))
    y = jnp.concatenate(parts, axis=-1) + d_ref[...] * u_ref[...]
    y = jax.nn.gelu(y)
    glu = jnp.dot(y.astype(BF16), wg_ref[...], preferred_element_type=F32) + bg_ref[...]
    y = y * jax.nn.sigmoid(glu)
    o_ref[...] = (y * _silu(g_ref[...])).astype(o_ref.dtype)


def s5_y(h, z, wc, d, w_glu, b_glu, out_dtype):
    m = z.shape[1]
    tm = _row_tile(m, 256)
    kw = S5_GROUPS_PER_DOT * C_A
    nw = S5_GROUPS_PER_DOT * N_A
    return pl.pallas_call(
        _s5_y_kernel,
        grid=(m // tm,),
        in_specs=[pl.BlockSpec((2, tm, S5_STATE), lambda i: (0, i, 0)),
                  pl.BlockSpec((None, tm, GROUP_W), lambda i: (G_UA, i, 0)),
                  pl.BlockSpec((None, tm, GROUP_W), lambda i: (G_GA, i, 0)),
                  pl.BlockSpec((2, S5_CHUNKS, nw, kw), lambda i: (0, 0, 0, 0)),
                  pl.BlockSpec((1, GROUP_W), lambda i: (0, 0)),
                  pl.BlockSpec((GROUP_W, GROUP_W), lambda i: (0, 0)),
                  pl.BlockSpec((1, GROUP_W), lambda i: (0, 0))],
        out_specs=pl.BlockSpec((tm, GROUP_W), lambda i: (i, 0)),
        out_shape=jax.ShapeDtypeStruct((m, GROUP_W), out_dtype),
        compiler_params=_params("parallel"),
        name="s5_y",
    )(h, z, z, wc, d.reshape(1, GROUP_W), w_glu, b_glu.reshape(1, GROUP_W))


def _s5_fused_kernel(u_ref, g_ref, wbu_ref, pw_ref, h0_ref, wc_ref, d_ref, wg_ref, bg_ref,
                     o_ref, hl_ref, st_ref, h_scr, *, tt):
    j = pl.program_id(1)
    nw = S5_GROUPS_PER_DOT * N_A
    kw = S5_GROUPS_PER_DOT * C_A
    n_tiles = tt // SUBLANES

    @pl.when(j == 0)
    def _():
        st_ref[...] = h0_ref[...]

    def axpy(x_re, x_im, a_re, a_im, s_re, s_im):
        return x_re + (a_re * s_re - a_im * s_im), x_im + (a_re * s_im + a_im * s_re)

    u = u_ref[...]
    y_parts = []
    for k in range(S5_CHUNKS):
        cols = slice(k * nw, (k + 1) * nw)
        bu = jnp.dot(u[:, k * kw:(k + 1) * kw].astype(BF16), wbu_ref[k], preferred_element_type=F32)
        h_scr[0] = bu[:, :nw].reshape(n_tiles, SUBLANES, nw)
        h_scr[1] = bu[:, nw:].reshape(n_tiles, SUBLANES, nw)

        def tile_step(g, carry, cols=cols):
            c_re, c_im = carry
            x_re, x_im = h_scr[0, g], h_scr[1, g]
            for p, shift in enumerate((1, 2, 4)):
                x_re, x_im = axpy(x_re, x_im, pw_ref[0, p, :, cols], pw_ref[1, p, :, cols],
                                  pltpu.roll(x_re, shift, axis=0), pltpu.roll(x_im, shift, axis=0))
            x_re, x_im = axpy(x_re, x_im, pw_ref[0, 3, :, cols], pw_ref[1, 3, :, cols], c_re, c_im)
            h_scr[0, g] = x_re
            h_scr[1, g] = x_im
            return x_re[SUBLANES - 1:SUBLANES, :], x_im[SUBLANES - 1:SUBLANES, :]

        c_re, c_im = lax.fori_loop(0, n_tiles, tile_step, (st_ref[0, :, cols], st_ref[1, :, cols]),
                                   unroll=min(n_tiles, 4))
        st_ref[0, :, cols] = c_re
        st_ref[1, :, cols] = c_im
        h_re = h_scr[0].reshape(tt, nw).astype(BF16)
        h_im = h_scr[1].reshape(tt, nw).astype(BF16)
        y_parts.append(jnp.dot(h_re, wc_ref[0, k], preferred_element_type=F32)
                       + jnp.dot(h_im, wc_ref[1, k], preferred_element_type=F32))
    y = jnp.concatenate(y_parts, axis=-1) + d_ref[...] * u
    y = jax.nn.gelu(y)
    glu = jnp.dot(y.astype(BF16), wg_ref[...], preferred_element_type=F32) + bg_ref[...]
    y = y * jax.nn.sigmoid(glu)
    o_ref[...] = (y * _silu(g_ref[...])).astype(o_ref.dtype)

    @pl.when(j == pl.num_programs(1) - 1)
    def _():
        hl_ref[...] = st_ref[...]


def s5_fused(z4, lw, h0, out_dtype):
    _, bsz, t, _ = z4.shape
    tt = _row_tile(t, 512)
    kw = S5_GROUPS_PER_DOT * C_A
    nw = S5_GROUPS_PER_DOT * N_A

    def tok(g):
        return pl.BlockSpec((None, None, tt, GROUP_W), lambda b, j: (g, b, j, 0))

    def whole(shape):
        return pl.BlockSpec(shape, lambda b, j: (0,) * len(shape))

    state_spec = pl.BlockSpec((2, None, 1, S5_STATE), lambda b, j: (0, b, 0, 0))
    y, h_last = pl.pallas_call(
        functools.partial(_s5_fused_kernel, tt=tt),
        grid=(bsz, t // tt),
        in_specs=[tok(G_UA), tok(G_GA),
                  whole((S5_CHUNKS, kw, 2 * nw)),
                  whole((2, 4, SUBLANES, S5_STATE)),
                  state_spec,
                  whole((2, S5_CHUNKS, nw, kw)),
                  whole((1, GROUP_W)), whole((GROUP_W, GROUP_W)), whole((1, GROUP_W))],
        out_specs=[pl.BlockSpec((None, tt, GROUP_W), lambda b, j: (b, j, 0)), state_spec],
        out_shape=[jax.ShapeDtypeStruct((bsz, t, GROUP_W), out_dtype),
                   jax.ShapeDtypeStruct((2, bsz, 1, S5_STATE), F32)],
        scratch_shapes=[pltpu.VMEM((2, 1, S5_STATE), F32),
                        pltpu.VMEM((2, tt // SUBLANES, SUBLANES, nw), F32)],
        compiler_params=_params("parallel", "arbitrary"),
        name="s5_fused",
    )(z4, z4, lw['w_bu'], lw['a_pow'], h0.reshape(2, bsz, 1, S5_STATE), lw['wc'],
      lw['s5_d'].reshape(1, GROUP_W), lw['w_glu'], lw['b_glu'].reshape(1, GROUP_W))
    return y, h_last.reshape(2, bsz, G_A, N_A)


def s5_scan_multipliers(a_pair):
    a = lax.complex(a_pair[0], a_pair[1]).reshape(S5_STATE)
    powers = [a]
    for _ in range(SUBLANES - 1):
        powers.append(powers[-1] * a)
    row = jnp.arange(SUBLANES)[:, None]
    planes = [jnp.where(row >= s, powers[s - 1][None, :], 0.0) for s in (1, 2, 4)]
    planes.append(jnp.stack(powers))
    pw = jnp.stack(planes)
    return jnp.stack([pw.real, pw.imag])


def s5_weights(lam_re, lam_im, log_dt, b_re, b_im, c_re, c_im):
    lam = lax.complex(lam_re.astype(F32), lam_im.astype(F32))
    dt = jnp.exp(log_dt.astype(F32))[:, None]
    a_bar = jnp.exp(lam * dt)
    b_bar = ((a_bar - 1.0) / lam)[..., None] * lax.complex(b_re.astype(F32), b_im.astype(F32))
    gpd, kw, nw = S5_GROUPS_PER_DOT, S5_GROUPS_PER_DOT * C_A, S5_GROUPS_PER_DOT * N_A
    bb = jnp.stack([b_bar.real, b_bar.imag]).reshape(2, S5_CHUNKS, gpd, N_A, C_A)
    b_cols = bb.transpose(1, 4, 0, 2, 3).reshape(S5_CHUNKS, C_A, 2 * nw)
    row_g = jnp.arange(kw)[:, None] // C_A
    col_h = (jnp.arange(2 * nw)[None, :] % nw) // N_A
    w_bu = jnp.where(row_g == col_h, jnp.tile(b_cols, (1, gpd, 1)), 0.0).astype(BF16)
    cc = jnp.stack([c_re.astype(F32), -c_im.astype(F32)]).reshape(2, S5_CHUNKS, gpd, C_A, N_A)
    c_rows = cc.transpose(0, 1, 2, 4, 3).reshape(2, S5_CHUNKS, nw, C_A)
    wc = jnp.where(jnp.arange(nw)[:, None] // N_A == jnp.arange(kw)[None, :] // C_A,
                   jnp.tile(c_rows, (1, 1, 1, gpd)), 0.0).astype(BF16)
    a_pair = jnp.stack([a_bar.real, a_bar.imag])
    return a_pair, w_bu, wc


def _hgrn_kernel(q_ref, f_ref, i_ref, g_ref, lb_ref, ng_ref, s0_ref, y_ref, sl_ref,
                 st_ref, pad_scr, *, tc, chunk, t_valid):
    sub = HGRN_SUB
    n_sb = chunk // sub
    j = pl.program_id(2)
    nt = (((1,), (1,)), ((), ()))

    @pl.when(j == 0)
    def _():
        st_ref[...] = s0_ref[...].T

    lb = lb_ref[...]
    padded = tc < chunk
    if padded:
        pad_scr[...] = jnp.zeros_like(pad_scr)
        pad_scr[0, 0:tc, :] = q_ref[...]
        pad_scr[1, 0:tc, :] = f_ref[...]
        pad_scr[2, 0:tc, :] = i_ref[...]
    row_c = lax.broadcasted_iota(jnp.int32, (chunk, LANES), 0)
    row_t = lax.broadcasted_iota(jnp.int32, (sub, chunk), 0)
    lane_s = lax.broadcasted_iota(jnp.int32, (sub, chunk), 1)
    tri = (lax.broadcasted_iota(jnp.int32, (chunk, chunk), 1)
           <= lax.broadcasted_iota(jnp.int32, (chunk, chunk), 0)).astype(F32)

    def chunk_step(ci, carry):
        r0 = pl.multiple_of(ci * chunk, chunk)
        if padded:
            q, f_pre, inp = pad_scr[0], pad_scr[1], pad_scr[2]
        else:
            q = q_ref[pl.ds(r0, chunk), :]
            f_pre = f_ref[pl.ds(r0, chunk), :]
            inp = i_ref[pl.ds(r0, chunk), :]
        f = lb + (1.0 - lb) * jax.nn.sigmoid(f_pre)
        log_f = jnp.log2(f)
        kk = 1.0 - f
        if padded:
            log_f = jnp.where(row_c < t_valid, log_f, 0.0)
            kk = jnp.where(row_c < t_valid, kk, 0.0)
        cum = jnp.dot(tri, log_f, precision=lax.Precision.HIGHEST, preferred_element_type=F32)
        last = cum[chunk - 1:chunk, :]
        st = st_ref[...]
        o = lax.dot_general((q * jnp.exp2(cum)).astype(BF16), st.astype(BF16), nt,
                            preferred_element_type=F32)
        inp_b = inp.astype(BF16)
        blocks = []
        for i in range(n_sb):
            rows = slice(i * sub, (i + 1) * sub)
            q_i, kk_i, cum_i = q[rows], kk[rows], cum[rows]
            if i > 0:
                edge = cum[i * sub - 1:i * sub, :]
                a_i = q_i * jnp.exp2(cum_i - edge)
                kt_i = kk * jnp.exp2(jnp.where(row_c < i * sub, edge - cum, -jnp.inf))
                sc = lax.dot_general(a_i.astype(BF16), kt_i.astype(BF16), nt, preferred_element_type=F32)
            else:
                sc = jnp.zeros((sub, chunk), F32)
            for s in range(min(sub, t_valid)):
                w = (q_i * kk_i[s:s + 1, :]) * jnp.exp2(cum_i - cum_i[s:s + 1, :])
                sc = jnp.where(lane_s == i * sub + s, jnp.sum(w, axis=-1, keepdims=True), sc)
            blocks.append(jnp.where(lane_s - i * sub > row_t, 0.0, sc))
        scores = blocks[0] if n_sb == 1 else jnp.concatenate(blocks, axis=0)
        o = o + jnp.dot(scores.astype(BF16), inp_b, preferred_element_type=F32)
        kt = (kk * jnp.exp2(last - cum)).astype(BF16)
        upd = lax.dot_general(inp_b, kt, (((0,), (0,)), ((), ())), preferred_element_type=F32)
        st_ref[...] = st * jnp.exp2(last) + upd
        y = o * lax.rsqrt(jnp.mean(o * o, axis=-1, keepdims=True) + EPS) * ng_ref[...]
        if padded:
            y_ref[...] = (y[0:tc] * _silu(g_ref[...])).astype(y_ref.dtype)
        else:
            y_ref[pl.ds(r0, chunk), :] = (y * _silu(g_ref[pl.ds(r0, chunk), :])).astype(y_ref.dtype)
        return carry

    lax.fori_loop(0, max(tc // chunk, 1), chunk_step, 0, unroll=True)

    @pl.when(j == pl.num_programs(2) - 1)
    def _():
        sl_ref[...] = st_ref[...].T


def hgrn(z4, lb, norm_g, s0, out_dtype):
    _, bsz, t, _ = z4.shape
    chunk = HGRN_CHUNK if t % HGRN_CHUNK == 0 else HGRN_SUB
    tc = _row_tile(t, 512) if t >= chunk else t
    t_valid = min(chunk, tc)

    def col(g):
        return pl.BlockSpec((None, None, tc, DK_B), lambda b, h, j: (g, b, j, h))

    state_spec = pl.BlockSpec((None, None, DK_B, DV_B), lambda b, h, j: (b, h, 0, 0))
    return pl.pallas_call(
        functools.partial(_hgrn_kernel, tc=tc, chunk=chunk, t_valid=t_valid),
        grid=(bsz, H_B, t // tc),
        in_specs=[col(G_QB), col(G_FB), col(G_IB), col(G_GB),
                  pl.BlockSpec((1, DK_B), lambda b, h, j: (0, h)),
                  pl.BlockSpec((1, DV_B), lambda b, h, j: (0, 0)),
                  state_spec],
        out_specs=[pl.BlockSpec((None, tc, DV_B), lambda b, h, j: (b, j, h)), state_spec],
        out_shape=[jax.ShapeDtypeStruct((bsz, t, GROUP_W), out_dtype),
                   jax.ShapeDtypeStruct((bsz, H_B, DK_B, DV_B), F32)],
        scratch_shapes=[pltpu.VMEM((DV_B, DK_B), F32),
                        pltpu.VMEM((3, chunk, LANES), F32)],
        compiler_params=_params("parallel", "parallel", "arbitrary"),
        name="hgrn",
    )(z4, z4, z4, z4, lb.reshape(1, GROUP_W), norm_g.reshape(1, DV_B), s0)


def _diff_norm_gate(o, ng, gate, out_scale):
    y = o * lax.rsqrt(jnp.mean(o * o, axis=-1, keepdims=True) + EPS) * ng
    return (y * out_scale) * _silu(gate)


def _dattn_prompt_kernel(lam_ref, q_ref, k_ref, v_ref, g_ref, ng_ref, o_ref,
                         kb_scr, vb_scr, *, tq, n_blk, out_scale):
    i = pl.program_id(2)

    @pl.when(i == 0)
    def _():
        kb_scr[...] = k_ref[...].astype(BF16)
        vb_scr[...] = v_ref[...].astype(BF16)

    nt = (((1,), (1,)), ((), ()))

    def attend(n_past):
        past = n_past * tq
        q = q_ref[...] * (DH_C ** -0.5 * LOG2_E)
        lane = lax.broadcasted_iota(jnp.int32, (tq, DV_C), 1)
        causal = (lax.broadcasted_iota(jnp.int32, (tq, tq), 1)
                  <= lax.broadcasted_iota(jnp.int32, (tq, tq), 0))
        k_diag = kb_scr[past:past + tq, :]
        v_diag = vb_scr[past:past + tq, :]

        def softmax_pv(qj):
            s_d = jnp.where(causal, lax.dot_general(qj, k_diag, nt, preferred_element_type=F32), -jnp.inf)
            m = jnp.max(s_d, axis=-1, keepdims=True)
            if n_past:
                s_p = lax.dot_general(qj, kb_scr[0:past, :], nt, preferred_element_type=F32)
                m = jnp.maximum(m, jnp.max(s_p, axis=-1, keepdims=True))
            e_d = jnp.exp2(s_d - m)
            l = jnp.sum(e_d, axis=-1, keepdims=True)
            pv = jnp.dot(e_d.astype(BF16), v_diag, preferred_element_type=F32)
            if n_past:
                e_p = jnp.exp2(s_p - m)
                l = l + jnp.sum(e_p, axis=-1, keepdims=True)
                pv = pv + jnp.dot(e_p.astype(BF16), vb_scr[0:past, :], preferred_element_type=F32)
            return pv / l

        o = (softmax_pv(jnp.where(lane < DH_C, q, 0.0).astype(BF16))
             - lam_ref[...] * softmax_pv(jnp.where(lane >= DH_C, q, 0.0).astype(BF16)))
        o_ref[...] = _diff_norm_gate(o, ng_ref[...], g_ref[...], out_scale).astype(o_ref.dtype)

    for n_past in range(n_blk):
        pl.when(i == n_past)(functools.partial(attend, n_past))


def dattn_prompt(z4, lam, norm_g, lam_init, out_dtype):
    _, bsz, t, _ = z4.shape
    tq = _row_tile(t, 512)

    def q_col(g):
        return pl.BlockSpec((None, None, tq, DV_C), lambda b, h, i: (g, b, i, h))

    def kv_col(g):
        return pl.BlockSpec((None, None, t, DV_C), lambda b, h, i: (g, b, 0, h))

    vec = pl.BlockSpec((1, DV_C), lambda b, h, i: (0, 0))
    return pl.pallas_call(
        functools.partial(_dattn_prompt_kernel, tq=tq, n_blk=t // tq, out_scale=1.0 - lam_init),
        grid=(bsz, H_C, t // tq),
        in_specs=[vec, q_col(G_QC), kv_col(G_KC), kv_col(G_VC), q_col(G_GC), vec],
        out_specs=pl.BlockSpec((None, tq, DV_C), lambda b, h, i: (b, i, h)),
        out_shape=jax.ShapeDtypeStruct((bsz, t, GROUP_W), out_dtype),
        scratch_shapes=[pltpu.VMEM((t, DV_C), BF16),
                        pltpu.VMEM((t, DV_C), BF16)],
        compiler_params=_params("parallel", "parallel", "arbitrary"),
        name="dattn_prompt",
    )(jnp.full((1, DV_C), lam, F32), z4, z4, z4, z4, norm_g.reshape(1, DV_C))


def _dattn_sample_kernel(pt_ref, lam_ref, q_ref, kn_ref, vn_ref, g_ref, ng_ref, *rest,
                         pps, t, out_scale):
    k_refs = rest[:pps]
    v_refs = rest[pps:2 * pps]
    o_ref = rest[2 * pps]
    qm_scr, m_scr, l_scr, acc_scr, kn_scr, vn_scr = rest[2 * pps + 1:]
    del pt_ref
    j = pl.program_id(1)
    half = t * H_C
    n_rows = 2 * half
    page_rows = k_refs[0].shape[0] * H_C
    nt = (((1,), (1,)), ((), ()))

    @pl.when(j == 0)
    def _():
        q = q_ref[...].reshape(half, DV_C) * (DH_C ** -0.5)
        lane = lax.broadcasted_iota(jnp.int32, (half, DV_C), 1)
        qm_scr[0:half, :] = jnp.where(lane < DH_C, q, 0.0).astype(BF16)
        qm_scr[half:n_rows, :] = jnp.where(lane >= DH_C, q, 0.0).astype(BF16)
        m_scr[...] = jnp.full_like(m_scr, -jnp.inf)
        l_scr[...] = jnp.zeros_like(l_scr)
        acc_scr[...] = jnp.zeros_like(acc_scr)

    def online_update(s, v_blocks, width):
        m_old = m_scr[...]
        m_new = jnp.maximum(m_old, jnp.max(s, axis=-1, keepdims=True))
        alpha = jnp.exp(m_old - m_new)
        e = jnp.exp(s - m_new)
        l_scr[...] = alpha * l_scr[...] + jnp.sum(e, axis=-1, keepdims=True)
        pv = None
        for r, vb in enumerate(v_blocks):
            d = jnp.dot(e[:, r * width:(r + 1) * width].astype(BF16), vb, preferred_element_type=F32)
            pv = d if pv is None else pv + d
        acc_scr[...] = alpha * acc_scr[...] + pv
        m_scr[...] = m_new

    qm = qm_scr[...]
    same_head = (lax.broadcasted_iota(jnp.int32, (n_rows, page_rows), 0) % H_C
                 == lax.broadcasted_iota(jnp.int32, (n_rows, page_rows), 1) % H_C)
    s_parts = []
    for r in range(pps):
        kp = k_refs[r][...].reshape(page_rows, DV_C).astype(BF16)
        s_parts.append(jnp.where(same_head, lax.dot_general(qm, kp, nt, preferred_element_type=F32),
                                 -jnp.inf))
    online_update(jnp.concatenate(s_parts, axis=-1),
                  [v_refs[r][...].reshape(page_rows, DV_C).astype(BF16) for r in range(pps)], page_rows)

    @pl.when(j == pl.num_programs(1) - 1)
    def _():
        kn_scr[...] = jnp.zeros_like(kn_scr)
        vn_scr[...] = jnp.zeros_like(vn_scr)
        kn_scr[0:half, :] = kn_ref[...].reshape(half, DV_C)
        vn_scr[0:half, :] = vn_ref[...].reshape(half, DV_C)
        s_new = lax.dot_general(qm, kn_scr[...].astype(BF16), nt, preferred_element_type=F32)
        row = lax.broadcasted_iota(jnp.int32, (n_rows, LANES), 0)
        col = lax.broadcasted_iota(jnp.int32, (n_rows, LANES), 1)
        visible = (col % H_C == row % H_C) & (col // H_C <= (row // H_C) % t)
        online_update(jnp.where(visible, s_new, -jnp.inf), [vn_scr[...].astype(BF16)], LANES)

        o = (acc_scr[0:half, :] / l_scr[0:half, :]
             - lam_ref[...] * (acc_scr[half:n_rows, :] / l_scr[half:n_rows, :]))
        y = _diff_norm_gate(o, ng_ref[...], g_ref[...].reshape(half, DV_C), out_scale)
        o_ref[...] = y.reshape(t, H_C, DV_C).astype(o_ref.dtype)


def dattn_sample(z4, cache_k, cache_v, layer, page_table, lam, norm_g, lam_init, out_dtype):
    _, bsz, t, _ = z4.shape
    page = cache_k.shape[2]
    n_pages = page_table.shape[1]
    pps = 16 if n_pages % 16 == 0 else 1
    heads = z4[G_QC:G_GC + 1].reshape(4, bsz, t, H_C, DV_C)

    def tok(g):
        return pl.BlockSpec((None, None, t, H_C, DV_C), lambda b, j, pt: (g, b, 0, 0, 0))

    def page_spec(r):
        return pl.BlockSpec((None, None, page, H_C, DV_C),
                            lambda b, j, pt: (layer, pt[b * n_pages + j * pps + r], 0, 0, 0))

    vec = pl.BlockSpec((1, DV_C), lambda b, j, pt: (0, 0))
    n_rows = 2 * t * H_C
    grid_spec = pltpu.PrefetchScalarGridSpec(
        num_scalar_prefetch=1,
        grid=(bsz, n_pages // pps),
        in_specs=[vec, tok(0), tok(1), tok(2), tok(3), vec]
        + [page_spec(r) for r in range(pps)] + [page_spec(r) for r in range(pps)],
        out_specs=pl.BlockSpec((None, t, H_C, DV_C), lambda b, j, pt: (b, 0, 0, 0)),
        scratch_shapes=[pltpu.VMEM((n_rows, DV_C), BF16),
                        pltpu.VMEM((n_rows, 1), F32),
                        pltpu.VMEM((n_rows, 1), F32),
                        pltpu.VMEM((n_rows, DV_C), F32),
                        pltpu.VMEM((LANES, DV_C), F32),
                        pltpu.VMEM((LANES, DV_C), F32)],
    )
    y = pl.pallas_call(
        functools.partial(_dattn_sample_kernel, pps=pps, t=t, out_scale=1.0 - lam_init),
        grid_spec=grid_spec,
        out_shape=jax.ShapeDtypeStruct((bsz, t, H_C, DV_C), out_dtype),
        compiler_params=_params("parallel", "arbitrary"),
        name="dattn_sample",
    )(page_table.reshape(-1), jnp.full((1, DV_C), lam, F32), heads, heads, heads, heads,
      norm_g.reshape(1, DV_C), *([cache_k] * pps), *([cache_v] * pps))
    return y.reshape(bsz, t, H_C * DV_C)


def _mem_attn_kernel(q_ref, g_ref, mk_ref, mv_ref, o_ref, pad_scr, *, tq):
    rows = pad_scr.shape[0]
    if tq < rows:
        pad_scr[...] = jnp.zeros_like(pad_scr)
        pad_scr[0:tq, :] = q_ref[...]
        q_all = pad_scr[...]
    else:
        q_all = q_ref[...]
    nt = (((1,), (1,)), ((), ()))
    for h in range(H_M):
        cols = slice(h * DH_M, (h + 1) * DH_M)
        q = (q_all[:, cols] * (DH_M ** -0.5)).astype(BF16)
        s = lax.dot_general(q, mk_ref[:, cols].astype(BF16), nt, preferred_element_type=F32)
        e = jnp.exp(s - jnp.max(s, axis=-1, keepdims=True))
        l = jnp.sum(e, axis=-1, keepdims=True)
        o = jnp.dot(e.astype(BF16), mv_ref[:, cols].astype(BF16), preferred_element_type=F32) / l
        o_ref[:, cols] = (o[0:tq, :] * _silu(g_ref[:, cols])).astype(o_ref.dtype)


def mem_attn(z4, mem_k, mem_v, out_dtype):
    _, bsz, t, _ = z4.shape
    n_mem = mem_k.shape[1]
    tq = _row_tile(t, 256)
    rows = max(tq, 2 * SUBLANES)

    def tok(g):
        return pl.BlockSpec((None, None, tq, GROUP_W), lambda b, i: (g, b, i, 0))

    mem_spec = pl.BlockSpec((None, n_mem, GROUP_W), lambda b, i: (b, 0, 0))
    return pl.pallas_call(
        functools.partial(_mem_attn_kernel, tq=tq),
        grid=(bsz, t // tq),
        in_specs=[tok(G_QM), tok(G_GM), mem_spec, mem_spec],
        out_specs=pl.BlockSpec((None, tq, GROUP_W), lambda b, i: (b, i, 0)),
        out_shape=jax.ShapeDtypeStruct((bsz, t, GROUP_W), out_dtype),
        scratch_shapes=[pltpu.VMEM((rows, GROUP_W), F32)],
        compiler_params=_params("parallel", "parallel"),
        name="mem_attn",
    )(z4, z4, mem_k.reshape(bsz, n_mem, GROUP_W), mem_v.reshape(bsz, n_mem, GROUP_W))


def input_projection(x_p, x_s, lw):
    d = x_p.shape[-1]
    h_p = rmsnorm(x_p.reshape(-1, d), lw['norm_g'], BF16)
    h_s = rmsnorm(x_s.reshape(-1, d), lw['norm_g'], BF16)
    return matmul_groups(h_p, lw['w_in'], lw['layer'], a_small=h_s)


def trunk_layer(x, z, lw, attend, mem_k, mem_v, s5_h0, hgrn_s0, part_dtype):
    bsz, t, d = x.shape
    m = bsz * t
    x2 = x.reshape(m, d)
    z4 = z.reshape(N_GROUPS_IN, bsz, t, GROUP_W)

    if t % SUBLANES == 0:
        y_a, s5_last = s5_fused(z4, lw, s5_h0, part_dtype)
        y_a = y_a.reshape(m, GROUP_W)
    else:
        bu = s5_bu(z, lw['w_bu'])
        h_all, s5_last = s5_scan(bu, lw['a_bar'], s5_h0, bsz, t)
        y_a = s5_y(h_all, z, lw['wc'], lw['s5_d'], lw['w_glu'], lw['b_glu'], part_dtype)

    y_b, hgrn_s = hgrn(z4, lw['lb'], lw['hgrn_norm_g'], hgrn_s0, part_dtype)
    y_c = attend(z4)
    y_m = mem_attn(z4, mem_k, mem_v, part_dtype)

    parts = (y_a, y_b.reshape(m, GROUP_W), y_c.reshape(m, GROUP_W), y_m.reshape(m, GROUP_W))
    x_out = outproj(parts, lw['w_out'], lw['layer'], x2).reshape(bsz, t, d)
    return x_out, z4, s5_last, hgrn_s


def kernel(x_prompt, x_sample, cache_k, cache_v, cache_mem_k, cache_mem_v, state_s5_re, state_s5_im, state_hgrn, page_table, mem_prompt, norm_g, w_in, s5_lambda_re, s5_lambda_im, s5_log_dt, s5_b_re, s5_b_im, s5_c_re, s5_c_im, s5_d, s5_w_glu, s5_b_glu, hgrn_lower_bounds, hgrn_norm_g, diff_lq1, diff_lk1, diff_lq2, diff_lk2, diff_norm_g, mem_norm_g, w_mem_kv, w_out, final_norm_g):
    depth = w_in.shape[0]
    bsz_p, t_p, d = x_prompt.shape
    bsz_s, t_s, _ = x_sample.shape
    n_mem = mem_prompt.shape[1]
    lb_all = jnp.cumsum(jax.nn.softmax(hgrn_lower_bounds.astype(F32), axis=0), axis=0)
    lb_all = lb_all - lb_all[0]
    zeros_s5 = jnp.zeros((2, bsz_p, G_A, N_A), F32)
    zeros_hgrn = jnp.zeros((bsz_p, H_B, DK_B, DV_B), F32)

    lam_inits = [0.8 - 0.6 * math.exp(-0.3 * l) for l in range(depth)]
    lams = (jnp.exp(jnp.sum(diff_lq1.astype(F32) * diff_lk1.astype(F32), axis=-1))
            - jnp.exp(jnp.sum(diff_lq2.astype(F32) * diff_lk2.astype(F32), axis=-1))
            + jnp.asarray(lam_inits, F32))
    a_bars, w_bus, wcs = jax.vmap(s5_weights)(s5_lambda_re, s5_lambda_im, s5_log_dt,
                                              s5_b_re, s5_b_im, s5_c_re, s5_c_im)
    a_pows = jax.vmap(s5_scan_multipliers)(a_bars)
    w_glus = s5_w_glu.astype(BF16)

    xp, xs = x_prompt, x_sample
    outs = {k: [] for k in ('kp', 'vp', 'ks', 'vs', 'mkp', 'mvp', 's5p', 's5s', 'hgp', 'hgs')}
    for l in range(depth):
        lam_init = lam_inits[l]
        lam = lams[l]
        lw = {'layer': l, 'norm_g': norm_g[l], 'w_in': w_in, 'w_out': w_out,
              'a_bar': a_bars[l], 'a_pow': a_pows[l], 'w_bu': w_bus[l], 'wc': wcs[l], 's5_d': s5_d[l],
              'w_glu': w_glus[l], 'b_glu': s5_b_glu[l],
              'lb': lb_all[l], 'hgrn_norm_g': hgrn_norm_g[l]}
        z_p, z_s = input_projection(xp, xs, lw)

        hm = rmsnorm(mem_prompt.reshape(bsz_p * n_mem, d), mem_norm_g[l], BF16)
        mkv = matmul_groups(hm, w_mem_kv, l)
        mk_p = mkv[0].reshape(bsz_p, n_mem, H_M, DH_M)
        mv_p = mkv[1].reshape(bsz_p, n_mem, H_M, DH_M)
        attend_p = functools.partial(dattn_prompt, lam=lam, norm_g=diff_norm_g[l],
                                     lam_init=lam_init, out_dtype=BF16)
        xp, z4, s5_last, hg = trunk_layer(xp, z_p, lw, attend_p, mk_p, mv_p, zeros_s5, zeros_hgrn, BF16)
        outs['kp'].append(z4[G_KC].reshape(bsz_p, t_p, H_C, 2 * DH_C))
        outs['vp'].append(z4[G_VC].reshape(bsz_p, t_p, H_C, DV_C))
        outs['mkp'].append(mk_p)
        outs['mvp'].append(mv_p)
        outs['s5p'].append(s5_last)
        outs['hgp'].append(hg)

        attend_s = functools.partial(dattn_sample, cache_k=cache_k, cache_v=cache_v, layer=l,
                                     page_table=page_table, lam=lam, norm_g=diff_norm_g[l],
                                     lam_init=lam_init, out_dtype=F32)
        s5_h0 = jnp.stack([state_s5_re[l].astype(F32), state_s5_im[l].astype(F32)])
        xs, z4, s5_last, hg = trunk_layer(xs, z_s, lw, attend_s, cache_mem_k[l], cache_mem_v[l],
                                          s5_h0, state_hgrn[l], F32)
        outs['ks'].append(z4[G_KC].reshape(bsz_s, t_s, H_C, 2 * DH_C))
        outs['vs'].append(z4[G_VC].reshape(bsz_s, t_s, H_C, DV_C))
        outs['s5s'].append(s5_last)
        outs['hgs'].append(hg)

    y_prompt = rmsnorm(xp.reshape(bsz_p * t_p, d), final_norm_g, F32).reshape(bsz_p, t_p, d)
    y_sample = rmsnorm(xs.reshape(bsz_s * t_s, d), final_norm_g, F32).reshape(bsz_s, t_s, d)
    s5p = jnp.stack(outs['s5p'])
    s5s = jnp.stack(outs['s5s'])
    return (y_prompt, y_sample,
            jnp.stack(outs['kp']), jnp.stack(outs['vp']), jnp.stack(outs['ks']), jnp.stack(outs['vs']),
            jnp.stack(outs['mkp']), jnp.stack(outs['mvp']),
            s5p[:, 0], s5p[:, 1], s5s[:, 0], s5s[:, 1],
            jnp.stack(outs['hgp']), jnp.stack(outs['hgs']))
```

```python
import functools
import math

import jax
import jax.numpy as jnp
from jax import lax
from jax.experimental import pallas as pl
from jax.experimental.pallas import tpu as pltpu

F32 = jnp.float32
BF16 = jnp.bfloat16

D_MODEL = 4096
GROUP_W = 1024
N_GROUPS_IN = 12
C_A = 16
G_A = GROUP_W // C_A
N_A = 64
S5_GROUPS_PER_DOT = 16
S5_CHUNKS = G_A // S5_GROUPS_PER_DOT
S5_STATE = G_A * N_A
DK_B = 128
DV_B = 128
H_B = GROUP_W // DK_B
DV_C = 128
DH_C = DV_C // 2
H_C = GROUP_W // DV_C
H_M = 4
DH_M = GROUP_W // H_M
EPS = 1e-6
LOG2_E = math.log2(math.e)
HGRN_CHUNK = 64
HGRN_SUB = 16
SUBLANES = 8
LANES = 128
MXU_N = 256
VMEM_LIMIT = 60 * 1024 * 1024

(G_UA, G_GA, G_QB, G_FB, G_IB, G_GB, G_QC, G_KC, G_VC, G_GC, G_QM, G_GM) = range(N_GROUPS_IN)


def _params(*sem):
    return pltpu.CompilerParams(dimension_semantics=sem, vmem_limit_bytes=VMEM_LIMIT)


def _silu(x):
    return x * jax.nn.sigmoid(x)


def _row_tile(m, cap):
    return cap if m % cap == 0 else m


def _rmsnorm_kernel(x_ref, g_ref, o_ref):
    x = x_ref[...]
    y = x * lax.rsqrt(jnp.mean(x * x, axis=-1, keepdims=True) + EPS)
    o_ref[...] = (y * g_ref[...]).astype(o_ref.dtype)


def rmsnorm(x, g, out_dtype):
    m, d = x.shape
    tm = _row_tile(m, 256)
    return pl.pallas_call(
        _rmsnorm_kernel,
        grid=(m // tm,),
        in_specs=[pl.BlockSpec((tm, d), lambda i: (i, 0)),
                  pl.BlockSpec((1, d), lambda i: (0, 0))],
        out_specs=pl.BlockSpec((tm, d), lambda i: (i, 0)),
        out_shape=jax.ShapeDtypeStruct((m, d), out_dtype),
        compiler_params=_params("parallel"),
        name="rmsnorm",
    )(x, g.reshape(1, d))


def _matmul_kernel(*refs):
    if len(refs) == 3:
        a_ref, b_ref, o_ref = refs
        o_ref[...] = jnp.dot(a_ref[...], b_ref[...].astype(BF16), preferred_element_type=F32)
        return
    a_ref, a2_ref, b_ref, o_ref, o2_ref = refs
    w = b_ref[...].astype(BF16)
    o_ref[...] = jnp.dot(a_ref[...], w, preferred_element_type=F32)
    first_pass = pl.program_id(0) == 0

    @pl.when(first_pass)
    def _():
        o2_ref[...] = jnp.dot(a2_ref[...], w, preferred_element_type=F32)

    @pl.when(jnp.logical_not(first_pass))
    def _():
        o2_ref[...] = jnp.zeros_like(o2_ref)


def matmul_groups(a, b, layer, a_small=None):
    m, kd = a.shape
    n = b.shape[2]
    tm = _row_tile(m, 2048)
    tn = MXU_N
    per_group = GROUP_W // tn
    lhs = [a] if a_small is None else [a, a_small]
    in_specs = [pl.BlockSpec((tm, kd), lambda i, j: (i, 0))]
    out_specs = [pl.BlockSpec((None, tm, tn), lambda i, j: (j // per_group, i, j % per_group))]
    out_shape = [jax.ShapeDtypeStruct((n // GROUP_W, m, GROUP_W), F32)]
    if a_small is not None:
        m2 = a_small.shape[0]
        in_specs.append(pl.BlockSpec((m2, kd), lambda i, j: (0, 0)))
        out_specs.append(pl.BlockSpec((None, None, m2, tn),
                                      lambda i, j: (i, j // per_group, 0, j % per_group)))
        out_shape.append(jax.ShapeDtypeStruct((m // tm, n // GROUP_W, m2, GROUP_W), F32))
    in_specs.append(pl.BlockSpec((None, kd, tn), lambda i, j: (layer, 0, j)))
    outs = pl.pallas_call(
        _matmul_kernel,
        grid=(m // tm, n // tn),
        in_specs=in_specs,
        out_specs=out_specs,
        out_shape=out_shape,
        compiler_params=_params("parallel", "arbitrary"),
        name="matmul_groups",
    )(*lhs, b)
    return outs[0] if a_small is None else (outs[0], outs[1][0])


def _outproj_kernel(a0_ref, a1_ref, a2_ref, a3_ref, w_ref, x_ref, o_ref):
    acc = x_ref[...]
    for g, a_ref in enumerate((a0_ref, a1_ref, a2_ref, a3_ref)):
        acc += jnp.dot(a_ref[...].astype(BF16), w_ref[g * GROUP_W:(g + 1) * GROUP_W, :].astype(BF16),
                       preferred_element_type=F32)
    o_ref[...] = acc


def outproj(parts, w, layer, x):
    m, d = x.shape
    tm = _row_tile(m, 2048)
    tn = MXU_N
    part_spec = pl.BlockSpec((tm, GROUP_W), lambda i, j: (i, 0))
    return pl.pallas_call(
        _outproj_kernel,
        grid=(m // tm, d // tn),
        in_specs=[part_spec, part_spec, part_spec, part_spec,
                  pl.BlockSpec((None, 4 * GROUP_W, tn), lambda i, j: (layer, 0, j)),
                  pl.BlockSpec((tm, tn), lambda i, j: (i, j))],
        out_specs=pl.BlockSpec((tm, tn), lambda i, j: (i, j)),
        out_shape=jax.ShapeDtypeStruct((m, d), F32),
        compiler_params=_params("parallel", "arbitrary"),
        name="outproj",
    )(*parts, w, x)


def _s5_bu_kernel(u_ref, w_ref, o_ref):
    r = jnp.dot(u_ref[...].astype(BF16), w_ref[...], preferred_element_type=F32)
    half = r.shape[1] // 2
    o_ref[0] = r[:, :half]
    o_ref[1] = r[:, half:]


def s5_bu(z, w_bu):
    m = z.shape[1]
    tm = _row_tile(m, 512)
    kw = S5_GROUPS_PER_DOT * C_A
    nw = S5_GROUPS_PER_DOT * N_A
    return pl.pallas_call(
        _s5_bu_kernel,
        grid=(m // tm, S5_CHUNKS),
        in_specs=[pl.BlockSpec((None, tm, kw), lambda i, k: (G_UA, i, k)),
                  pl.BlockSpec((None, kw, 2 * nw), lambda i, k: (k, 0, 0))],
        out_specs=pl.BlockSpec((2, tm, nw), lambda i, k: (0, i, k)),
        out_shape=jax.ShapeDtypeStruct((2, m, S5_STATE), F32),
        compiler_params=_params("parallel", "parallel"),
        name="s5_bu",
    )(z, w_bu)


def _s5_scan_kernel(bu_ref, a_ref, h0_ref, h_ref, hl_ref, st_ref, *, tt):
    j = pl.program_id(1)

    @pl.when(j == 0)
    def _():
        st_ref[...] = h0_ref[...]

    a_re = a_ref[0]
    a_im = a_ref[1]

    def body(t, carry):
        h_re, h_im = carry
        n_re = a_re * h_re - a_im * h_im + bu_ref[0, t]
        n_im = a_re * h_im + a_im * h_re + bu_ref[1, t]
        h_ref[0, t] = n_re
        h_ref[1, t] = n_im
        return n_re, n_im

    h_re, h_im = lax.fori_loop(0, tt, body, (st_ref[0], st_ref[1]), unroll=min(tt, 8))
    st_ref[0] = h_re
    st_ref[1] = h_im

    @pl.when(j == pl.num_programs(1) - 1)
    def _():
        hl_ref[0] = h_re
        hl_ref[1] = h_im


def s5_scan(bu, a_bar, h0, bsz, t):
    rows = S5_STATE // LANES
    bu5 = bu.reshape(2, bsz, t, rows, LANES)
    tt = _row_tile(t, 128)
    h, h_last = pl.pallas_call(
        functools.partial(_s5_scan_kernel, tt=tt),
        grid=(bsz, t // tt),
        in_specs=[pl.BlockSpec((2, None, tt, rows, LANES), lambda b, j: (0, b, j, 0, 0)),
                  pl.BlockSpec((2, rows, LANES), lambda b, j: (0, 0, 0)),
                  pl.BlockSpec((2, None, rows, LANES), lambda b, j: (0, b, 0, 0))],
        out_specs=[pl.BlockSpec((2, None, tt, rows, LANES), lambda b, j: (0, b, j, 0, 0)),
                   pl.BlockSpec((2, None, rows, LANES), lambda b, j: (0, b, 0, 0))],
        out_shape=[jax.ShapeDtypeStruct((2, bsz, t, rows, LANES), F32),
                   jax.ShapeDtypeStruct((2, bsz, rows, LANES), F32)],
        scratch_shapes=[pltpu.VMEM((2, rows, LANES), F32)],
        compiler_params=_params("parallel", "arbitrary"),
        name="s5_scan",
    )(bu5, a_bar.reshape(2, rows, LANES), h0.reshape(2, bsz, rows, LANES))
    return h.reshape(2, bsz * t, S5_STATE), h_last.reshape(2, bsz, G_A, N_A)


def _s5_y_kernel(h_ref, u_ref, g_ref, wc_ref, d_ref, wg_ref, bg_ref, o_ref):
    nw = S5_GROUPS_PER_DOT * N_A
    parts = []
    for k in range(S5_CHUNKS):
        h_re = h_ref[0, :, k * nw:(k + 1) * nw].astype(BF16)
        h_im = h_ref[1, :, k * nw:(k + 1) * nw].astype(BF16)
        parts.append(jnp.dot(h_re, wc_ref[0, k], preferred_element_type=F32)
                     + jnp.dot(h_im, wc_ref[1, k], preferred_element_type=F32))
    y = jnp.concatenate(parts, axis=-1) + d_ref[...] * u_ref[...]
    y = jax.nn.gelu(y)
    glu = jnp.dot(y.astype(BF16), wg_ref[...], preferred_element_type=F32) + bg_ref[...]
    y = y * jax.nn.sigmoid(glu)
    o_ref[...] = (y * _silu(g_ref[...])).astype(o_ref.dtype)


def s5_y(h, z, wc, d, w_glu, b_glu, out_dtype):
    m = z.shape[1]
    tm = _row_tile(m, 256)
    kw = S5_GROUPS_PER_DOT * C_A
    nw = S5_GROUPS_PER_DOT * N_A
    return pl.pallas_call(
        _s5_y_kernel,
        grid=(m // tm,),
        in_specs=[pl.BlockSpec((2, tm, S5_STATE), lambda i: (0, i, 0)),
                  pl.BlockSpec((None, tm, GROUP_W), lambda i: (G_UA, i, 0)),
                  pl.BlockSpec((None, tm, GROUP_W), lambda i: (G_GA, i, 0)),
                  pl.BlockSpec((2, S5_CHUNKS, nw, kw), lambda i: (0, 0, 0, 0)),
                  pl.BlockSpec((1, GROUP_W), lambda i: (0, 0)),
                  pl.BlockSpec((GROUP_W, GROUP_W), lambda i: (0, 0)),
                  pl.BlockSpec((1, GROUP_W), lambda i: (0, 0))],
        out_specs=pl.BlockSpec((tm, GROUP_W), lambda i: (i, 0)),
        out_shape=jax.ShapeDtypeStruct((m, GROUP_W), out_dtype),
        compiler_params=_params("parallel"),
        name="s5_y",
    )(h, z, z, wc, d.reshape(1, GROUP_W), w_glu, b_glu.reshape(1, GROUP_W))


def _s5_fused_kernel(u_ref, g_ref, wbu_ref, pw_ref, h0_ref, wc_ref, d_ref, wg_ref, bg_ref,
                     o_ref, hl_ref, st_ref, h_scr, *, tt):
    j = pl.program_id(1)
    nw = S5_GROUPS_PER_DOT * N_A
    kw = S5_GROUPS_PER_DOT * C_A
    n_tiles = tt // SUBLANES

    @pl.when(j == 0)
    def _():
        st_ref[...] = h0_ref[...]

    def axpy(x_re, x_im, a_re, a_im, s_re, s_im):
        return x_re + (a_re * s_re - a_im * s_im), x_im + (a_re * s_im + a_im * s_re)

    u = u_ref[...]
    y_parts = []
    for k in range(S5_CHUNKS):
        cols = slice(k * nw, (k + 1) * nw)
        bu = jnp.dot(u[:, k * kw:(k + 1) * kw].astype(BF16), wbu_ref[k], preferred_element_type=F32)
        h_scr[0] = bu[:, :nw].reshape(n_tiles, SUBLANES, nw)
        h_scr[1] = bu[:, nw:].reshape(n_tiles, SUBLANES, nw)

        def tile_step(g, carry, cols=cols):
            c_re, c_im = carry
            x_re, x_im = h_scr[0, g], h_scr[1, g]
            for p, shift in enumerate((1, 2, 4)):
                x_re, x_im = axpy(x_re, x_im, pw_ref[0, p, :, cols], pw_ref[1, p, :, cols],
                                  pltpu.roll(x_re, shift, axis=0), pltpu.roll(x_im, shift, axis=0))
            x_re, x_im = axpy(x_re, x_im, pw_ref[0, 3, :, cols], pw_ref[1, 3, :, cols], c_re, c_im)
            h_scr[0, g] = x_re
            h_scr[1, g] = x_im
            return x_re[SUBLANES - 1:SUBLANES, :], x_im[SUBLANES - 1:SUBLANES, :]

        c_re, c_im = lax.fori_loop(0, n_tiles, tile_step, (st_ref[0, :, cols], st_ref[1, :, cols]),
                                   unroll=True)
        st_ref[0, :, cols] = c_re
        st_ref[1, :, cols] = c_im
        h_re = h_scr[0].reshape(tt, nw).astype(BF16)
        h_im = h_scr[1].reshape(tt, nw).astype(BF16)
        y_parts.append(jnp.dot(h_re, wc_ref[0, k], preferred_element_type=F32)
                       + jnp.dot(h_im, wc_ref[1, k], preferred_element_type=F32))
    y = jnp.concatenate(y_parts, axis=-1) + d_ref[...] * u
    y = jax.nn.gelu(y)
    glu = jnp.dot(y.astype(BF16), wg_ref[...], preferred_element_type=F32) + bg_ref[...]
    y = y * jax.nn.sigmoid(glu)
    o_ref[...] = (y * _silu(g_ref[...])).astype(o_ref.dtype)

    @pl.when(j == pl.num_programs(1) - 1)
    def _():
        hl_ref[...] = st_ref[...]


def s5_fused(z4, lw, h0, out_dtype):
    _, bsz, t, _ = z4.shape
    tt = _row_tile(t, 512)
    kw = S5_GROUPS_PER_DOT * C_A
    nw = S5_GROUPS_PER_DOT * N_A

    def tok(g):
        return pl.BlockSpec((None, None, tt, GROUP_W), lambda b, j: (g, b, j, 0))

    def whole(shape):
        return pl.BlockSpec(shape, lambda b, j: (0,) * len(shape))

    state_spec = pl.BlockSpec((2, None, 1, S5_STATE), lambda b, j: (0, b, 0, 0))
    y, h_last = pl.pallas_call(
        functools.partial(_s5_fused_kernel, tt=tt),
        grid=(bsz, t // tt),
        in_specs=[tok(G_UA), tok(G_GA),
                  whole((S5_CHUNKS, kw, 2 * nw)),
                  whole((2, 4, SUBLANES, S5_STATE)),
                  state_spec,
                  whole((2, S5_CHUNKS, nw, kw)),
                  whole((1, GROUP_W)), whole((GROUP_W, GROUP_W)), whole((1, GROUP_W))],
        out_specs=[pl.BlockSpec((None, tt, GROUP_W), lambda b, j: (b, j, 0)), state_spec],
        out_shape=[jax.ShapeDtypeStruct((bsz, t, GROUP_W), out_dtype),
                   jax.ShapeDtypeStruct((2, bsz, 1, S5_STATE), F32)],
        scratch_shapes=[pltpu.VMEM((2, 1, S5_STATE), F32),
                        pltpu.VMEM((2, tt // SUBLANES, SUBLANES, nw), F32)],
        compiler_params=_params("parallel", "arbitrary"),
        name="s5_fused",
    )(z4, z4, lw['w_bu'], lw['a_pow'], h0.reshape(2, bsz, 1, S5_STATE), lw['wc'],
      lw['s5_d'].reshape(1, GROUP_W), lw['w_glu'], lw['b_glu'].reshape(1, GROUP_W))
    return y, h_last.reshape(2, bsz, G_A, N_A)


def s5_scan_multipliers(a_pair):
    a = lax.complex(a_pair[0], a_pair[1]).reshape(S5_STATE)
    powers = [a]
    for _ in range(SUBLANES - 1):
        powers.append(powers[-1] * a)
    row = jnp.arange(SUBLANES)[:, None]
    planes = [jnp.where(row >= s, powers[s - 1][None, :], 0.0) for s in (1, 2, 4)]
    planes.append(jnp.stack(powers))
    pw = jnp.stack(planes)
    return jnp.stack([pw.real, pw.imag])


def s5_weights(lam_re, lam_im, log_dt, b_re, b_im, c_re, c_im):
    lam = lax.complex(lam_re.astype(F32), lam_im.astype(F32))
    dt = jnp.exp(log_dt.astype(F32))[:, None]
    a_bar = jnp.exp(lam * dt)
    b_bar = ((a_bar - 1.0) / lam)[..., None] * lax.complex(b_re.astype(F32), b_im.astype(F32))
    gpd, kw, nw = S5_GROUPS_PER_DOT, S5_GROUPS_PER_DOT * C_A, S5_GROUPS_PER_DOT * N_A
    bb = jnp.stack([b_bar.real, b_bar.imag]).reshape(2, S5_CHUNKS, gpd, N_A, C_A)
    b_cols = bb.transpose(1, 4, 0, 2, 3).reshape(S5_CHUNKS, C_A, 2 * nw)
    row_g = jnp.arange(kw)[:, None] // C_A
    col_h = (jnp.arange(2 * nw)[None, :] % nw) // N_A
    w_bu = jnp.where(row_g == col_h, jnp.tile(b_cols, (1, gpd, 1)), 0.0).astype(BF16)
    cc = jnp.stack([c_re.astype(F32), -c_im.astype(F32)]).reshape(2, S5_CHUNKS, gpd, C_A, N_A)
    c_rows = cc.transpose(0, 1, 2, 4, 3).reshape(2, S5_CHUNKS, nw, C_A)
    wc = jnp.where(jnp.arange(nw)[:, None] // N_A == jnp.arange(kw)[None, :] // C_A,
                   jnp.tile(c_rows, (1, 1, 1, gpd)), 0.0).astype(BF16)
    a_pair = jnp.stack([a_bar.real, a_bar.imag])
    return a_pair, w_bu, wc


def _hgrn_kernel(q_ref, f_ref, i_ref, g_ref, lb_ref, ng_ref, s0_ref, y_ref, sl_ref,
                 st_ref, pad_scr, *, tc, chunk, t_valid):
    sub = HGRN_SUB
    n_sb = chunk // sub
    j = pl.program_id(2)
    nt = (((1,), (1,)), ((), ()))

    @pl.when(j == 0)
    def _():
        st_ref[...] = s0_ref[...].T

    lb = lb_ref[...]
    padded = tc < chunk
    if padded:
        pad_scr[...] = jnp.zeros_like(pad_scr)
        pad_scr[0, 0:tc, :] = q_ref[...]
        pad_scr[1, 0:tc, :] = f_ref[...]
        pad_scr[2, 0:tc, :] = i_ref[...]
    row_c = lax.broadcasted_iota(jnp.int32, (chunk, LANES), 0)
    row_t = lax.broadcasted_iota(jnp.int32, (sub, chunk), 0)
    lane_s = lax.broadcasted_iota(jnp.int32, (sub, chunk), 1)
    tri = (lax.broadcasted_iota(jnp.int32, (chunk, chunk), 1)
           <= lax.broadcasted_iota(jnp.int32, (chunk, chunk), 0)).astype(F32)

    def chunk_step(ci, carry):
        r0 = pl.multiple_of(ci * chunk, chunk)
        if padded:
            q, f_pre, inp = pad_scr[0], pad_scr[1], pad_scr[2]
        else:
            q = q_ref[pl.ds(r0, chunk), :]
            f_pre = f_ref[pl.ds(r0, chunk), :]
            inp = i_ref[pl.ds(r0, chunk), :]
        f = lb + (1.0 - lb) * jax.nn.sigmoid(f_pre)
        log_f = jnp.log2(f)
        kk = 1.0 - f
        if padded:
            log_f = jnp.where(row_c < t_valid, log_f, 0.0)
            kk = jnp.where(row_c < t_valid, kk, 0.0)
        cum = jnp.dot(tri, log_f, precision=lax.Precision.HIGHEST, preferred_element_type=F32)
        last = cum[chunk - 1:chunk, :]
        st = st_ref[...]
        o = lax.dot_general((q * jnp.exp2(cum)).astype(BF16), st.astype(BF16), nt,
                            preferred_element_type=F32)
        inp_b = inp.astype(BF16)
        blocks = []
        for i in range(n_sb):
            rows = slice(i * sub, (i + 1) * sub)
            q_i, kk_i, cum_i = q[rows], kk[rows], cum[rows]
            if i > 0:
                edge = cum[i * sub - 1:i * sub, :]
                a_i = q_i * jnp.exp2(cum_i - edge)
                kt_i = kk * jnp.exp2(jnp.where(row_c < i * sub, edge - cum, -jnp.inf))
                sc = lax.dot_general(a_i.astype(BF16), kt_i.astype(BF16), nt, preferred_element_type=F32)
            else:
                sc = jnp.zeros((sub, chunk), F32)
            for s in range(min(sub, t_valid)):
                w = (q_i * kk_i[s:s + 1, :]) * jnp.exp2(cum_i - cum_i[s:s + 1, :])
                sc = jnp.where(lane_s == i * sub + s, jnp.sum(w, axis=-1, keepdims=True), sc)
            blocks.append(jnp.where(lane_s - i * sub > row_t, 0.0, sc))
        scores = blocks[0] if n_sb == 1 else jnp.concatenate(blocks, axis=0)
        o = o + jnp.dot(scores.astype(BF16), inp_b, preferred_element_type=F32)
        kt = (kk * jnp.exp2(last - cum)).astype(BF16)
        upd = lax.dot_general(inp_b, kt, (((0,), (0,)), ((), ())), preferred_element_type=F32)
        st_ref[...] = st * jnp.exp2(last) + upd
        y = o * lax.rsqrt(jnp.mean(o * o, axis=-1, keepdims=True) + EPS) * ng_ref[...]
        if padded:
            y_ref[...] = (y[0:tc] * _silu(g_ref[...])).astype(y_ref.dtype)
        else:
            y_ref[pl.ds(r0, chunk), :] = (y * _silu(g_ref[pl.ds(r0, chunk), :])).astype(y_ref.dtype)
        return carry

    lax.fori_loop(0, max(tc // chunk, 1), chunk_step, 0, unroll=True)

    @pl.when(j == pl.num_programs(2) - 1)
    def _():
        sl_ref[...] = st_ref[...].T


def hgrn(z4, lb, norm_g, s0, out_dtype):
    _, bsz, t, _ = z4.shape
    chunk = HGRN_CHUNK if t % HGRN_CHUNK == 0 else HGRN_SUB
    tc = _row_tile(t, 512) if t >= chunk else t
    t_valid = min(chunk, tc)

    def col(g):
        return pl.BlockSpec((None, None, tc, DK_B), lambda b, h, j: (g, b, j, h))

    state_spec = pl.BlockSpec((None, None, DK_B, DV_B), lambda b, h, j: (b, h, 0, 0))
    return pl.pallas_call(
        functools.partial(_hgrn_kernel, tc=tc, chunk=chunk, t_valid=t_valid),
        grid=(bsz, H_B, t // tc),
        in_specs=[col(G_QB), col(G_FB), col(G_IB), col(G_GB),
                  pl.BlockSpec((1, DK_B), lambda b, h, j: (0, h)),
                  pl.BlockSpec((1, DV_B), lambda b, h, j: (0, 0)),
                  state_spec],
        out_specs=[pl.BlockSpec((None, tc, DV_B), lambda b, h, j: (b, j, h)), state_spec],
        out_shape=[jax.ShapeDtypeStruct((bsz, t, GROUP_W), out_dtype),
                   jax.ShapeDtypeStruct((bsz, H_B, DK_B, DV_B), F32)],
        scratch_shapes=[pltpu.VMEM((DV_B, DK_B), F32),
                        pltpu.VMEM((3, chunk, LANES), F32)],
        compiler_params=_params("parallel", "parallel", "arbitrary"),
        name="hgrn",
    )(z4, z4, z4, z4, lb.reshape(1, GROUP_W), norm_g.reshape(1, DV_B), s0)


def _diff_norm_gate(o, ng, gate, out_scale):
    y = o * lax.rsqrt(jnp.mean(o * o, axis=-1, keepdims=True) + EPS) * ng
    return (y * out_scale) * _silu(gate)


def _dattn_prompt_kernel(lam_ref, q_ref, k_ref, v_ref, g_ref, ng_ref, o_ref,
                         kb_scr, vb_scr, *, tq, n_blk, out_scale):
    i = pl.program_id(2)

    @pl.when(i == 0)
    def _():
        kb_scr[...] = k_ref[...].astype(BF16)
        vb_scr[...] = v_ref[...].astype(BF16)

    nt = (((1,), (1,)), ((), ()))

    def attend(n_past):
        past = n_past * tq
        q = q_ref[...] * (DH_C ** -0.5 * LOG2_E)
        lane = lax.broadcasted_iota(jnp.int32, (tq, DV_C), 1)
        causal = (lax.broadcasted_iota(jnp.int32, (tq, tq), 1)
                  <= lax.broadcasted_iota(jnp.int32, (tq, tq), 0))
        k_diag = kb_scr[past:past + tq, :]
        v_diag = vb_scr[past:past + tq, :]

        def softmax_pv(qj):
            s_d = jnp.where(causal, lax.dot_general(qj, k_diag, nt, preferred_element_type=F32), -jnp.inf)
            m = jnp.max(s_d, axis=-1, keepdims=True)
            if n_past:
                s_p = lax.dot_general(qj, kb_scr[0:past, :], nt, preferred_element_type=F32)
                m = jnp.maximum(m, jnp.max(s_p, axis=-1, keepdims=True))
            e_d = jnp.exp2(s_d - m)
            l = jnp.sum(e_d, axis=-1, keepdims=True)
            pv = jnp.dot(e_d.astype(BF16), v_diag, preferred_element_type=F32)
            if n_past:
                e_p = jnp.exp2(s_p - m)
                l = l + jnp.sum(e_p, axis=-1, keepdims=True)
                pv = pv + jnp.dot(e_p.astype(BF16), vb_scr[0:past, :], preferred_element_type=F32)
            return pv / l

        o = (softmax_pv(jnp.where(lane < DH_C, q, 0.0).astype(BF16))
             - lam_ref[...] * softmax_pv(jnp.where(lane >= DH_C, q, 0.0).astype(BF16)))
        o_ref[...] = _diff_norm_gate(o, ng_ref[...], g_ref[...], out_scale).astype(o_ref.dtype)

    for n_past in range(n_blk):
        pl.when(i == n_past)(functools.partial(attend, n_past))


def dattn_prompt(z4, lam, norm_g, lam_init, out_dtype):
    _, bsz, t, _ = z4.shape
    tq = _row_tile(t, 512)

    def q_col(g):
        return pl.BlockSpec((None, None, tq, DV_C), lambda b, h, i: (g, b, i, h))

    def kv_col(g):
        return pl.BlockSpec((None, None, t, DV_C), lambda b, h, i: (g, b, 0, h))

    vec = pl.BlockSpec((1, DV_C), lambda b, h, i: (0, 0))
    return pl.pallas_call(
        functools.partial(_dattn_prompt_kernel, tq=tq, n_blk=t // tq, out_scale=1.0 - lam_init),
        grid=(bsz, H_C, t // tq),
        in_specs=[vec, q_col(G_QC), kv_col(G_KC), kv_col(G_VC), q_col(G_GC), vec],
        out_specs=pl.BlockSpec((None, tq, DV_C), lambda b, h, i: (b, i, h)),
        out_shape=jax.ShapeDtypeStruct((bsz, t, GROUP_W), out_dtype),
        scratch_shapes=[pltpu.VMEM((t, DV_C), BF16),
                        pltpu.VMEM((t, DV_C), BF16)],
        compiler_params=_params("parallel", "parallel", "arbitrary"),
        name="dattn_prompt",
    )(jnp.full((1, DV_C), lam, F32), z4, z4, z4, z4, norm_g.reshape(1, DV_C))


def _dattn_sample_kernel(pt_ref, lam_ref, q_ref, kn_ref, vn_ref, g_ref, ng_ref, *rest,
                         pps, t, out_scale):
    k_refs = rest[:pps]
    v_refs = rest[pps:2 * pps]
    o_ref = rest[2 * pps]
    qm_scr, m_scr, l_scr, acc_scr, kn_scr, vn_scr = rest[2 * pps + 1:]
    del pt_ref
    j = pl.program_id(1)
    half = t * H_C
    n_rows = 2 * half
    page_rows = k_refs[0].shape[0] * H_C
    nt = (((1,), (1,)), ((), ()))

    @pl.when(j == 0)
    def _():
        q = q_ref[...].reshape(half, DV_C) * (DH_C ** -0.5)
        lane = lax.broadcasted_iota(jnp.int32, (half, DV_C), 1)
        qm_scr[0:half, :] = jnp.where(lane < DH_C, q, 0.0).astype(BF16)
        qm_scr[half:n_rows, :] = jnp.where(lane >= DH_C, q, 0.0).astype(BF16)
        m_scr[...] = jnp.full_like(m_scr, -jnp.inf)
        l_scr[...] = jnp.zeros_like(l_scr)
        acc_scr[...] = jnp.zeros_like(acc_scr)

    def online_update(s, v_blocks, width):
        m_old = m_scr[...]
        m_new = jnp.maximum(m_old, jnp.max(s, axis=-1, keepdims=True))
        alpha = jnp.exp(m_old - m_new)
        e = jnp.exp(s - m_new)
        l_scr[...] = alpha * l_scr[...] + jnp.sum(e, axis=-1, keepdims=True)
        pv = None
        for r, vb in enumerate(v_blocks):
            d = jnp.dot(e[:, r * width:(r + 1) * width].astype(BF16), vb, preferred_element_type=F32)
            pv = d if pv is None else pv + d
        acc_scr[...] = alpha * acc_scr[...] + pv
        m_scr[...] = m_new

    qm = qm_scr[...]
    same_head = (lax.broadcasted_iota(jnp.int32, (n_rows, page_rows), 0) % H_C
                 == lax.broadcasted_iota(jnp.int32, (n_rows, page_rows), 1) % H_C)
    s_parts = []
    for r in range(pps):
        kp = k_refs[r][...].reshape(page_rows, DV_C).astype(BF16)
        s_parts.append(jnp.where(same_head, lax.dot_general(qm, kp, nt, preferred_element_type=F32),
                                 -jnp.inf))
    online_update(jnp.concatenate(s_parts, axis=-1),
                  [v_refs[r][...].reshape(page_rows, DV_C).astype(BF16) for r in range(pps)], page_rows)

    @pl.when(j == pl.num_programs(1) - 1)
    def _():
        kn_scr[...] = jnp.zeros_like(kn_scr)
        vn_scr[...] = jnp.zeros_like(vn_scr)
        kn_scr[0:half, :] = kn_ref[...].reshape(half, DV_C)
        vn_scr[0:half, :] = vn_ref[...].reshape(half, DV_C)
        s_new = lax.dot_general(qm, kn_scr[...].astype(BF16), nt, preferred_element_type=F32)
        row = lax.broadcasted_iota(jnp.int32, (n_rows, LANES), 0)
        col = lax.broadcasted_iota(jnp.int32, (n_rows, LANES), 1)
        visible = (col % H_C == row % H_C) & (col // H_C <= (row // H_C) % t)
        online_update(jnp.where(visible, s_new, -jnp.inf), [vn_scr[...].astype(BF16)], LANES)

        o = (acc_scr[0:half, :] / l_scr[0:half, :]
             - lam_ref[...] * (acc_scr[half:n_rows, :] / l_scr[half:n_rows, :]))
        y = _diff_norm_gate(o, ng_ref[...], g_ref[...].reshape(half, DV_C), out_scale)
        o_ref[...] = y.reshape(t, H_C, DV_C).astype(o_ref.dtype)


def dattn_sample(z4, cache_k, cache_v, layer, page_table, lam, norm_g, lam_init, out_dtype):
    _, bsz, t, _ = z4.shape
    page = cache_k.shape[2]
    n_pages = page_table.shape[1]
    pps = 16 if n_pages % 16 == 0 else 1
    heads = z4[G_QC:G_GC + 1].reshape(4, bsz, t, H_C, DV_C)

    def tok(g):
        return pl.BlockSpec((None, None, t, H_C, DV_C), lambda b, j, pt: (g, b, 0, 0, 0))

    def page_spec(r):
        return pl.BlockSpec((None, None, page, H_C, DV_C),
                            lambda b, j, pt: (layer, pt[b * n_pages + j * pps + r], 0, 0, 0))

    vec = pl.BlockSpec((1, DV_C), lambda b, j, pt: (0, 0))
    n_rows = 2 * t * H_C
    grid_spec = pltpu.PrefetchScalarGridSpec(
        num_scalar_prefetch=1,
        grid=(bsz, n_pages // pps),
        in_specs=[vec, tok(0), tok(1), tok(2), tok(3), vec]
        + [page_spec(r) for r in range(pps)] + [page_spec(r) for r in range(pps)],
        out_specs=pl.BlockSpec((None, t, H_C, DV_C), lambda b, j, pt: (b, 0, 0, 0)),
        scratch_shapes=[pltpu.VMEM((n_rows, DV_C), BF16),
                        pltpu.VMEM((n_rows, 1), F32),
                        pltpu.VMEM((n_rows, 1), F32),
                        pltpu.VMEM((n_rows, DV_C), F32),
                        pltpu.VMEM((LANES, DV_C), F32),
                        pltpu.VMEM((LANES, DV_C), F32)],
    )
    y = pl.pallas_call(
        functools.partial(_dattn_sample_kernel, pps=pps, t=t, out_scale=1.0 - lam_init),
        grid_spec=grid_spec,
        out_shape=jax.ShapeDtypeStruct((bsz, t, H_C, DV_C), out_dtype),
        compiler_params=_params("parallel", "arbitrary"),
        name="dattn_sample",
    )(page_table.reshape(-1), jnp.full((1, DV_C), lam, F32), heads, heads, heads, heads,
      norm_g.reshape(1, DV_C), *([cache_k] * pps), *([cache_v] * pps))
    return y.reshape(bsz, t, H_C * DV_C)


def _mem_attn_kernel(q_ref, g_ref, mk_ref, mv_ref, o_ref, pad_scr, *, tq):
    rows = pad_scr.shape[0]
    if tq < rows:
        pad_scr[...] = jnp.zeros_like(pad_scr)
        pad_scr[0:tq, :] = q_ref[...]
        q_all = pad_scr[...]
    else:
        q_all = q_ref[...]
    nt = (((1,), (1,)), ((), ()))
    for h in range(H_M):
        cols = slice(h * DH_M, (h + 1) * DH_M)
        q = (q_all[:, cols] * (DH_M ** -0.5)).astype(BF16)
        s = lax.dot_general(q, mk_ref[:, cols].astype(BF16), nt, preferred_element_type=F32)
        e = jnp.exp(s - jnp.max(s, axis=-1, keepdims=True))
        l = jnp.sum(e, axis=-1, keepdims=True)
        o = jnp.dot(e.astype(BF16), mv_ref[:, cols].astype(BF16), preferred_element_type=F32) / l
        o_ref[:, cols] = (o[0:tq, :] * _silu(g_ref[:, cols])).astype(o_ref.dtype)


def mem_attn(z4, mem_k, mem_v, out_dtype):
    _, bsz, t, _ = z4.shape
    n_mem = mem_k.shape[1]
    tq = _row_tile(t, 256)
    rows = max(tq, 2 * SUBLANES)

    def tok(g):
        return pl.BlockSpec((None, None, tq, GROUP_W), lambda b, i: (g, b, i, 0))

    mem_spec = pl.BlockSpec((None, n_mem, GROUP_W), lambda b, i: (b, 0, 0))
    return pl.pallas_call(
        functools.partial(_mem_attn_kernel, tq=tq),
        grid=(bsz, t // tq),
        in_specs=[tok(G_QM), tok(G_GM), mem_spec, mem_spec],
        out_specs=pl.BlockSpec((None, tq, GROUP_W), lambda b, i: (b, i, 0)),
        out_shape=jax.ShapeDtypeStruct((bsz, t, GROUP_W), out_dtype),
        scratch_shapes=[pltpu.VMEM((rows, GROUP_W), F32)],
        compiler_params=_params("parallel", "parallel"),
        name="mem_attn",
    )(z4, z4, mem_k.reshape(bsz, n_mem, GROUP_W), mem_v.reshape(bsz, n_mem, GROUP_W))


def input_projection(x_p, x_s, lw):
    d = x_p.shape[-1]
    h_p = rmsnorm(x_p.reshape(-1, d), lw['norm_g'], BF16)
    h_s = rmsnorm(x_s.reshape(-1, d), lw['norm_g'], BF16)
    return matmul_groups(h_p, lw['w_in'], lw['layer'], a_small=h_s)


def trunk_layer(x, z, lw, attend, mem_k, mem_v, s5_h0, hgrn_s0, part_dtype):
    bsz, t, d = x.shape
    m = bsz * t
    x2 = x.reshape(m, d)
    z4 = z.reshape(N_GROUPS_IN, bsz, t, GROUP_W)

    if t % SUBLANES == 0:
        y_a, s5_last = s5_fused(z4, lw, s5_h0, part_dtype)
        y_a = y_a.reshape(m, GROUP_W)
    else:
        bu = s5_bu(z, lw['w_bu'])
        h_all, s5_last = s5_scan(bu, lw['a_bar'], s5_h0, bsz, t)
        y_a = s5_y(h_all, z, lw['wc'], lw['s5_d'], lw['w_glu'], lw['b_glu'], part_dtype)

    y_b, hgrn_s = hgrn(z4, lw['lb'], lw['hgrn_norm_g'], hgrn_s0, part_dtype)
    y_c = attend(z4)
    y_m = mem_attn(z4, mem_k, mem_v, part_dtype)

    parts = (y_a, y_b.reshape(m, GROUP_W), y_c.reshape(m, GROUP_W), y_m.reshape(m, GROUP_W))
    x_out = outproj(parts, lw['w_out'], lw['layer'], x2).reshape(bsz, t, d)
    return x_out, z4, s5_last, hgrn_s


def kernel(x_prompt, x_sample, cache_k, cache_v, cache_mem_k, cache_mem_v, state_s5_re, state_s5_im, state_hgrn, page_table, mem_prompt, norm_g, w_in, s5_lambda_re, s5_lambda_im, s5_log_dt, s5_b_re, s5_b_im, s5_c_re, s5_c_im, s5_d, s5_w_glu, s5_b_glu, hgrn_lower_bounds, hgrn_norm_g, diff_lq1, diff_lk1, diff_lq2, diff_lk2, diff_norm_g, mem_norm_g, w_mem_kv, w_out, final_norm_g):
    depth = w_in.shape[0]
    bsz_p, t_p, d = x_prompt.shape
    bsz_s, t_s, _ = x_sample.shape
    n_mem = mem_prompt.shape[1]
    lb_all = jnp.cumsum(jax.nn.softmax(hgrn_lower_bounds.astype(F32), axis=0), axis=0)
    lb_all = lb_all - lb_all[0]
    zeros_s5 = jnp.zeros((2, bsz_p, G_A, N_A), F32)
    zeros_hgrn = jnp.zeros((bsz_p, H_B, DK_B, DV_B), F32)

    lam_inits = [0.8 - 0.6 * math.exp(-0.3 * l) for l in range(depth)]
    lams = (jnp.exp(jnp.sum(diff_lq1.astype(F32) * diff_lk1.astype(F32), axis=-1))
            - jnp.exp(jnp.sum(diff_lq2.astype(F32) * diff_lk2.astype(F32), axis=-1))
            + jnp.asarray(lam_inits, F32))
    a_bars, w_bus, wcs = jax.vmap(s5_weights)(s5_lambda_re, s5_lambda_im, s5_log_dt,
                                              s5_b_re, s5_b_im, s5_c_re, s5_c_im)
    a_pows = jax.vmap(s5_scan_multipliers)(a_bars)
    w_glus = s5_w_glu.astype(BF16)

    xp, xs = x_prompt, x_sample
    outs = {k: [] for k in ('kp', 'vp', 'ks', 'vs', 'mkp', 'mvp', 's5p', 's5s', 'hgp', 'hgs')}
    for l in range(depth):
        lam_init = lam_inits[l]
        lam = lams[l]
        lw = {'layer': l, 'norm_g': norm_g[l], 'w_in': w_in, 'w_out': w_out,
              'a_bar': a_bars[l], 'a_pow': a_pows[l], 'w_bu': w_bus[l], 'wc': wcs[l], 's5_d': s5_d[l],
              'w_glu': w_glus[l], 'b_glu': s5_b_glu[l],
              'lb': lb_all[l], 'hgrn_norm_g': hgrn_norm_g[l]}
        z_p, z_s = input_projection(xp, xs, lw)

        hm = rmsnorm(mem_prompt.reshape(bsz_p * n_mem, d), mem_norm_g[l], BF16)
        mkv = matmul_groups(hm, w_mem_kv, l)
        mk_p = mkv[0].reshape(bsz_p, n_mem, H_M, DH_M)
        mv_p = mkv[1].reshape(bsz_p, n_mem, H_M, DH_M)
        attend_p = functools.partial(dattn_prompt, lam=lam, norm_g=diff_norm_g[l],
                                     lam_init=lam_init, out_dtype=BF16)
        xp, z4, s5_last, hg = trunk_layer(xp, z_p, lw, attend_p, mk_p, mv_p, zeros_s5, zeros_hgrn, BF16)
        outs['kp'].append(z4[G_KC].reshape(bsz_p, t_p, H_C, 2 * DH_C))
        outs['vp'].append(z4[G_VC].reshape(bsz_p, t_p, H_C, DV_C))
        outs['mkp'].append(mk_p)
        outs['mvp'].append(mv_p)
        outs['s5p'].append(s5_last)
        outs['hgp'].append(hg)

        attend_s = functools.partial(dattn_sample, cache_k=cache_k, cache_v=cache_v, layer=l,
                                     page_table=page_table, lam=lam, norm_g=diff_norm_g[l],
                                     lam_init=lam_init, out_dtype=F32)
        s5_h0 = jnp.stack([state_s5_re[l].astype(F32), state_s5_im[l].astype(F32)])
        xs, z4, s5_last, hg = trunk_layer(xs, z_s, lw, attend_s, cache_mem_k[l], cache_mem_v[l],
                                          s5_h0, state_hgrn[l], F32)
        outs['ks'].append(z4[G_KC].reshape(bsz_s, t_s, H_C, 2 * DH_C))
        outs['vs'].append(z4[G_VC].reshape(bsz_s, t_s, H_C, DV_C))
        outs['s5s'].append(s5_last)
        outs['hgs'].append(hg)

    y_prompt = rmsnorm(xp.reshape(bsz_p * t_p, d), final_norm_g, F32).reshape(bsz_p, t_p, d)
    y_sample = rmsnorm(xs.reshape(bsz_s * t_s, d), final_norm_g, F32).reshape(bsz_s, t_s, d)
    s5p = jnp.stack(outs['s5p'])
    s5s = jnp.stack(outs['s5s'])
    return (y_prompt, y_sample,
            jnp.stack(outs['kp']), jnp.stack(outs['vp']), jnp.stack(outs['ks']), jnp.stack(outs['vs']),
            jnp.stack(outs['mkp']), jnp.stack(outs['mvp']),
            s5p[:, 0], s5p[:, 1], s5s[:, 0], s5s[:, 1],
            jnp.stack(outs['hgp']), jnp.stack(outs['hgs']))
```

```python
import functools
import math

import jax
import jax.numpy as jnp
from jax import lax
from jax.experimental import pallas as pl
from jax.experimental.pallas import tpu as pltpu

F32 = jnp.float32
BF16 = jnp.bfloat16

D_MODEL = 4096
GROUP_W = 1024
N_GROUPS_IN = 12
C_A = 16
G_A = GROUP_W // C_A
N_A = 64
S5_GROUPS_PER_DOT = 16
S5_CHUNKS = G_A // S5_GROUPS_PER_DOT
S5_STATE = G_A * N_A
DK_B = 128
DV_B = 128
H_B = GROUP_W // DK_B
DV_C = 128
DH_C = DV_C // 2
H_C = GROUP_W // DV_C
H_M = 4
DH_M = GROUP_W // H_M
EPS = 1e-6
LOG2_E = math.log2(math.e)
HGRN_CHUNK = 64
HGRN_SUB = 16
SUBLANES = 8
LANES = 128
MXU_N = 256
VMEM_LIMIT = 60 * 1024 * 1024

(G_UA, G_GA, G_QB, G_FB, G_IB, G_GB, G_QC, G_KC, G_VC, G_GC, G_QM, G_GM) = range(N_GROUPS_IN)


def _params(*sem):
    return pltpu.CompilerParams(dimension_semantics=sem, vmem_limit_bytes=VMEM_LIMIT)


def _silu(x):
    return x * jax.nn.sigmoid(x)


def _row_tile(m, cap):
    return cap if m % cap == 0 else m


def _rmsnorm_kernel(x_ref, g_ref, o_ref):
    x = x_ref[...]
    y = x * lax.rsqrt(jnp.mean(x * x, axis=-1, keepdims=True) + EPS)
    o_ref[...] = (y * g_ref[...]).astype(o_ref.dtype)


def rmsnorm(x, g, out_dtype):
    m, d = x.shape
    tm = _row_tile(m, 256)
    return pl.pallas_call(
        _rmsnorm_kernel,
        grid=(m // tm,),
        in_specs=[pl.BlockSpec((tm, d), lambda i: (i, 0)),
                  pl.BlockSpec((1, d), lambda i: (0, 0))],
        out_specs=pl.BlockSpec((tm, d), lambda i: (i, 0)),
        out_shape=jax.ShapeDtypeStruct((m, d), out_dtype),
        compiler_params=_params("parallel"),
        name="rmsnorm",
    )(x, g.reshape(1, d))


def _matmul_kernel(*refs):
    if len(refs) == 3:
        a_ref, b_ref, o_ref = refs
        o_ref[...] = jnp.dot(a_ref[...], b_ref[...].astype(BF16), preferred_element_type=F32)
        return
    a_ref, a2_ref, b_ref, o_ref, o2_ref = refs
    w = b_ref[...].astype(BF16)
    o_ref[...] = jnp.dot(a_ref[...], w, preferred_element_type=F32)
    first_pass = pl.program_id(0) == 0

    @pl.when(first_pass)
    def _():
        o2_ref[...] = jnp.dot(a2_ref[...], w, preferred_element_type=F32)

    @pl.when(jnp.logical_not(first_pass))
    def _():
        o2_ref[...] = jnp.zeros_like(o2_ref)


def matmul_groups(a, b, layer, a_small=None):
    m, kd = a.shape
    n = b.shape[2]
    tm = _row_tile(m, 2048)
    tn = MXU_N
    per_group = GROUP_W // tn
    lhs = [a] if a_small is None else [a, a_small]
    in_specs = [pl.BlockSpec((tm, kd), lambda i, j: (i, 0))]
    out_specs = [pl.BlockSpec((None, tm, tn), lambda i, j: (j // per_group, i, j % per_group))]
    out_shape = [jax.ShapeDtypeStruct((n // GROUP_W, m, GROUP_W), F32)]
    if a_small is not None:
        m2 = a_small.shape[0]
        in_specs.append(pl.BlockSpec((m2, kd), lambda i, j: (0, 0)))
        out_specs.append(pl.BlockSpec((None, None, m2, tn),
                                      lambda i, j: (i, j // per_group, 0, j % per_group)))
        out_shape.append(jax.ShapeDtypeStruct((m // tm, n // GROUP_W, m2, GROUP_W), F32))
    in_specs.append(pl.BlockSpec((None, kd, tn), lambda i, j: (layer, 0, j)))
    outs = pl.pallas_call(
        _matmul_kernel,
        grid=(m // tm, n // tn),
        in_specs=in_specs,
        out_specs=out_specs,
        out_shape=out_shape,
        compiler_params=_params("parallel", "arbitrary"),
        name="matmul_groups",
    )(*lhs, b)
    return outs[0] if a_small is None else (outs[0], outs[1][0])


def _outproj_kernel(a0_ref, a1_ref, a2_ref, a3_ref, w_ref, x_ref, o_ref):
    acc = x_ref[...]
    for g, a_ref in enumerate((a0_ref, a1_ref, a2_ref, a3_ref)):
        acc += jnp.dot(a_ref[...].astype(BF16), w_ref[g * GROUP_W:(g + 1) * GROUP_W, :].astype(BF16),
                       preferred_element_type=F32)
    o_ref[...] = acc


def outproj(parts, w, layer, x):
    m, d = x.shape
    tm = _row_tile(m, 2048)
    tn = MXU_N
    part_spec = pl.BlockSpec((tm, GROUP_W), lambda i, j: (i, 0))
    return pl.pallas_call(
        _outproj_kernel,
        grid=(m // tm, d // tn),
        in_specs=[part_spec, part_spec, part_spec, part_spec,
                  pl.BlockSpec((None, 4 * GROUP_W, tn), lambda i, j: (layer, 0, j)),
                  pl.BlockSpec((tm, tn), lambda i, j: (i, j))],
        out_specs=pl.BlockSpec((tm, tn), lambda i, j: (i, j)),
        out_shape=jax.ShapeDtypeStruct((m, d), F32),
        compiler_params=_params("parallel", "arbitrary"),
        name="outproj",
    )(*parts, w, x)


def _s5_bu_kernel(u_ref, w_ref, o_ref):
    r = jnp.dot(u_ref[...].astype(BF16), w_ref[...], preferred_element_type=F32)
    half = r.shape[1] // 2
    o_ref[0] = r[:, :half]
    o_ref[1] = r[:, half:]


def s5_bu(z, w_bu):
    m = z.shape[1]
    tm = _row_tile(m, 512)
    kw = S5_GROUPS_PER_DOT * C_A
    nw = S5_GROUPS_PER_DOT * N_A
    return pl.pallas_call(
        _s5_bu_kernel,
        grid=(m // tm, S5_CHUNKS),
        in_specs=[pl.BlockSpec((None, tm, kw), lambda i, k: (G_UA, i, k)),
                  pl.BlockSpec((None, kw, 2 * nw), lambda i, k: (k, 0, 0))],
        out_specs=pl.BlockSpec((2, tm, nw), lambda i, k: (0, i, k)),
        out_shape=jax.ShapeDtypeStruct((2, m, S5_STATE), F32),
        compiler_params=_params("parallel", "parallel"),
        name="s5_bu",
    )(z, w_bu)


def _s5_scan_kernel(bu_ref, a_ref, h0_ref, h_ref, hl_ref, st_ref, *, tt):
    j = pl.program_id(1)

    @pl.when(j == 0)
    def _():
        st_ref[...] = h0_ref[...]

    a_re = a_ref[0]
    a_im = a_ref[1]

    def body(t, carry):
        h_re, h_im = carry
        n_re = a_re * h_re - a_im * h_im + bu_ref[0, t]
        n_im = a_re * h_im + a_im * h_re + bu_ref[1, t]
        h_ref[0, t] = n_re
        h_ref[1, t] = n_im
        return n_re, n_im

    h_re, h_im = lax.fori_loop(0, tt, body, (st_ref[0], st_ref[1]), unroll=min(tt, 8))
    st_ref[0] = h_re
    st_ref[1] = h_im

    @pl.when(j == pl.num_programs(1) - 1)
    def _():
        hl_ref[0] = h_re
        hl_ref[1] = h_im


def s5_scan(bu, a_bar, h0, bsz, t):
    rows = S5_STATE // LANES
    bu5 = bu.reshape(2, bsz, t, rows, LANES)
    tt = _row_tile(t, 128)
    h, h_last = pl.pallas_call(
        functools.partial(_s5_scan_kernel, tt=tt),
        grid=(bsz, t // tt),
        in_specs=[pl.BlockSpec((2, None, tt, rows, LANES), lambda b, j: (0, b, j, 0, 0)),
                  pl.BlockSpec((2, rows, LANES), lambda b, j: (0, 0, 0)),
                  pl.BlockSpec((2, None, rows, LANES), lambda b, j: (0, b, 0, 0))],
        out_specs=[pl.BlockSpec((2, None, tt, rows, LANES), lambda b, j: (0, b, j, 0, 0)),
                   pl.BlockSpec((2, None, rows, LANES), lambda b, j: (0, b, 0, 0))],
        out_shape=[jax.ShapeDtypeStruct((2, bsz, t, rows, LANES), F32),
                   jax.ShapeDtypeStruct((2, bsz, rows, LANES), F32)],
        scratch_shapes=[pltpu.VMEM((2, rows, LANES), F32)],
        compiler_params=_params("parallel", "arbitrary"),
        name="s5_scan",
    )(bu5, a_bar.reshape(2, rows, LANES), h0.reshape(2, bsz, rows, LANES))
    return h.reshape(2, bsz * t, S5_STATE), h_last.reshape(2, bsz, G_A, N_A)


def _s5_y_kernel(h_ref, u_ref, g_ref, wc_ref, d_ref, wg_ref, bg_ref, o_ref):
    nw = S5_GROUPS_PER_DOT * N_A
    parts = []
    for k in range(S5_CHUNKS):
        h_re = h_ref[0, :, k * nw:(k + 1) * nw].astype(BF16)
        h_im = h_ref[1, :, k * nw:(k + 1) * nw].astype(BF16)
        parts.append(jnp.dot(h_re, wc_ref[0, k], preferred_element_type=F32)
                     + jnp.dot(h_im, wc_ref[1, k], preferred_element_type=F32))
    y = jnp.concatenate(parts, axis=-1) + d_ref[...] * u_ref[...]
    y = jax.nn.gelu(y)
    glu = jnp.dot(y.astype(BF16), wg_ref[...], preferred_element_type=F32) + bg_ref[...]
    y = y * jax.nn.sigmoid(glu)
    o_ref[...] = (y * _silu(g_ref[...])).astype(o_ref.dtype)


def s5_y(h, z, wc, d, w_glu, b_glu, out_dtype):
    m = z.shape[1]
    tm = _row_tile(m, 256)
    kw = S5_GROUPS_PER_DOT * C_A
    nw = S5_GROUPS_PER_DOT * N_A
    return pl.pallas_call(
        _s5_y_kernel,
        grid=(m // tm,),
        in_specs=[pl.BlockSpec((2, tm, S5_STATE), lambda i: (0, i, 0)),
                  pl.BlockSpec((None, tm, GROUP_W), lambda i: (G_UA, i, 0)),
                  pl.BlockSpec((None, tm, GROUP_W), lambda i: (G_GA, i, 0)),
                  pl.BlockSpec((2, S5_CHUNKS, nw, kw), lambda i: (0, 0, 0, 0)),
                  pl.BlockSpec((1, GROUP_W), lambda i: (0, 0)),
                  pl.BlockSpec((GROUP_W, GROUP_W), lambda i: (0, 0)),
                  pl.BlockSpec((1, GROUP_W), lambda i: (0, 0))],
        out_specs=pl.BlockSpec((tm, GROUP_W), lambda i: (i, 0)),
        out_shape=jax.ShapeDtypeStruct((m, GROUP_W), out_dtype),
        compiler_params=_params("parallel"),
        name="s5_y",
    )(h, z, z, wc, d.reshape(1, GROUP_W), w_glu, b_glu.reshape(1, GROUP_W))


def _s5_fused_kernel(u_ref, g_ref, wbu_ref, pw_ref, h0_ref, wc_ref, d_ref, wg_ref, bg_ref,
                     o_ref, hl_ref, st_ref, h_scr, *, tt):
    j = pl.program_id(1)
    nw = S5_GROUPS_PER_DOT * N_A
    kw = S5_GROUPS_PER_DOT * C_A
    n_tiles = tt // SUBLANES

    @pl.when(j == 0)
    def _():
        st_ref[...] = h0_ref[...]

    def axpy(x_re, x_im, a_re, a_im, s_re, s_im):
        return x_re + (a_re * s_re - a_im * s_im), x_im + (a_re * s_im + a_im * s_re)

    u = u_ref[...]
    y_parts = []
    for k in range(S5_CHUNKS):
        cols = slice(k * nw, (k + 1) * nw)
        bu = jnp.dot(u[:, k * kw:(k + 1) * kw].astype(BF16), wbu_ref[k], preferred_element_type=F32)
        h_scr[0] = bu[:, :nw].reshape(n_tiles, SUBLANES, nw)
        h_scr[1] = bu[:, nw:].reshape(n_tiles, SUBLANES, nw)

        def tile_step(g, carry, cols=cols):
            c_re, c_im = carry
            x_re, x_im = h_scr[0, g], h_scr[1, g]
            for p, shift in enumerate((1, 2, 4)):
                x_re, x_im = axpy(x_re, x_im, pw_ref[0, p, :, cols], pw_ref[1, p, :, cols],
                                  pltpu.roll(x_re, shift, axis=0), pltpu.roll(x_im, shift, axis=0))
            x_re, x_im = axpy(x_re, x_im, pw_ref[0, 3, :, cols], pw_ref[1, 3, :, cols], c_re, c_im)
            h_scr[0, g] = x_re
            h_scr[1, g] = x_im
            return x_re[SUBLANES - 1:SUBLANES, :], x_im[SUBLANES - 1:SUBLANES, :]

        c_re, c_im = lax.fori_loop(0, n_tiles, tile_step, (st_ref[0, :, cols], st_ref[1, :, cols]),
                                   unroll=True)
        st_ref[0, :, cols] = c_re
        st_ref[1, :, cols] = c_im
        h_re = h_scr[0].reshape(tt, nw).astype(BF16)
        h_im = h_scr[1].reshape(tt, nw).astype(BF16)
        y_parts.append(jnp.dot(h_re, wc_ref[0, k], preferred_element_type=F32)
                       + jnp.dot(h_im, wc_ref[1, k], preferred_element_type=F32))
    y = jnp.concatenate(y_parts, axis=-1) + d_ref[...] * u
    y = jax.nn.gelu(y)
    glu = jnp.dot(y.astype(BF16), wg_ref[...], preferred_element_type=F32) + bg_ref[...]
    y = y * jax.nn.sigmoid(glu)
    o_ref[...] = (y * _silu(g_ref[...])).astype(o_ref.dtype)

    @pl.when(j == pl.num_programs(1) - 1)
    def _():
        hl_ref[...] = st_ref[...]


def s5_fused(z4, lw, h0, out_dtype):
    _, bsz, t, _ = z4.shape
    tt = _row_tile(t, 512)
    kw = S5_GROUPS_PER_DOT * C_A
    nw = S5_GROUPS_PER_DOT * N_A

    def tok(g):
        return pl.BlockSpec((None, None, tt, GROUP_W), lambda b, j: (g, b, j, 0))

    def whole(shape):
        return pl.BlockSpec(shape, lambda b, j: (0,) * len(shape))

    state_spec = pl.BlockSpec((2, None, 1, S5_STATE), lambda b, j: (0, b, 0, 0))
    y, h_last = pl.pallas_call(
        functools.partial(_s5_fused_kernel, tt=tt),
        grid=(bsz, t // tt),
        in_specs=[tok(G_UA), tok(G_GA),
                  whole((S5_CHUNKS, kw, 2 * nw)),
                  whole((2, 4, SUBLANES, S5_STATE)),
                  state_spec,
                  whole((2, S5_CHUNKS, nw, kw)),
                  whole((1, GROUP_W)), whole((GROUP_W, GROUP_W)), whole((1, GROUP_W))],
        out_specs=[pl.BlockSpec((None, tt, GROUP_W), lambda b, j: (b, j, 0)), state_spec],
        out_shape=[jax.ShapeDtypeStruct((bsz, t, GROUP_W), out_dtype),
                   jax.ShapeDtypeStruct((2, bsz, 1, S5_STATE), F32)],
        scratch_shapes=[pltpu.VMEM((2, 1, S5_STATE), F32),
                        pltpu.VMEM((2, tt // SUBLANES, SUBLANES, nw), F32)],
        compiler_params=_params("parallel", "arbitrary"),
        name="s5_fused",
    )(z4, z4, lw['w_bu'], lw['a_pow'], h0.reshape(2, bsz, 1, S5_STATE), lw['wc'],
      lw['s5_d'].reshape(1, GROUP_W), lw['w_glu'], lw['b_glu'].reshape(1, GROUP_W))
    return y, h_last.reshape(2, bsz, G_A, N_A)


def s5_scan_multipliers(a_pair):
    a = lax.complex(a_pair[0], a_pair[1]).reshape(S5_STATE)
    powers = [a]
    for _ in range(SUBLANES - 1):
        powers.append(powers[-1] * a)
    row = jnp.arange(SUBLANES)[:, None]
    planes = [jnp.where(row >= s, powers[s - 1][None, :], 0.0) for s in (1, 2, 4)]
    planes.append(jnp.stack(powers))
    pw = jnp.stack(planes)
    return jnp.stack([pw.real, pw.imag])


def s5_weights(lam_re, lam_im, log_dt, b_re, b_im, c_re, c_im):
    lam = lax.complex(lam_re.astype(F32), lam_im.astype(F32))
    dt = jnp.exp(log_dt.astype(F32))[:, None]
    a_bar = jnp.exp(lam * dt)
    b_bar = ((a_bar - 1.0) / lam)[..., None] * lax.complex(b_re.astype(F32), b_im.astype(F32))
    gpd, kw, nw = S5_GROUPS_PER_DOT, S5_GROUPS_PER_DOT * C_A, S5_GROUPS_PER_DOT * N_A
    bb = jnp.stack([b_bar.real, b_bar.imag]).reshape(2, S5_CHUNKS, gpd, N_A, C_A)
    b_cols = bb.transpose(1, 4, 0, 2, 3).reshape(S5_CHUNKS, C_A, 2 * nw)
    row_g = jnp.arange(kw)[:, None] // C_A
    col_h = (jnp.arange(2 * nw)[None, :] % nw) // N_A
    w_bu = jnp.where(row_g == col_h, jnp.tile(b_cols, (1, gpd, 1)), 0.0).astype(BF16)
    cc = jnp.stack([c_re.astype(F32), -c_im.astype(F32)]).reshape(2, S5_CHUNKS, gpd, C_A, N_A)
    c_rows = cc.transpose(0, 1, 2, 4, 3).reshape(2, S5_CHUNKS, nw, C_A)
    wc = jnp.where(jnp.arange(nw)[:, None] // N_A == jnp.arange(kw)[None, :] // C_A,
                   jnp.tile(c_rows, (1, 1, 1, gpd)), 0.0).astype(BF16)
    a_pair = jnp.stack([a_bar.real, a_bar.imag])
    return a_pair, w_bu, wc


def _hgrn_kernel(q_ref, f_ref, i_ref, g_ref, lb_ref, ng_ref, s0_ref, y_ref, sl_ref,
                 st_ref, pad_scr, *, tc, chunk, t_valid):
    sub = HGRN_SUB
    n_sb = chunk // sub
    j = pl.program_id(2)
    nt = (((1,), (1,)), ((), ()))

    @pl.when(j == 0)
    def _():
        st_ref[...] = s0_ref[...].T

    lb = lb_ref[...]
    padded = tc < chunk
    if padded:
        pad_scr[...] = jnp.zeros_like(pad_scr)
        pad_scr[0, 0:tc, :] = q_ref[...]
        pad_scr[1, 0:tc, :] = f_ref[...]
        pad_scr[2, 0:tc, :] = i_ref[...]
    row_c = lax.broadcasted_iota(jnp.int32, (chunk, LANES), 0)
    row_t = lax.broadcasted_iota(jnp.int32, (sub, chunk), 0)
    lane_s = lax.broadcasted_iota(jnp.int32, (sub, chunk), 1)
    lane_8 = lax.broadcasted_iota(jnp.int32, (SUBLANES, chunk), 1)
    tri = (lax.broadcasted_iota(jnp.int32, (chunk, chunk), 1)
           <= lax.broadcasted_iota(jnp.int32, (chunk, chunk), 0)).astype(F32)

    def gates(ci):
        rows = slice(ci * chunk, (ci + 1) * chunk)
        if padded:
            q, f_pre, inp = pad_scr[0], pad_scr[1], pad_scr[2]
        else:
            q, f_pre, inp = q_ref[rows, :], f_ref[rows, :], i_ref[rows, :]
        f = lb + (1.0 - lb) * jax.nn.sigmoid(f_pre)
        log_f = jnp.log2(f)
        kk = 1.0 - f
        if padded:
            log_f = jnp.where(row_c < t_valid, log_f, 0.0)
            kk = jnp.where(row_c < t_valid, kk, 0.0)
        cum = jnp.dot(tri, log_f, precision=lax.Precision.HIGHEST, preferred_element_type=F32)
        return q, kk, cum, inp.astype(BF16)

    def intra_scores(q, kk, cum):
        blocks = []
        for i in range(n_sb):
            rows = slice(i * sub, (i + 1) * sub)
            q_i, kk_i, cum_i = q[rows], kk[rows], cum[rows]
            if i > 0:
                edge = cum[i * sub - 1:i * sub, :]
                a_i = q_i * jnp.exp2(cum_i - edge)
                kt_i = kk * jnp.exp2(jnp.where(row_c < i * sub, edge - cum, -jnp.inf))
                sc = lax.dot_general(a_i.astype(BF16), kt_i.astype(BF16), nt, preferred_element_type=F32)
            else:
                sc = jnp.zeros((sub, chunk), F32)
            tiles = [sc[r:r + SUBLANES] for r in range(0, sub, SUBLANES)]
            for s in range(min(sub, t_valid)):
                first = s // SUBLANES
                live = slice(first * SUBLANES, sub)
                w = (q_i[live] * kk_i[s:s + 1, :]) * jnp.exp2(cum_i[live] - cum_i[s:s + 1, :])
                col = jnp.sum(w, axis=-1, keepdims=True)
                for r in range(first, sub // SUBLANES):
                    piece = col[(r - first) * SUBLANES:(r - first + 1) * SUBLANES]
                    tiles[r] = jnp.where(lane_8 == i * sub + s, piece, tiles[r])
            sc = jnp.concatenate(tiles, axis=0)
            blocks.append(jnp.where(lane_s - i * sub > row_t, 0.0, sc))
        scores = blocks[0] if n_sb == 1 else jnp.concatenate(blocks, axis=0)
        return scores.astype(BF16)

    def advance(ci, q, kk, cum, inp_b, scores):
        rows = slice(ci * chunk, (ci + 1) * chunk)
        last = cum[chunk - 1:chunk, :]
        st = st_ref[...]
        o = lax.dot_general((q * jnp.exp2(cum)).astype(BF16), st.astype(BF16), nt,
                            preferred_element_type=F32)
        o = o + jnp.dot(scores, inp_b, preferred_element_type=F32)
        kt = (kk * jnp.exp2(last - cum)).astype(BF16)
        upd = lax.dot_general(inp_b, kt, (((0,), (0,)), ((), ())), preferred_element_type=F32)
        st_ref[...] = st * jnp.exp2(last) + upd
        y = o * lax.rsqrt(jnp.mean(o * o, axis=-1, keepdims=True) + EPS) * ng_ref[...]
        if padded:
            y_ref[...] = (y[0:tc] * _silu(g_ref[...])).astype(y_ref.dtype)
        else:
            y_ref[rows, :] = (y * _silu(g_ref[rows, :])).astype(y_ref.dtype)

    n_chunks = max(tc // chunk, 1)
    staged = [gates(ci) for ci in range(n_chunks)]
    scored = [intra_scores(q, kk, cum) for q, kk, cum, _ in staged]
    for ci in range(n_chunks):
        advance(ci, *staged[ci], scored[ci])

    @pl.when(j == pl.num_programs(2) - 1)
    def _():
        sl_ref[...] = st_ref[...].T


def hgrn(z4, lb, norm_g, s0, out_dtype):
    _, bsz, t, _ = z4.shape
    chunk = HGRN_CHUNK if t % HGRN_CHUNK == 0 else HGRN_SUB
    tc = _row_tile(t, 512) if t >= chunk else t
    t_valid = min(chunk, tc)

    def col(g):
        return pl.BlockSpec((None, None, tc, DK_B), lambda b, h, j: (g, b, j, h))

    state_spec = pl.BlockSpec((None, None, DK_B, DV_B), lambda b, h, j: (b, h, 0, 0))
    return pl.pallas_call(
        functools.partial(_hgrn_kernel, tc=tc, chunk=chunk, t_valid=t_valid),
        grid=(bsz, H_B, t // tc),
        in_specs=[col(G_QB), col(G_FB), col(G_IB), col(G_GB),
                  pl.BlockSpec((1, DK_B), lambda b, h, j: (0, h)),
                  pl.BlockSpec((1, DV_B), lambda b, h, j: (0, 0)),
                  state_spec],
        out_specs=[pl.BlockSpec((None, tc, DV_B), lambda b, h, j: (b, j, h)), state_spec],
        out_shape=[jax.ShapeDtypeStruct((bsz, t, GROUP_W), out_dtype),
                   jax.ShapeDtypeStruct((bsz, H_B, DK_B, DV_B), F32)],
        scratch_shapes=[pltpu.VMEM((DV_B, DK_B), F32),
                        pltpu.VMEM((3, chunk, LANES), F32)],
        compiler_params=_params("parallel", "parallel", "arbitrary"),
        name="hgrn",
    )(z4, z4, z4, z4, lb.reshape(1, GROUP_W), norm_g.reshape(1, DV_B), s0)


def _diff_norm_gate(o, ng, gate, out_scale):
    y = o * lax.rsqrt(jnp.mean(o * o, axis=-1, keepdims=True) + EPS) * ng
    return (y * out_scale) * _silu(gate)


def _dattn_prompt_kernel(lam_ref, q_ref, k_ref, v_ref, g_ref, ng_ref, o_ref,
                         kb_scr, vb_scr, *, tq, n_blk, out_scale):
    i = pl.program_id(2)

    @pl.when(i == 0)
    def _():
        kb_scr[...] = k_ref[...].astype(BF16)
        vb_scr[...] = v_ref[...].astype(BF16)

    nt = (((1,), (1,)), ((), ()))

    def attend(n_past):
        past = n_past * tq
        q = q_ref[...] * (DH_C ** -0.5 * LOG2_E)
        lane = lax.broadcasted_iota(jnp.int32, (tq, DV_C), 1)
        causal = (lax.broadcasted_iota(jnp.int32, (tq, tq), 1)
                  <= lax.broadcasted_iota(jnp.int32, (tq, tq), 0))
        k_diag = kb_scr[past:past + tq, :]
        v_diag = vb_scr[past:past + tq, :]

        def scores(qj):
            s_d = jnp.where(causal, lax.dot_general(qj, k_diag, nt, preferred_element_type=F32), -jnp.inf)
            m = jnp.max(s_d, axis=-1, keepdims=True)
            s_p = None
            if n_past:
                s_p = lax.dot_general(qj, kb_scr[0:past, :], nt, preferred_element_type=F32)
                m = jnp.maximum(m, jnp.max(s_p, axis=-1, keepdims=True))
            return s_d, s_p, m

        def weights(s_d, s_p, m):
            e_d = jnp.exp2(s_d - m)
            l = jnp.sum(e_d, axis=-1, keepdims=True)
            e_p = None
            if n_past:
                e_p = jnp.exp2(s_p - m)
                l = l + jnp.sum(e_p, axis=-1, keepdims=True)
                e_p = e_p.astype(BF16)
            return e_d.astype(BF16), e_p, l

        def attend_values(e_d, e_p, l):
            pv = jnp.dot(e_d, v_diag, preferred_element_type=F32)
            if n_past:
                pv = pv + jnp.dot(e_p, vb_scr[0:past, :], preferred_element_type=F32)
            return pv / l

        staged = [scores(jnp.where(lane < DH_C, q, 0.0).astype(BF16)),
                  scores(jnp.where(lane >= DH_C, q, 0.0).astype(BF16))]
        staged = [weights(*st) for st in staged]
        o = attend_values(*staged[0]) - lam_ref[...] * attend_values(*staged[1])
        o_ref[...] = _diff_norm_gate(o, ng_ref[...], g_ref[...], out_scale).astype(o_ref.dtype)

    for n_past in range(n_blk):
        pl.when(i == n_past)(functools.partial(attend, n_past))


def dattn_prompt(z4, lam, norm_g, lam_init, out_dtype):
    _, bsz, t, _ = z4.shape
    tq = _row_tile(t, 512)

    def q_col(g):
        return pl.BlockSpec((None, None, tq, DV_C), lambda b, h, i: (g, b, i, h))

    def kv_col(g):
        return pl.BlockSpec((None, None, t, DV_C), lambda b, h, i: (g, b, 0, h))

    vec = pl.BlockSpec((1, DV_C), lambda b, h, i: (0, 0))
    return pl.pallas_call(
        functools.partial(_dattn_prompt_kernel, tq=tq, n_blk=t // tq, out_scale=1.0 - lam_init),
        grid=(bsz, H_C, t // tq),
        in_specs=[vec, q_col(G_QC), kv_col(G_KC), kv_col(G_VC), q_col(G_GC), vec],
        out_specs=pl.BlockSpec((None, tq, DV_C), lambda b, h, i: (b, i, h)),
        out_shape=jax.ShapeDtypeStruct((bsz, t, GROUP_W), out_dtype),
        scratch_shapes=[pltpu.VMEM((t, DV_C), BF16),
                        pltpu.VMEM((t, DV_C), BF16)],
        compiler_params=_params("parallel", "parallel", "arbitrary"),
        name="dattn_prompt",
    )(jnp.full((1, DV_C), lam, F32), z4, z4, z4, z4, norm_g.reshape(1, DV_C))


def _dattn_sample_kernel(pt_ref, lam_ref, q_ref, kn_ref, vn_ref, g_ref, ng_ref, *rest,
                         pps, t, out_scale):
    k_refs = rest[:pps]
    v_refs = rest[pps:2 * pps]
    o_ref = rest[2 * pps]
    qm_scr, m_scr, l_scr, acc_scr, kn_scr, vn_scr = rest[2 * pps + 1:]
    del pt_ref
    j = pl.program_id(1)
    half = t * H_C
    n_rows = 2 * half
    page_rows = k_refs[0].shape[0] * H_C
    nt = (((1,), (1,)), ((), ()))

    @pl.when(j == 0)
    def _():
        q = q_ref[...].reshape(half, DV_C) * (DH_C ** -0.5)
        lane = lax.broadcasted_iota(jnp.int32, (half, DV_C), 1)
        qm_scr[0:half, :] = jnp.where(lane < DH_C, q, 0.0).astype(BF16)
        qm_scr[half:n_rows, :] = jnp.where(lane >= DH_C, q, 0.0).astype(BF16)
        m_scr[...] = jnp.full_like(m_scr, -jnp.inf)
        l_scr[...] = jnp.zeros_like(l_scr)
        acc_scr[...] = jnp.zeros_like(acc_scr)

    def online_update(s, v_blocks, width):
        m_old = m_scr[...]
        m_new = jnp.maximum(m_old, jnp.max(s, axis=-1, keepdims=True))
        alpha = jnp.exp(m_old - m_new)
        e = jnp.exp(s - m_new)
        l_scr[...] = alpha * l_scr[...] + jnp.sum(e, axis=-1, keepdims=True)
        pv = None
        for r, vb in enumerate(v_blocks):
            d = jnp.dot(e[:, r * width:(r + 1) * width].astype(BF16), vb, preferred_element_type=F32)
            pv = d if pv is None else pv + d
        acc_scr[...] = alpha * acc_scr[...] + pv
        m_scr[...] = m_new

    qm = qm_scr[...]
    same_head = (lax.broadcasted_iota(jnp.int32, (n_rows, page_rows), 0) % H_C
                 == lax.broadcasted_iota(jnp.int32, (n_rows, page_rows), 1) % H_C)
    s_parts = []
    for r in range(pps):
        kp = k_refs[r][...].reshape(page_rows, DV_C).astype(BF16)
        s_parts.append(jnp.where(same_head, lax.dot_general(qm, kp, nt, preferred_element_type=F32),
                                 -jnp.inf))
    online_update(jnp.concatenate(s_parts, axis=-1),
                  [v_refs[r][...].reshape(page_rows, DV_C).astype(BF16) for r in range(pps)], page_rows)

    @pl.when(j == pl.num_programs(1) - 1)
    def _():
        kn_scr[...] = jnp.zeros_like(kn_scr)
        vn_scr[...] = jnp.zeros_like(vn_scr)
        kn_scr[0:half, :] = kn_ref[...].reshape(half, DV_C)
        vn_scr[0:half, :] = vn_ref[...].reshape(half, DV_C)
        s_new = lax.dot_general(qm, kn_scr[...].astype(BF16), nt, preferred_element_type=F32)
        row = lax.broadcasted_iota(jnp.int32, (n_rows, LANES), 0)
        col = lax.broadcasted_iota(jnp.int32, (n_rows, LANES), 1)
        visible = (col % H_C == row % H_C) & (col // H_C <= (row // H_C) % t)
        online_update(jnp.where(visible, s_new, -jnp.inf), [vn_scr[...].astype(BF16)], LANES)

        o = (acc_scr[0:half, :] / l_scr[0:half, :]
             - lam_ref[...] * (acc_scr[half:n_rows, :] / l_scr[half:n_rows, :]))
        y = _diff_norm_gate(o, ng_ref[...], g_ref[...].reshape(half, DV_C), out_scale)
        o_ref[...] = y.reshape(t, H_C, DV_C).astype(o_ref.dtype)


def dattn_sample(z4, cache_k, cache_v, layer, page_table, lam, norm_g, lam_init, out_dtype):
    _, bsz, t, _ = z4.shape
    page = cache_k.shape[2]
    n_pages = page_table.shape[1]
    pps = 16 if n_pages % 16 == 0 else 1
    heads = z4[G_QC:G_GC + 1].reshape(4, bsz, t, H_C, DV_C)

    def tok(g):
        return pl.BlockSpec((None, None, t, H_C, DV_C), lambda b, j, pt: (g, b, 0, 0, 0))

    def page_spec(r):
        return pl.BlockSpec((None, None, page, H_C, DV_C),
                            lambda b, j, pt: (layer, pt[b * n_pages + j * pps + r], 0, 0, 0))

    vec = pl.BlockSpec((1, DV_C), lambda b, j, pt: (0, 0))
    n_rows = 2 * t * H_C
    grid_spec = pltpu.PrefetchScalarGridSpec(
        num_scalar_prefetch=1,
        grid=(bsz, n_pages // pps),
        in_specs=[vec, tok(0), tok(1), tok(2), tok(3), vec]
        + [page_spec(r) for r in range(pps)] + [page_spec(r) for r in range(pps)],
        out_specs=pl.BlockSpec((None, t, H_C, DV_C), lambda b, j, pt: (b, 0, 0, 0)),
        scratch_shapes=[pltpu.VMEM((n_rows, DV_C), BF16),
                        pltpu.VMEM((n_rows, 1), F32),
                        pltpu.VMEM((n_rows, 1), F32),
                        pltpu.VMEM((n_rows, DV_C), F32),
                        pltpu.VMEM((LANES, DV_C), F32),
                        pltpu.VMEM((LANES, DV_C), F32)],
    )
    y = pl.pallas_call(
        functools.partial(_dattn_sample_kernel, pps=pps, t=t, out_scale=1.0 - lam_init),
        grid_spec=grid_spec,
        out_shape=jax.ShapeDtypeStruct((bsz, t, H_C, DV_C), out_dtype),
        compiler_params=_params("parallel", "arbitrary"),
        name="dattn_sample",
    )(page_table.reshape(-1), jnp.full((1, DV_C), lam, F32), heads, heads, heads, heads,
      norm_g.reshape(1, DV_C), *([cache_k] * pps), *([cache_v] * pps))
    return y.reshape(bsz, t, H_C * DV_C)


def _mem_attn_kernel(q_ref, g_ref, mk_ref, mv_ref, o_ref, pad_scr, *, tq):
    rows = pad_scr.shape[0]
    if tq < rows:
        pad_scr[...] = jnp.zeros_like(pad_scr)
        pad_scr[0:tq, :] = q_ref[...]
        q_all = pad_scr[...]
    else:
        q_all = q_ref[...]
    nt = (((1,), (1,)), ((), ()))
    for h in range(H_M):
        cols = slice(h * DH_M, (h + 1) * DH_M)
        q = (q_all[:, cols] * (DH_M ** -0.5)).astype(BF16)
        s = lax.dot_general(q, mk_ref[:, cols].astype(BF16), nt, preferred_element_type=F32)
        e = jnp.exp(s - jnp.max(s, axis=-1, keepdims=True))
        l = jnp.sum(e, axis=-1, keepdims=True)
        o = jnp.dot(e.astype(BF16), mv_ref[:, cols].astype(BF16), preferred_element_type=F32) / l
        o_ref[:, cols] = (o[0:tq, :] * _silu(g_ref[:, cols])).astype(o_ref.dtype)


def mem_attn(z4, mem_k, mem_v, out_dtype):
    _, bsz, t, _ = z4.shape
    n_mem = mem_k.shape[1]
    tq = _row_tile(t, 256)
    rows = max(tq, 2 * SUBLANES)

    def tok(g):
        return pl.BlockSpec((None, None, tq, GROUP_W), lambda b, i: (g, b, i, 0))

    mem_spec = pl.BlockSpec((None, n_mem, GROUP_W), lambda b, i: (b, 0, 0))
    return pl.pallas_call(
        functools.partial(_mem_attn_kernel, tq=tq),
        grid=(bsz, t // tq),
        in_specs=[tok(G_QM), tok(G_GM), mem_spec, mem_spec],
        out_specs=pl.BlockSpec((None, tq, GROUP_W), lambda b, i: (b, i, 0)),
        out_shape=jax.ShapeDtypeStruct((bsz, t, GROUP_W), out_dtype),
        scratch_shapes=[pltpu.VMEM((rows, GROUP_W), F32)],
        compiler_params=_params("parallel", "parallel"),
        name="mem_attn",
    )(z4, z4, mem_k.reshape(bsz, n_mem, GROUP_W), mem_v.reshape(bsz, n_mem, GROUP_W))


def input_projection(x_p, x_s, lw):
    d = x_p.shape[-1]
    h_p = rmsnorm(x_p.reshape(-1, d), lw['norm_g'], BF16)
    h_s = rmsnorm(x_s.reshape(-1, d), lw['norm_g'], BF16)
    return matmul_groups(h_p, lw['w_in'], lw['layer'], a_small=h_s)


def trunk_layer(x, z, lw, attend, mem_k, mem_v, s5_h0, hgrn_s0, part_dtype):
    bsz, t, d = x.shape
    m = bsz * t
    x2 = x.reshape(m, d)
    z4 = z.reshape(N_GROUPS_IN, bsz, t, GROUP_W)

    if t % SUBLANES == 0:
        y_a, s5_last = s5_fused(z4, lw, s5_h0, part_dtype)
        y_a = y_a.reshape(m, GROUP_W)
    else:
        bu = s5_bu(z, lw['w_bu'])
        h_all, s5_last = s5_scan(bu, lw['a_bar'], s5_h0, bsz, t)
        y_a = s5_y(h_all, z, lw['wc'], lw['s5_d'], lw['w_glu'], lw['b_glu'], part_dtype)

    y_b, hgrn_s = hgrn(z4, lw['lb'], lw['hgrn_norm_g'], hgrn_s0, part_dtype)
    y_c = attend(z4)
    y_m = mem_attn(z4, mem_k, mem_v, part_dtype)

    parts = (y_a, y_b.reshape(m, GROUP_W), y_c.reshape(m, GROUP_W), y_m.reshape(m, GROUP_W))
    x_out = outproj(parts, lw['w_out'], lw['layer'], x2).reshape(bsz, t, d)
    return x_out, z4, s5_last, hgrn_s


def kernel(x_prompt, x_sample, cache_k, cache_v, cache_mem_k, cache_mem_v, state_s5_re, state_s5_im, state_hgrn, page_table, mem_prompt, norm_g, w_in, s5_lambda_re, s5_lambda_im, s5_log_dt, s5_b_re, s5_b_im, s5_c_re, s5_c_im, s5_d, s5_w_glu, s5_b_glu, hgrn_lower_bounds, hgrn_norm_g, diff_lq1, diff_lk1, diff_lq2, diff_lk2, diff_norm_g, mem_norm_g, w_mem_kv, w_out, final_norm_g):
    depth = w_in.shape[0]
    bsz_p, t_p, d = x_prompt.shape
    bsz_s, t_s, _ = x_sample.shape
    n_mem = mem_prompt.shape[1]
    lb_all = jnp.cumsum(jax.nn.softmax(hgrn_lower_bounds.astype(F32), axis=0), axis=0)
    lb_all = lb_all - lb_all[0]
    zeros_s5 = jnp.zeros((2, bsz_p, G_A, N_A), F32)
    zeros_hgrn = jnp.zeros((bsz_p, H_B, DK_B, DV_B), F32)

    lam_inits = [0.8 - 0.6 * math.exp(-0.3 * l) for l in range(depth)]
    lams = (jnp.exp(jnp.sum(diff_lq1.astype(F32) * diff_lk1.astype(F32), axis=-1))
            - jnp.exp(jnp.sum(diff_lq2.astype(F32) * diff_lk2.astype(F32), axis=-1))
            + jnp.asarray(lam_inits, F32))
    a_bars, w_bus, wcs = jax.vmap(s5_weights)(s5_lambda_re, s5_lambda_im, s5_log_dt,
                                              s5_b_re, s5_b_im, s5_c_re, s5_c_im)
    a_pows = jax.vmap(s5_scan_multipliers)(a_bars)
    w_glus = s5_w_glu.astype(BF16)

    xp, xs = x_prompt, x_sample
    outs = {k: [] for k in ('kp', 'vp', 'ks', 'vs', 'mkp', 'mvp', 's5p', 's5s', 'hgp', 'hgs')}
    for l in range(depth):
        lam_init = lam_inits[l]
        lam = lams[l]
        lw = {'layer': l, 'norm_g': norm_g[l], 'w_in': w_in, 'w_out': w_out,
              'a_bar': a_bars[l], 'a_pow': a_pows[l], 'w_bu': w_bus[l], 'wc': wcs[l], 's5_d': s5_d[l],
              'w_glu': w_glus[l], 'b_glu': s5_b_glu[l],
              'lb': lb_all[l], 'hgrn_norm_g': hgrn_norm_g[l]}
        z_p, z_s = input_projection(xp, xs, lw)

        hm = rmsnorm(mem_prompt.reshape(bsz_p * n_mem, d), mem_norm_g[l], BF16)
        mkv = matmul_groups(hm, w_mem_kv, l)
        mk_p = mkv[0].reshape(bsz_p, n_mem, H_M, DH_M)
        mv_p = mkv[1].reshape(bsz_p, n_mem, H_M, DH_M)
        attend_p = functools.partial(dattn_prompt, lam=lam, norm_g=diff_norm_g[l],
                                     lam_init=lam_init, out_dtype=BF16)
        xp, z4, s5_last, hg = trunk_layer(xp, z_p, lw, attend_p, mk_p, mv_p, zeros_s5, zeros_hgrn, BF16)
        outs['kp'].append(z4[G_KC].reshape(bsz_p, t_p, H_C, 2 * DH_C))
        outs['vp'].append(z4[G_VC].reshape(bsz_p, t_p, H_C, DV_C))
        outs['mkp'].append(mk_p)
        outs['mvp'].append(mv_p)
        outs['s5p'].append(s5_last)
        outs['hgp'].append(hg)

        attend_s = functools.partial(dattn_sample, cache_k=cache_k, cache_v=cache_v, layer=l,
                                     page_table=page_table, lam=lam, norm_g=diff_norm_g[l],
                                     lam_init=lam_init, out_dtype=F32)
        s5_h0 = jnp.stack([state_s5_re[l].astype(F32), state_s5_im[l].astype(F32)])
        xs, z4, s5_last, hg = trunk_layer(xs, z_s, lw, attend_s, cache_mem_k[l], cache_mem_v[l],
                                          s5_h0, state_hgrn[l], F32)
        outs['ks'].append(z4[G_KC].reshape(bsz_s, t_s, H_C, 2 * DH_C))
        outs['vs'].append(z4[G_VC].reshape(bsz_s, t_s, H_C, DV_C))
        outs['s5s'].append(s5_last)
        outs['hgs'].append(hg)

    y_prompt = rmsnorm(xp.reshape(bsz_p * t_p, d), final_norm_g, F32).reshape(bsz_p, t_p, d)
    y_sample = rmsnorm(xs.reshape(bsz_s * t_s, d), final_norm_g, F32).reshape(bsz_s, t_s, d)
    s5p = jnp.stack(outs['s5p'])
    s5s = jnp.stack(outs['s5s'])
    return (y_prompt, y_sample,
            jnp.stack(outs['kp']), jnp.stack(outs['vp']), jnp.stack(outs['ks']), jnp.stack(outs['vs']),
            jnp.stack(outs['mkp']), jnp.stack(outs['mvp']),
            s5p[:, 0], s5p[:, 1], s5s[:, 0], s5s[:, 1],
            jnp.stack(outs['hgp']), jnp.stack(outs['hgs']))
```

```python
import functools
import math

import jax
import jax.numpy as jnp
from jax import lax
from jax.experimental import pallas as pl
from jax.experimental.pallas import tpu as pltpu

F32 = jnp.float32
BF16 = jnp.bfloat16

D_MODEL = 4096
GROUP_W = 1024
N_GROUPS_IN = 12
C_A = 16
G_A = GROUP_W // C_A
N_A = 64
S5_GROUPS_PER_DOT = 16
S5_CHUNKS = G_A // S5_GROUPS_PER_DOT
S5_STATE = G_A * N_A
DK_B = 128
DV_B = 128
H_B = GROUP_W // DK_B
DV_C = 128
DH_C = DV_C // 2
H_C = GROUP_W // DV_C
H_M = 4
DH_M = GROUP_W // H_M
EPS = 1e-6
LOG2_E = math.log2(math.e)
HGRN_CHUNK = 64
HGRN_SUB = 16
SUBLANES = 8
LANES = 128
MXU_N = 256
VMEM_LIMIT = 60 * 1024 * 1024

(G_UA, G_GA, G_QB, G_FB, G_IB, G_GB, G_QC, G_KC, G_VC, G_GC, G_QM, G_GM) = range(N_GROUPS_IN)


def _params(*sem):
    return pltpu.CompilerParams(dimension_semantics=sem, vmem_limit_bytes=VMEM_LIMIT)


def _silu(x):
    return x * jax.nn.sigmoid(x)


def _row_tile(m, cap):
    return cap if m % cap == 0 else m


def _rmsnorm_kernel(x_ref, g_ref, o_ref):
    x = x_ref[...]
    y = x * lax.rsqrt(jnp.mean(x * x, axis=-1, keepdims=True) + EPS)
    o_ref[...] = (y * g_ref[...]).astype(o_ref.dtype)


def rmsnorm(x, g, out_dtype):
    m, d = x.shape
    tm = _row_tile(m, 256)
    return pl.pallas_call(
        _rmsnorm_kernel,
        grid=(m // tm,),
        in_specs=[pl.BlockSpec((tm, d), lambda i: (i, 0)),
                  pl.BlockSpec((1, d), lambda i: (0, 0))],
        out_specs=pl.BlockSpec((tm, d), lambda i: (i, 0)),
        out_shape=jax.ShapeDtypeStruct((m, d), out_dtype),
        compiler_params=_params("parallel"),
        name="rmsnorm",
    )(x, g.reshape(1, d))


def _matmul_kernel(*refs):
    if len(refs) == 3:
        a_ref, b_ref, o_ref = refs
        o_ref[...] = jnp.dot(a_ref[...], b_ref[...].astype(BF16), preferred_element_type=F32)
        return
    a_ref, a2_ref, b_ref, o_ref, o2_ref = refs
    w = b_ref[...].astype(BF16)
    o_ref[...] = jnp.dot(a_ref[...], w, preferred_element_type=F32)
    first_pass = pl.program_id(0) == 0

    @pl.when(first_pass)
    def _():
        o2_ref[...] = jnp.dot(a2_ref[...], w, preferred_element_type=F32)

    @pl.when(jnp.logical_not(first_pass))
    def _():
        o2_ref[...] = jnp.zeros_like(o2_ref)


def matmul_groups(a, b, layer, a_small=None):
    m, kd = a.shape
    n = b.shape[2]
    tm = _row_tile(m, 2048)
    tn = MXU_N
    per_group = GROUP_W // tn
    lhs = [a] if a_small is None else [a, a_small]
    in_specs = [pl.BlockSpec((tm, kd), lambda i, j: (i, 0))]
    out_specs = [pl.BlockSpec((None, tm, tn), lambda i, j: (j // per_group, i, j % per_group))]
    out_shape = [jax.ShapeDtypeStruct((n // GROUP_W, m, GROUP_W), F32)]
    if a_small is not None:
        m2 = a_small.shape[0]
        in_specs.append(pl.BlockSpec((m2, kd), lambda i, j: (0, 0)))
        out_specs.append(pl.BlockSpec((None, None, m2, tn),
                                      lambda i, j: (i, j // per_group, 0, j % per_group)))
        out_shape.append(jax.ShapeDtypeStruct((m // tm, n // GROUP_W, m2, GROUP_W), F32))
    in_specs.append(pl.BlockSpec((None, kd, tn), lambda i, j: (layer, 0, j)))
    outs = pl.pallas_call(
        _matmul_kernel,
        grid=(m // tm, n // tn),
        in_specs=in_specs,
        out_specs=out_specs,
        out_shape=out_shape,
        compiler_params=_params("parallel", "arbitrary"),
        name="matmul_groups",
    )(*lhs, b)
    return outs[0] if a_small is None else (outs[0], outs[1][0])


def _outproj_kernel(a0_ref, a1_ref, a2_ref, a3_ref, w_ref, x_ref, o_ref):
    acc = x_ref[...]
    for g, a_ref in enumerate((a0_ref, a1_ref, a2_ref, a3_ref)):
        acc += jnp.dot(a_ref[...].astype(BF16), w_ref[g * GROUP_W:(g + 1) * GROUP_W, :].astype(BF16),
                       preferred_element_type=F32)
    o_ref[...] = acc


def outproj(parts, w, layer, x):
    m, d = x.shape
    tm = _row_tile(m, 2048)
    tn = MXU_N
    part_spec = pl.BlockSpec((tm, GROUP_W), lambda i, j: (i, 0))
    return pl.pallas_call(
        _outproj_kernel,
        grid=(m // tm, d // tn),
        in_specs=[part_spec, part_spec, part_spec, part_spec,
                  pl.BlockSpec((None, 4 * GROUP_W, tn), lambda i, j: (layer, 0, j)),
                  pl.BlockSpec((tm, tn), lambda i, j: (i, j))],
        out_specs=pl.BlockSpec((tm, tn), lambda i, j: (i, j)),
        out_shape=jax.ShapeDtypeStruct((m, d), F32),
        compiler_params=_params("parallel", "arbitrary"),
        name="outproj",
    )(*parts, w, x)


def _s5_bu_kernel(u_ref, w_ref, o_ref):
    r = jnp.dot(u_ref[...].astype(BF16), w_ref[...], preferred_element_type=F32)
    half = r.shape[1] // 2
    o_ref[0] = r[:, :half]
    o_ref[1] = r[:, half:]


def s5_bu(z, w_bu):
    m = z.shape[1]
    tm = _row_tile(m, 512)
    kw = S5_GROUPS_PER_DOT * C_A
    nw = S5_GROUPS_PER_DOT * N_A
    return pl.pallas_call(
        _s5_bu_kernel,
        grid=(m // tm, S5_CHUNKS),
        in_specs=[pl.BlockSpec((None, tm, kw), lambda i, k: (G_UA, i, k)),
                  pl.BlockSpec((None, kw, 2 * nw), lambda i, k: (k, 0, 0))],
        out_specs=pl.BlockSpec((2, tm, nw), lambda i, k: (0, i, k)),
        out_shape=jax.ShapeDtypeStruct((2, m, S5_STATE), F32),
        compiler_params=_params("parallel", "parallel"),
        name="s5_bu",
    )(z, w_bu)


def _s5_scan_kernel(bu_ref, a_ref, h0_ref, h_ref, hl_ref, st_ref, *, tt):
    j = pl.program_id(1)

    @pl.when(j == 0)
    def _():
        st_ref[...] = h0_ref[...]

    a_re = a_ref[0]
    a_im = a_ref[1]

    def body(t, carry):
        h_re, h_im = carry
        n_re = a_re * h_re - a_im * h_im + bu_ref[0, t]
        n_im = a_re * h_im + a_im * h_re + bu_ref[1, t]
        h_ref[0, t] = n_re
        h_ref[1, t] = n_im
        return n_re, n_im

    h_re, h_im = lax.fori_loop(0, tt, body, (st_ref[0], st_ref[1]), unroll=min(tt, 8))
    st_ref[0] = h_re
    st_ref[1] = h_im

    @pl.when(j == pl.num_programs(1) - 1)
    def _():
        hl_ref[0] = h_re
        hl_ref[1] = h_im


def s5_scan(bu, a_bar, h0, bsz, t):
    rows = S5_STATE // LANES
    bu5 = bu.reshape(2, bsz, t, rows, LANES)
    tt = _row_tile(t, 128)
    h, h_last = pl.pallas_call(
        functools.partial(_s5_scan_kernel, tt=tt),
        grid=(bsz, t // tt),
        in_specs=[pl.BlockSpec((2, None, tt, rows, LANES), lambda b, j: (0, b, j, 0, 0)),
                  pl.BlockSpec((2, rows, LANES), lambda b, j: (0, 0, 0)),
                  pl.BlockSpec((2, None, rows, LANES), lambda b, j: (0, b, 0, 0))],
        out_specs=[pl.BlockSpec((2, None, tt, rows, LANES), lambda b, j: (0, b, j, 0, 0)),
                   pl.BlockSpec((2, None, rows, LANES), lambda b, j: (0, b, 0, 0))],
        out_shape=[jax.ShapeDtypeStruct((2, bsz, t, rows, LANES), F32),
                   jax.ShapeDtypeStruct((2, bsz, rows, LANES), F32)],
        scratch_shapes=[pltpu.VMEM((2, rows, LANES), F32)],
        compiler_params=_params("parallel", "arbitrary"),
        name="s5_scan",
    )(bu5, a_bar.reshape(2, rows, LANES), h0.reshape(2, bsz, rows, LANES))
    return h.reshape(2, bsz * t, S5_STATE), h_last.reshape(2, bsz, G_A, N_A)


def _s5_y_kernel(h_ref, u_ref, g_ref, wc_ref, d_ref, wg_ref, bg_ref, o_ref):
    nw = S5_GROUPS_PER_DOT * N_A
    parts = []
    for k in range(S5_CHUNKS):
        h_re = h_ref[0, :, k * nw:(k + 1) * nw].astype(BF16)
        h_im = h_ref[1, :, k * nw:(k + 1) * nw].astype(BF16)
        parts.append(jnp.dot(h_re, wc_ref[0, k], preferred_element_type=F32)
                     + jnp.dot(h_im, wc_ref[1, k], preferred_element_type=F32))
    y = jnp.concatenate(parts, axis=-1) + d_ref[...] * u_ref[...]
    y = jax.nn.gelu(y)
    glu = jnp.dot(y.astype(BF16), wg_ref[...], preferred_element_type=F32) + bg_ref[...]
    y = y * jax.nn.sigmoid(glu)
    o_ref[...] = (y * _silu(g_ref[...])).astype(o_ref.dtype)


def s5_y(h, z, wc, d, w_glu, b_glu, out_dtype):
    m = z.shape[1]
    tm = _row_tile(m, 256)
    kw = S5_GROUPS_PER_DOT * C_A
    nw = S5_GROUPS_PER_DOT * N_A
    return pl.pallas_call(
        _s5_y_kernel,
        grid=(m // tm,),
        in_specs=[pl.BlockSpec((2, tm, S5_STATE), lambda i: (0, i, 0)),
                  pl.BlockSpec((None, tm, GROUP_W), lambda i: (G_UA, i, 0)),
                  pl.BlockSpec((None, tm, GROUP_W), lambda i: (G_GA, i, 0)),
                  pl.BlockSpec((2, S5_CHUNKS, nw, kw), lambda i: (0, 0, 0, 0)),
                  pl.BlockSpec((1, GROUP_W), lambda i: (0, 0)),
                  pl.BlockSpec((GROUP_W, GROUP_W), lambda i: (0, 0)),
                  pl.BlockSpec((1, GROUP_W), lambda i: (0, 0))],
        out_specs=pl.BlockSpec((tm, GROUP_W), lambda i: (i, 0)),
        out_shape=jax.ShapeDtypeStruct((m, GROUP_W), out_dtype),
        compiler_params=_params("parallel"),
        name="s5_y",
    )(h, z, z, wc, d.reshape(1, GROUP_W), w_glu, b_glu.reshape(1, GROUP_W))


def _s5_fused_kernel(u_ref, g_ref, wbu_ref, pw_ref, h0_ref, wc_ref, d_ref, wg_ref, bg_ref,
                     o_ref, hl_ref, st_ref, h_scr, *, tt):
    j = pl.program_id(1)
    nw = S5_GROUPS_PER_DOT * N_A
    kw = S5_GROUPS_PER_DOT * C_A
    n_tiles = tt // SUBLANES

    @pl.when(j == 0)
    def _():
        st_ref[...] = h0_ref[...]

    def axpy(x_re, x_im, a_re, a_im, s_re, s_im):
        return x_re + (a_re * s_re - a_im * s_im), x_im + (a_re * s_im + a_im * s_re)

    u = u_ref[...]
    y_parts = []
    for k in range(S5_CHUNKS):
        cols = slice(k * nw, (k + 1) * nw)
        bu = jnp.dot(u[:, k * kw:(k + 1) * kw].astype(BF16), wbu_ref[k], preferred_element_type=F32)
        h_scr[0] = bu[:, :nw].reshape(n_tiles, SUBLANES, nw)
        h_scr[1] = bu[:, nw:].reshape(n_tiles, SUBLANES, nw)

        def tile_step(g, carry, cols=cols):
            c_re, c_im = carry
            x_re, x_im = h_scr[0, g], h_scr[1, g]
            for p, shift in enumerate((1, 2, 4)):
                x_re, x_im = axpy(x_re, x_im, pw_ref[0, p, :, cols], pw_ref[1, p, :, cols],
                                  pltpu.roll(x_re, shift, axis=0), pltpu.roll(x_im, shift, axis=0))
            x_re, x_im = axpy(x_re, x_im, pw_ref[0, 3, :, cols], pw_ref[1, 3, :, cols], c_re, c_im)
            h_scr[0, g] = x_re
            h_scr[1, g] = x_im
            return x_re[SUBLANES - 1:SUBLANES, :], x_im[SUBLANES - 1:SUBLANES, :]

        c_re, c_im = lax.fori_loop(0, n_tiles, tile_step, (st_ref[0, :, cols], st_ref[1, :, cols]),
                                   unroll=True)
        st_ref[0, :, cols] = c_re
        st_ref[1, :, cols] = c_im
        h_re = h_scr[0].reshape(tt, nw).astype(BF16)
        h_im = h_scr[1].reshape(tt, nw).astype(BF16)
        y_parts.append(jnp.dot(h_re, wc_ref[0, k], preferred_element_type=F32)
                       + jnp.dot(h_im, wc_ref[1, k], preferred_element_type=F32))
    y = jnp.concatenate(y_parts, axis=-1) + d_ref[...] * u
    y = jax.nn.gelu(y)
    glu = jnp.dot(y.astype(BF16), wg_ref[...], preferred_element_type=F32) + bg_ref[...]
    y = y * jax.nn.sigmoid(glu)
    o_ref[...] = (y * _silu(g_ref[...])).astype(o_ref.dtype)

    @pl.when(j == pl.num_programs(1) - 1)
    def _():
        hl_ref[...] = st_ref[...]


def s5_fused(z4, lw, h0, out_dtype):
    _, bsz, t, _ = z4.shape
    tt = _row_tile(t, 512)
    kw = S5_GROUPS_PER_DOT * C_A
    nw = S5_GROUPS_PER_DOT * N_A

    def tok(g):
        return pl.BlockSpec((None, None, tt, GROUP_W), lambda b, j: (g, b, j, 0))

    def whole(shape):
        return pl.BlockSpec(shape, lambda b, j: (0,) * len(shape))

    state_spec = pl.BlockSpec((2, None, 1, S5_STATE), lambda b, j: (0, b, 0, 0))
    y, h_last = pl.pallas_call(
        functools.partial(_s5_fused_kernel, tt=tt),
        grid=(bsz, t // tt),
        in_specs=[tok(G_UA), tok(G_GA),
                  whole((S5_CHUNKS, kw, 2 * nw)),
                  whole((2, 4, SUBLANES, S5_STATE)),
                  state_spec,
                  whole((2, S5_CHUNKS, nw, kw)),
                  whole((1, GROUP_W)), whole((GROUP_W, GROUP_W)), whole((1, GROUP_W))],
        out_specs=[pl.BlockSpec((None, tt, GROUP_W), lambda b, j: (b, j, 0)), state_spec],
        out_shape=[jax.ShapeDtypeStruct((bsz, t, GROUP_W), out_dtype),
                   jax.ShapeDtypeStruct((2, bsz, 1, S5_STATE), F32)],
        scratch_shapes=[pltpu.VMEM((2, 1, S5_STATE), F32),
                        pltpu.VMEM((2, tt // SUBLANES, SUBLANES, nw), F32)],
        compiler_params=_params("parallel", "arbitrary"),
        name="s5_fused",
    )(z4, z4, lw['w_bu'], lw['a_pow'], h0.reshape(2, bsz, 1, S5_STATE), lw['wc'],
      lw['s5_d'].reshape(1, GROUP_W), lw['w_glu'], lw['b_glu'].reshape(1, GROUP_W))
    return y, h_last.reshape(2, bsz, G_A, N_A)


def s5_scan_multipliers(a_pair):
    a = lax.complex(a_pair[0], a_pair[1]).reshape(S5_STATE)
    powers = [a]
    for _ in range(SUBLANES - 1):
        powers.append(powers[-1] * a)
    row = jnp.arange(SUBLANES)[:, None]
    planes = [jnp.where(row >= s, powers[s - 1][None, :], 0.0) for s in (1, 2, 4)]
    planes.append(jnp.stack(powers))
    pw = jnp.stack(planes)
    return jnp.stack([pw.real, pw.imag])


def s5_weights(lam_re, lam_im, log_dt, b_re, b_im, c_re, c_im):
    lam = lax.complex(lam_re.astype(F32), lam_im.astype(F32))
    dt = jnp.exp(log_dt.astype(F32))[:, None]
    a_bar = jnp.exp(lam * dt)
    b_bar = ((a_bar - 1.0) / lam)[..., None] * lax.complex(b_re.astype(F32), b_im.astype(F32))
    gpd, kw, nw = S5_GROUPS_PER_DOT, S5_GROUPS_PER_DOT * C_A, S5_GROUPS_PER_DOT * N_A
    bb = jnp.stack([b_bar.real, b_bar.imag]).reshape(2, S5_CHUNKS, gpd, N_A, C_A)
    b_cols = bb.transpose(1, 4, 0, 2, 3).reshape(S5_CHUNKS, C_A, 2 * nw)
    row_g = jnp.arange(kw)[:, None] // C_A
    col_h = (jnp.arange(2 * nw)[None, :] % nw) // N_A
    w_bu = jnp.where(row_g == col_h, jnp.tile(b_cols, (1, gpd, 1)), 0.0).astype(BF16)
    cc = jnp.stack([c_re.astype(F32), -c_im.astype(F32)]).reshape(2, S5_CHUNKS, gpd, C_A, N_A)
    c_rows = cc.transpose(0, 1, 2, 4, 3).reshape(2, S5_CHUNKS, nw, C_A)
    wc = jnp.where(jnp.arange(nw)[:, None] // N_A == jnp.arange(kw)[None, :] // C_A,
                   jnp.tile(c_rows, (1, 1, 1, gpd)), 0.0).astype(BF16)
    a_pair = jnp.stack([a_bar.real, a_bar.imag])
    return a_pair, w_bu, wc


def _hgrn_kernel(q_ref, f_ref, i_ref, g_ref, lb_ref, ng_ref, s0_ref, y_ref, sl_ref,
                 st_ref, pad_scr, *, tc, chunk, t_valid):
    sub = HGRN_SUB
    n_sb = chunk // sub
    j = pl.program_id(2)
    nt = (((1,), (1,)), ((), ()))

    @pl.when(j == 0)
    def _():
        st_ref[...] = s0_ref[...].T

    lb = lb_ref[...]
    padded = tc < chunk
    if padded:
        pad_scr[...] = jnp.zeros_like(pad_scr)
        pad_scr[0, 0:tc, :] = q_ref[...]
        pad_scr[1, 0:tc, :] = f_ref[...]
        pad_scr[2, 0:tc, :] = i_ref[...]
    row_c = lax.broadcasted_iota(jnp.int32, (chunk, LANES), 0)
    row_t = lax.broadcasted_iota(jnp.int32, (sub, chunk), 0)
    lane_s = lax.broadcasted_iota(jnp.int32, (sub, chunk), 1)
    lane_8 = lax.broadcasted_iota(jnp.int32, (SUBLANES, chunk), 1)
    tri = (lax.broadcasted_iota(jnp.int32, (chunk, chunk), 1)
           <= lax.broadcasted_iota(jnp.int32, (chunk, chunk), 0)).astype(F32)

    def gates(ci):
        rows = slice(ci * chunk, (ci + 1) * chunk)
        if padded:
            q, f_pre, inp = pad_scr[0], pad_scr[1], pad_scr[2]
        else:
            q, f_pre, inp = q_ref[rows, :], f_ref[rows, :], i_ref[rows, :]
        f = lb + (1.0 - lb) * jax.nn.sigmoid(f_pre)
        log_f = jnp.log2(f)
        kk = 1.0 - f
        if padded:
            log_f = jnp.where(row_c < t_valid, log_f, 0.0)
            kk = jnp.where(row_c < t_valid, kk, 0.0)
        cum = jnp.dot(tri, log_f, precision=lax.Precision.HIGHEST, preferred_element_type=F32)
        return q, kk, cum, inp.astype(BF16)

    def intra_scores(q, kk, cum):
        blocks = []
        for i in range(n_sb):
            rows = slice(i * sub, (i + 1) * sub)
            q_i, kk_i, cum_i = q[rows], kk[rows], cum[rows]
            if i > 0:
                edge = cum[i * sub - 1:i * sub, :]
                a_i = q_i * jnp.exp2(cum_i - edge)
                kt_i = kk * jnp.exp2(jnp.where(row_c < i * sub, edge - cum, -jnp.inf))
                sc = lax.dot_general(a_i.astype(BF16), kt_i.astype(BF16), nt, preferred_element_type=F32)
            else:
                sc = jnp.zeros((sub, chunk), F32)
            tiles = [sc[r:r + SUBLANES] for r in range(0, sub, SUBLANES)]
            for s in range(min(sub, t_valid)):
                first = s // SUBLANES
                live = slice(first * SUBLANES, sub)
                w = (q_i[live] * kk_i[s:s + 1, :]) * jnp.exp2(cum_i[live] - cum_i[s:s + 1, :])
                col = jnp.sum(w, axis=-1, keepdims=True)
                for r in range(first, sub // SUBLANES):
                    piece = col[(r - first) * SUBLANES:(r - first + 1) * SUBLANES]
                    tiles[r] = jnp.where(lane_8 == i * sub + s, piece, tiles[r])
            sc = jnp.concatenate(tiles, axis=0)
            blocks.append(jnp.where(lane_s - i * sub > row_t, 0.0, sc))
        scores = blocks[0] if n_sb == 1 else jnp.concatenate(blocks, axis=0)
        return scores.astype(BF16)

    def advance(ci, q, kk, cum, inp_b, scores):
        rows = slice(ci * chunk, (ci + 1) * chunk)
        last = cum[chunk - 1:chunk, :]
        st = st_ref[...]
        o = lax.dot_general((q * jnp.exp2(cum)).astype(BF16), st.astype(BF16), nt,
                            preferred_element_type=F32)
        o = o + jnp.dot(scores, inp_b, preferred_element_type=F32)
        kt = (kk * jnp.exp2(last - cum)).astype(BF16)
        upd = lax.dot_general(inp_b, kt, (((0,), (0,)), ((), ())), preferred_element_type=F32)
        st_ref[...] = st * jnp.exp2(last) + upd
        y = o * lax.rsqrt(jnp.mean(o * o, axis=-1, keepdims=True) + EPS) * ng_ref[...]
        if padded:
            y_ref[...] = (y[0:tc] * _silu(g_ref[...])).astype(y_ref.dtype)
        else:
            y_ref[rows, :] = (y * _silu(g_ref[rows, :])).astype(y_ref.dtype)

    n_chunks = max(tc // chunk, 1)
    staged = [gates(ci) for ci in range(n_chunks)]
    scored = [intra_scores(q, kk, cum) for q, kk, cum, _ in staged]
    for ci in range(n_chunks):
        advance(ci, *staged[ci], scored[ci])

    @pl.when(j == pl.num_programs(2) - 1)
    def _():
        sl_ref[...] = st_ref[...].T


def hgrn(z4, lb, norm_g, s0, out_dtype):
    _, bsz, t, _ = z4.shape
    chunk = HGRN_CHUNK if t % HGRN_CHUNK == 0 else HGRN_SUB
    tc = _row_tile(t, 1024) if t >= chunk else t
    t_valid = min(chunk, tc)

    def col(g):
        return pl.BlockSpec((None, None, tc, DK_B), lambda b, h, j: (g, b, j, h))

    state_spec = pl.BlockSpec((None, None, DK_B, DV_B), lambda b, h, j: (b, h, 0, 0))
    return pl.pallas_call(
        functools.partial(_hgrn_kernel, tc=tc, chunk=chunk, t_valid=t_valid),
        grid=(bsz, H_B, t // tc),
        in_specs=[col(G_QB), col(G_FB), col(G_IB), col(G_GB),
                  pl.BlockSpec((1, DK_B), lambda b, h, j: (0, h)),
                  pl.BlockSpec((1, DV_B), lambda b, h, j: (0, 0)),
                  state_spec],
        out_specs=[pl.BlockSpec((None, tc, DV_B), lambda b, h, j: (b, j, h)), state_spec],
        out_shape=[jax.ShapeDtypeStruct((bsz, t, GROUP_W), out_dtype),
                   jax.ShapeDtypeStruct((bsz, H_B, DK_B, DV_B), F32)],
        scratch_shapes=[pltpu.VMEM((DV_B, DK_B), F32),
                        pltpu.VMEM((3, chunk, LANES), F32)],
        compiler_params=_params("parallel", "parallel", "arbitrary"),
        name="hgrn",
    )(z4, z4, z4, z4, lb.reshape(1, GROUP_W), norm_g.reshape(1, DV_B), s0)


def _diff_norm_gate(o, ng, gate, out_scale):
    y = o * lax.rsqrt(jnp.mean(o * o, axis=-1, keepdims=True) + EPS) * ng
    return (y * out_scale) * _silu(gate)


def _dattn_prompt_kernel(lam_ref, q_ref, k_ref, v_ref, g_ref, ng_ref, o_ref,
                         kb_scr, vb_scr, *, tq, n_blk, out_scale):
    i = pl.program_id(2)

    @pl.when(i == 0)
    def _():
        kb_scr[...] = k_ref[...].astype(BF16)
        vb_scr[...] = v_ref[...].astype(BF16)

    nt = (((1,), (1,)), ((), ()))

    def attend(n_past):
        past = n_past * tq
        q = q_ref[...] * (DH_C ** -0.5 * LOG2_E)
        lane = lax.broadcasted_iota(jnp.int32, (tq, DV_C), 1)
        causal = (lax.broadcasted_iota(jnp.int32, (tq, tq), 1)
                  <= lax.broadcasted_iota(jnp.int32, (tq, tq), 0))
        k_diag = kb_scr[past:past + tq, :]
        v_diag = vb_scr[past:past + tq, :]

        def scores(qj):
            s_d = jnp.where(causal, lax.dot_general(qj, k_diag, nt, preferred_element_type=F32), -jnp.inf)
            m = jnp.max(s_d, axis=-1, keepdims=True)
            s_p = None
            if n_past:
                s_p = lax.dot_general(qj, kb_scr[0:past, :], nt, preferred_element_type=F32)
                m = jnp.maximum(m, jnp.max(s_p, axis=-1, keepdims=True))
            return s_d, s_p, m

        def weights(s_d, s_p, m):
            e_d = jnp.exp2(s_d - m)
            l = jnp.sum(e_d, axis=-1, keepdims=True)
            e_p = None
            if n_past:
                e_p = jnp.exp2(s_p - m)
                l = l + jnp.sum(e_p, axis=-1, keepdims=True)
                e_p = e_p.astype(BF16)
            return e_d.astype(BF16), e_p, l

        def attend_values(e_d, e_p, l):
            pv = jnp.dot(e_d, v_diag, preferred_element_type=F32)
            if n_past:
                pv = pv + jnp.dot(e_p, vb_scr[0:past, :], preferred_element_type=F32)
            return pv / l

        staged = [scores(jnp.where(lane < DH_C, q, 0.0).astype(BF16)),
                  scores(jnp.where(lane >= DH_C, q, 0.0).astype(BF16))]
        staged = [weights(*st) for st in staged]
        o = attend_values(*staged[0]) - lam_ref[...] * attend_values(*staged[1])
        o_ref[...] = _diff_norm_gate(o, ng_ref[...], g_ref[...], out_scale).astype(o_ref.dtype)

    for n_past in range(n_blk):
        pl.when(i == n_past)(functools.partial(attend, n_past))


def dattn_prompt(z4, lam, norm_g, lam_init, out_dtype):
    _, bsz, t, _ = z4.shape
    tq = _row_tile(t, 512)

    def q_col(g):
        return pl.BlockSpec((None, None, tq, DV_C), lambda b, h, i: (g, b, i, h))

    def kv_col(g):
        return pl.BlockSpec((None, None, t, DV_C), lambda b, h, i: (g, b, 0, h))

    vec = pl.BlockSpec((1, DV_C), lambda b, h, i: (0, 0))
    return pl.pallas_call(
        functools.partial(_dattn_prompt_kernel, tq=tq, n_blk=t // tq, out_scale=1.0 - lam_init),
        grid=(bsz, H_C, t // tq),
        in_specs=[vec, q_col(G_QC), kv_col(G_KC), kv_col(G_VC), q_col(G_GC), vec],
        out_specs=pl.BlockSpec((None, tq, DV_C), lambda b, h, i: (b, i, h)),
        out_shape=jax.ShapeDtypeStruct((bsz, t, GROUP_W), out_dtype),
        scratch_shapes=[pltpu.VMEM((t, DV_C), BF16),
                        pltpu.VMEM((t, DV_C), BF16)],
        compiler_params=_params("parallel", "parallel", "arbitrary"),
        name="dattn_prompt",
    )(jnp.full((1, DV_C), lam, F32), z4, z4, z4, z4, norm_g.reshape(1, DV_C))


def _dattn_sample_kernel(pt_ref, lam_ref, q_ref, kn_ref, vn_ref, g_ref, ng_ref, *rest,
                         pps, t, out_scale):
    k_refs = rest[:pps]
    v_refs = rest[pps:2 * pps]
    o_ref = rest[2 * pps]
    qm_scr, m_scr, l_scr, acc_scr, kn_scr, vn_scr = rest[2 * pps + 1:]
    del pt_ref
    j = pl.program_id(1)
    half = t * H_C
    n_rows = 2 * half
    page_rows = k_refs[0].shape[0] * H_C
    nt = (((1,), (1,)), ((), ()))

    @pl.when(j == 0)
    def _():
        q = q_ref[...].reshape(half, DV_C) * (DH_C ** -0.5)
        lane = lax.broadcasted_iota(jnp.int32, (half, DV_C), 1)
        qm_scr[0:half, :] = jnp.where(lane < DH_C, q, 0.0).astype(BF16)
        qm_scr[half:n_rows, :] = jnp.where(lane >= DH_C, q, 0.0).astype(BF16)
        m_scr[...] = jnp.full_like(m_scr, -jnp.inf)
        l_scr[...] = jnp.zeros_like(l_scr)
        acc_scr[...] = jnp.zeros_like(acc_scr)

    def online_update(s, v_blocks, width):
        m_old = m_scr[...]
        m_new = jnp.maximum(m_old, jnp.max(s, axis=-1, keepdims=True))
        alpha = jnp.exp(m_old - m_new)
        e = jnp.exp(s - m_new)
        l_scr[...] = alpha * l_scr[...] + jnp.sum(e, axis=-1, keepdims=True)
        pv = None
        for r, vb in enumerate(v_blocks):
            d = jnp.dot(e[:, r * width:(r + 1) * width].astype(BF16), vb, preferred_element_type=F32)
            pv = d if pv is None else pv + d
        acc_scr[...] = alpha * acc_scr[...] + pv
        m_scr[...] = m_new

    qm = qm_scr[...]
    same_head = (lax.broadcasted_iota(jnp.int32, (n_rows, page_rows), 0) % H_C
                 == lax.broadcasted_iota(jnp.int32, (n_rows, page_rows), 1) % H_C)
    s_parts = []
    for r in range(pps):
        kp = k_refs[r][...].reshape(page_rows, DV_C).astype(BF16)
        s_parts.append(jnp.where(same_head, lax.dot_general(qm, kp, nt, preferred_element_type=F32),
                                 -jnp.inf))
    online_update(jnp.concatenate(s_parts, axis=-1),
                  [v_refs[r][...].reshape(page_rows, DV_C).astype(BF16) for r in range(pps)], page_rows)

    @pl.when(j == pl.num_programs(1) - 1)
    def _():
        kn_scr[...] = jnp.zeros_like(kn_scr)
        vn_scr[...] = jnp.zeros_like(vn_scr)
        kn_scr[0:half, :] = kn_ref[...].reshape(half, DV_C)
        vn_scr[0:half, :] = vn_ref[...].reshape(half, DV_C)
        s_new = lax.dot_general(qm, kn_scr[...].astype(BF16), nt, preferred_element_type=F32)
        row = lax.broadcasted_iota(jnp.int32, (n_rows, LANES), 0)
        col = lax.broadcasted_iota(jnp.int32, (n_rows, LANES), 1)
        visible = (col % H_C == row % H_C) & (col // H_C <= (row // H_C) % t)
        online_update(jnp.where(visible, s_new, -jnp.inf), [vn_scr[...].astype(BF16)], LANES)

        o = (acc_scr[0:half, :] / l_scr[0:half, :]
             - lam_ref[...] * (acc_scr[half:n_rows, :] / l_scr[half:n_rows, :]))
        y = _diff_norm_gate(o, ng_ref[...], g_ref[...].reshape(half, DV_C), out_scale)
        o_ref[...] = y.reshape(t, H_C, DV_C).astype(o_ref.dtype)


def dattn_sample(z4, cache_k, cache_v, layer, page_table, lam, norm_g, lam_init, out_dtype):
    _, bsz, t, _ = z4.shape
    page = cache_k.shape[2]
    n_pages = page_table.shape[1]
    pps = 16 if n_pages % 16 == 0 else 1
    heads = z4[G_QC:G_GC + 1].reshape(4, bsz, t, H_C, DV_C)

    def tok(g):
        return pl.BlockSpec((None, None, t, H_C, DV_C), lambda b, j, pt: (g, b, 0, 0, 0))

    def page_spec(r):
        return pl.BlockSpec((None, None, page, H_C, DV_C),
                            lambda b, j, pt: (layer, pt[b * n_pages + j * pps + r], 0, 0, 0))

    vec = pl.BlockSpec((1, DV_C), lambda b, j, pt: (0, 0))
    n_rows = 2 * t * H_C
    grid_spec = pltpu.PrefetchScalarGridSpec(
        num_scalar_prefetch=1,
        grid=(bsz, n_pages // pps),
        in_specs=[vec, tok(0), tok(1), tok(2), tok(3), vec]
        + [page_spec(r) for r in range(pps)] + [page_spec(r) for r in range(pps)],
        out_specs=pl.BlockSpec((None, t, H_C, DV_C), lambda b, j, pt: (b, 0, 0, 0)),
        scratch_shapes=[pltpu.VMEM((n_rows, DV_C), BF16),
                        pltpu.VMEM((n_rows, 1), F32),
                        pltpu.VMEM((n_rows, 1), F32),
                        pltpu.VMEM((n_rows, DV_C), F32),
                        pltpu.VMEM((LANES, DV_C), F32),
                        pltpu.VMEM((LANES, DV_C), F32)],
    )
    y = pl.pallas_call(
        functools.partial(_dattn_sample_kernel, pps=pps, t=t, out_scale=1.0 - lam_init),
        grid_spec=grid_spec,
        out_shape=jax.ShapeDtypeStruct((bsz, t, H_C, DV_C), out_dtype),
        compiler_params=_params("parallel", "arbitrary"),
        name="dattn_sample",
    )(page_table.reshape(-1), jnp.full((1, DV_C), lam, F32), heads, heads, heads, heads,
      norm_g.reshape(1, DV_C), *([cache_k] * pps), *([cache_v] * pps))
    return y.reshape(bsz, t, H_C * DV_C)


def _mem_attn_kernel(q_ref, g_ref, mk_ref, mv_ref, o_ref, pad_scr, *, tq):
    rows = pad_scr.shape[0]
    if tq < rows:
        pad_scr[...] = jnp.zeros_like(pad_scr)
        pad_scr[0:tq, :] = q_ref[...]
        q_all = pad_scr[...]
    else:
        q_all = q_ref[...]
    nt = (((1,), (1,)), ((), ()))
    for h in range(H_M):
        cols = slice(h * DH_M, (h + 1) * DH_M)
        q = (q_all[:, cols] * (DH_M ** -0.5)).astype(BF16)
        s = lax.dot_general(q, mk_ref[:, cols].astype(BF16), nt, preferred_element_type=F32)
        e = jnp.exp(s - jnp.max(s, axis=-1, keepdims=True))
        l = jnp.sum(e, axis=-1, keepdims=True)
        o = jnp.dot(e.astype(BF16), mv_ref[:, cols].astype(BF16), preferred_element_type=F32) / l
        o_ref[:, cols] = (o[0:tq, :] * _silu(g_ref[:, cols])).astype(o_ref.dtype)


def mem_attn(z4, mem_k, mem_v, out_dtype):
    _, bsz, t, _ = z4.shape
    n_mem = mem_k.shape[1]
    tq = _row_tile(t, 256)
    rows = max(tq, 2 * SUBLANES)

    def tok(g):
        return pl.BlockSpec((None, None, tq, GROUP_W), lambda b, i: (g, b, i, 0))

    mem_spec = pl.BlockSpec((None, n_mem, GROUP_W), lambda b, i: (b, 0, 0))
    return pl.pallas_call(
        functools.partial(_mem_attn_kernel, tq=tq),
        grid=(bsz, t // tq),
        in_specs=[tok(G_QM), tok(G_GM), mem_spec, mem_spec],
        out_specs=pl.BlockSpec((None, tq, GROUP_W), lambda b, i: (b, i, 0)),
        out_shape=jax.ShapeDtypeStruct((bsz, t, GROUP_W), out_dtype),
        scratch_shapes=[pltpu.VMEM((rows, GROUP_W), F32)],
        compiler_params=_params("parallel", "parallel"),
        name="mem_attn",
    )(z4, z4, mem_k.reshape(bsz, n_mem, GROUP_W), mem_v.reshape(bsz, n_mem, GROUP_W))


def input_projection(x_p, x_s, lw):
    d = x_p.shape[-1]
    h_p = rmsnorm(x_p.reshape(-1, d), lw['norm_g'], BF16)
    h_s = rmsnorm(x_s.reshape(-1, d), lw['norm_g'], BF16)
    return matmul_groups(h_p, lw['w_in'], lw['layer'], a_small=h_s)


def trunk_layer(x, z, lw, attend, mem_k, mem_v, s5_h0, hgrn_s0, part_dtype):
    bsz, t, d = x.shape
    m = bsz * t
    x2 = x.reshape(m, d)
    z4 = z.reshape(N_GROUPS_IN, bsz, t, GROUP_W)

    if t % SUBLANES == 0:
        y_a, s5_last = s5_fused(z4, lw, s5_h0, part_dtype)
        y_a = y_a.reshape(m, GROUP_W)
    else:
        bu = s5_bu(z, lw['w_bu'])
        h_all, s5_last = s5_scan(bu, lw['a_bar'], s5_h0, bsz, t)
        y_a = s5_y(h_all, z, lw['wc'], lw['s5_d'], lw['w_glu'], lw['b_glu'], part_dtype)

    y_b, hgrn_s = hgrn(z4, lw['lb'], lw['hgrn_norm_g'], hgrn_s0, part_dtype)
    y_c = attend(z4)
    y_m = mem_attn(z4, mem_k, mem_v, part_dtype)

    parts = (y_a, y_b.reshape(m, GROUP_W), y_c.reshape(m, GROUP_W), y_m.reshape(m, GROUP_W))
    x_out = outproj(parts, lw['w_out'], lw['layer'], x2).reshape(bsz, t, d)
    return x_out, z4, s5_last, hgrn_s


def kernel(x_prompt, x_sample, cache_k, cache_v, cache_mem_k, cache_mem_v, state_s5_re, state_s5_im, state_hgrn, page_table, mem_prompt, norm_g, w_in, s5_lambda_re, s5_lambda_im, s5_log_dt, s5_b_re, s5_b_im, s5_c_re, s5_c_im, s5_d, s5_w_glu, s5_b_glu, hgrn_lower_bounds, hgrn_norm_g, diff_lq1, diff_lk1, diff_lq2, diff_lk2, diff_norm_g, mem_norm_g, w_mem_kv, w_out, final_norm_g):
    depth = w_in.shape[0]
    bsz_p, t_p, d = x_prompt.shape
    bsz_s, t_s, _ = x_sample.shape
    n_mem = mem_prompt.shape[1]
    lb_all = jnp.cumsum(jax.nn.softmax(hgrn_lower_bounds.astype(F32), axis=0), axis=0)
    lb_all = lb_all - lb_all[0]
    zeros_s5 = jnp.zeros((2, bsz_p, G_A, N_A), F32)
    zeros_hgrn = jnp.zeros((bsz_p, H_B, DK_B, DV_B), F32)

    lam_inits = [0.8 - 0.6 * math.exp(-0.3 * l) for l in range(depth)]
    lams = (jnp.exp(jnp.sum(diff_lq1.astype(F32) * diff_lk1.astype(F32), axis=-1))
            - jnp.exp(jnp.sum(diff_lq2.astype(F32) * diff_lk2.astype(F32), axis=-1))
            + jnp.asarray(lam_inits, F32))
    a_bars, w_bus, wcs = jax.vmap(s5_weights)(s5_lambda_re, s5_lambda_im, s5_log_dt,
                                              s5_b_re, s5_b_im, s5_c_re, s5_c_im)
    a_pows = jax.vmap(s5_scan_multipliers)(a_bars)
    w_glus = s5_w_glu.astype(BF16)

    xp, xs = x_prompt, x_sample
    outs = {k: [] for k in ('kp', 'vp', 'ks', 'vs', 'mkp', 'mvp', 's5p', 's5s', 'hgp', 'hgs')}
    for l in range(depth):
        lam_init = lam_inits[l]
        lam = lams[l]
        lw = {'layer': l, 'norm_g': norm_g[l], 'w_in': w_in, 'w_out': w_out,
              'a_bar': a_bars[l], 'a_pow': a_pows[l], 'w_bu': w_bus[l], 'wc': wcs[l], 's5_d': s5_d[l],
              'w_glu': w_glus[l], 'b_glu': s5_b_glu[l],
              'lb': lb_all[l], 'hgrn_norm_g': hgrn_norm_g[l]}
        z_p, z_s = input_projection(xp, xs, lw)

        hm = rmsnorm(mem_prompt.reshape(bsz_p * n_mem, d), mem_norm_g[l], BF16)
        mkv = matmul_groups(hm, w_mem_kv, l)
        mk_p = mkv[0].reshape(bsz_p, n_mem, H_M, DH_M)
        mv_p = mkv[1].reshape(bsz_p, n_mem, H_M, DH_M)
        attend_p = functools.partial(dattn_prompt, lam=lam, norm_g=diff_norm_g[l],
                                     lam_init=lam_init, out_dtype=BF16)
        xp, z4, s5_last, hg = trunk_layer(xp, z_p, lw, attend_p, mk_p, mv_p, zeros_s5, zeros_hgrn, BF16)
        outs['kp'].append(z4[G_KC].reshape(bsz_p, t_p, H_C, 2 * DH_C))
        outs['vp'].append(z4[G_VC].reshape(bsz_p, t_p, H_C, DV_C))
        outs['mkp'].append(mk_p)
        outs['mvp'].append(mv_p)
        outs['s5p'].append(s5_last)
        outs['hgp'].append(hg)

        attend_s = functools.partial(dattn_sample, cache_k=cache_k, cache_v=cache_v, layer=l,
                                     page_table=page_table, lam=lam, norm_g=diff_norm_g[l],
                                     lam_init=lam_init, out_dtype=F32)
        s5_h0 = jnp.stack([state_s5_re[l].astype(F32), state_s5_im[l].astype(F32)])
        xs, z4, s5_last, hg = trunk_layer(xs, z_s, lw, attend_s, cache_mem_k[l], cache_mem_v[l],
                                          s5_h0, state_hgrn[l], F32)
        outs['ks'].append(z4[G_KC].reshape(bsz_s, t_s, H_C, 2 * DH_C))
        outs['vs'].append(z4[G_VC].reshape(bsz_s, t_s, H_C, DV_C))
        outs['s5s'].append(s5_last)
        outs['hgs'].append(hg)

    y_prompt = rmsnorm(xp.reshape(bsz_p * t_p, d), final_norm_g, F32).reshape(bsz_p, t_p, d)
    y_sample = rmsnorm(xs.reshape(bsz_s * t_s, d), final_norm_g, F32).reshape(bsz_s, t_s, d)
    s5p = jnp.stack(outs['s5p'])
    s5s = jnp.stack(outs['s5s'])
    return (y_prompt, y_sample,
            jnp.stack(outs['kp']), jnp.stack(outs['vp']), jnp.stack(outs['ks']), jnp.stack(outs['vs']),
            jnp.stack(outs['mkp']), jnp.stack(outs['mvp']),
            s5p[:, 0], s5p[:, 1], s5s[:, 0], s5s[:, 1],
            jnp.stack(outs['hgp']), jnp.stack(outs['hgs']))
```
